```python
import math
import jax, jax.numpy as jnp
from jax import lax
import numpy as np

D_MODEL = 4096
BATCH = 1
SEQ = 16384
DEPTH = 4

GRID_W = 64
CTX_LEN = 256
ROPE_THETA = 10000.0
NORM_EPS = 1e-6
N_MIXERS = 2
N_ATTN_LAYERS = (DEPTH + 1) // 2
N_SSM_LAYERS = DEPTH // 2

ADA_RANK = 256
N_MOD = 6

MLA_HEADS = D_MODEL // 128
Q_RANK = 1536
KV_RANK = 512
QK_NOPE = 128
QK_ROPE = 64
V_HEAD = 128
Q_BLOCK = 128
MLA_IN_DIM = Q_RANK + KV_RANK + QK_ROPE

D_INNER = 2 * D_MODEL
SSM_HEAD_DIM = 64
SSM_HEADS = D_INNER // SSM_HEAD_DIM
SSM_GROUPS = 8
D_STATE = 128
D_CONV = 5
SSM_CHUNK = 128
GN = SSM_GROUPS * D_STATE
CONV_DIM = D_INNER + 2 * GN
SSM_IN_DIM = D_INNER + CONV_DIM + 2 * SSM_HEADS

N_EXPERTS = 16
N_GROUPS = 4
EXPERTS_PER_GROUP = N_EXPERTS // N_GROUPS
TOPK_GROUP = 1
TOP_K = 2
D_EXPERT = 512
D_SHARED = 512

kernel_name = "hybrid_mla_mamba2_moe_dit"

F32 = jnp.float32


def rms_norm(x, g):
    xf = x.astype(F32)
    y = xf * lax.rsqrt(jnp.mean(xf * xf, axis=-1, keepdims=True) + NORM_EPS)
    return (y * g.astype(F32)).astype(x.dtype)


def modulate(h, shift, scale):
    return h * (1 + scale) + shift


def ada_modulation(cvec, w_down, w_up, bias):
    m = (jax.nn.silu(cvec) @ w_down) @ w_up + bias
    return jnp.split(m[:, None, :], N_MOD, axis=-1)


def axial_rope(n_tokens, rot_dim):
    rows = n_tokens // GRID_W
    row = jnp.broadcast_to(jnp.arange(rows, dtype=F32)[:, None], (rows, GRID_W)).reshape(-1)
    col = jnp.broadcast_to(jnp.arange(GRID_W, dtype=F32)[None, :], (rows, GRID_W)).reshape(-1)
    n_freq = rot_dim // 4
    inv = ROPE_THETA ** (-jnp.arange(n_freq, dtype=F32) / n_freq)
    ang = jnp.concatenate([row[:, None] * inv, col[:, None] * inv], axis=-1)
    return jnp.cos(ang), jnp.sin(ang)


def apply_rope(u, cos, sin):
    uf = u.astype(F32)
    u1, u2 = jnp.split(uf, 2, axis=-1)
    return jnp.concatenate([u1 * cos - u2 * sin, u2 * cos + u1 * sin], axis=-1).astype(u.dtype)


def mla_mixer(h_lat, h_ctx, w_in, q_norm, kv_norm, w_uq, w_ukv, w_o, with_ctx_out):
    b, seq_len, _ = h_lat.shape
    cos, sin = axial_rope(seq_len, QK_ROPE)

    def queries(cq):
        q = (rms_norm(cq, q_norm) @ w_uq).reshape(*cq.shape[:2], MLA_HEADS, QK_NOPE + QK_ROPE)
        return q[..., :QK_NOPE], q[..., QK_NOPE:]

    def keys_values(ckv):
        kv = (rms_norm(ckv, kv_norm) @ w_ukv).reshape(*ckv.shape[:2], MLA_HEADS, QK_NOPE + V_HEAD)
        return kv[..., :QK_NOPE], kv[..., QK_NOPE:]

    lat = h_lat @ w_in
    qn_l, qr_l = queries(lat[..., :Q_RANK])
    qr_l = apply_rope(qr_l, cos[None, :, None], sin[None, :, None])
    kn_l, v_l = keys_values(lat[..., Q_RANK:Q_RANK + KV_RANK])
    kr_l = apply_rope(lat[..., Q_RANK + KV_RANK:], cos[None], sin[None])

    if with_ctx_out:
        ctx_p = h_ctx @ w_in
        ctx_kv = ctx_p[..., Q_RANK:]
    else:
        ctx_kv = h_ctx @ w_in[:, Q_RANK:]
    kn_c, v_c = keys_values(ctx_kv[..., :KV_RANK])
    kr_c = ctx_kv[..., KV_RANK:]

    kn = jnp.concatenate([kn_c, kn_l], axis=1)
    kr = jnp.concatenate([kr_c, kr_l], axis=1)
    v = jnp.concatenate([v_c, v_l], axis=1)
    scale = (QK_NOPE + QK_ROPE) ** -0.5

    def attend(qn, qr, kn_, kr_, v_):
        s = jnp.einsum('bqhd,bkhd->bhqk', qn, kn_) + jnp.einsum('bqhr,bkr->bhqk', qr, kr_)
        p = jax.nn.softmax(s.astype(F32) * scale, axis=-1).astype(v_.dtype)
        return jnp.einsum('bhqk,bkhd->bqhd', p, v_)

    nb = seq_len // Q_BLOCK

    def to_blocks(t):
        return jnp.moveaxis(t.reshape(b, nb, Q_BLOCK, *t.shape[2:]), 1, 0)

    o_l = lax.map(lambda qs: attend(qs[0], qs[1], kn, kr, v), (to_blocks(qn_l), to_blocks(qr_l)))
    o_l = jnp.moveaxis(o_l, 0, 1).reshape(b, seq_len, MLA_HEADS * V_HEAD) @ w_o
    if not with_ctx_out:
        return o_l, None
    qn_c, qr_c = queries(ctx_p[..., :Q_RANK])
    o_c = attend(qn_c, qr_c, kn_c, kr_c, v_c).reshape(b, -1, MLA_HEADS * V_HEAD) @ w_o
    return o_l, o_c


def dwconv_centred(u, w, bias):
    pad = w.shape[0] // 2
    out = lax.conv_general_dilated(u, w[:, None, :], window_strides=(1,), padding=[(pad, pad)],
                                   dimension_numbers=('NWC', 'WIO', 'NWC'),
                                   feature_group_count=u.shape[-1])
    return out + bias


def ssd_chunked(xs, dt, a, bm, cm, init_state, with_y):
    b, l, h, p = xs.shape
    g, n = bm.shape[2:]
    hg = h // g
    nc = l // SSM_CHUNK
    x = xs.astype(F32).reshape(b, nc, SSM_CHUNK, g, hg, p)
    dtc = dt.astype(F32).reshape(b, nc, SSM_CHUNK, g, hg)
    bc = bm.astype(F32).reshape(b, nc, SSM_CHUNK, g, n)
    cc = cm.astype(F32).reshape(b, nc, SSM_CHUNK, g, n)
    cum = jnp.cumsum(dtc * a.reshape(g, hg), axis=2)
    total = cum[:, :, -1]
    xdt = x * dtc[..., None]
    states = jnp.einsum('bcqgn,bcqghp->bcghpn', bc,
                        xdt * jnp.exp(total[:, :, None] - cum)[..., None])

    def step(s, inp):
        st, tot = inp
        return s * jnp.exp(tot)[..., None, None] + st, s

    init = init_state.astype(F32).reshape(b, g, hg, p, n)
    final, s_in = lax.scan(step, init, (jnp.moveaxis(states, 1, 0), jnp.moveaxis(total, 1, 0)))
    final = final.reshape(b, h, p, n)
    if not with_y:
        return None, final
    y_off = jnp.einsum('bcqgn,cbghpn->bcqghp', cc, s_in) * jnp.exp(cum)[..., None]
    cb = jnp.einsum('bcign,bcjgn->bcijg', cc, bc)
    seg = cum[:, :, :, None] - cum[:, :, None, :]
    mask = jnp.tril(jnp.ones((SSM_CHUNK, SSM_CHUNK), dtype=bool))[:, :, None, None]
    lmat = jnp.exp(jnp.where(mask, seg, -jnp.inf))
    y_diag = jnp.einsum('bcijgh,bcjghp->bcighp', cb[..., None] * lmat, xdt)
    return (y_diag + y_off).reshape(b, l, h, p), final


def mamba2_mixer(h_lat, h_ctx, w_in, conv_w, conv_b, dt_bias, a_log, d_skip, norm_g, w_out, with_ctx_out):
    def project(hh):
        bb, ll = hh.shape[:2]
        zxd = hh @ w_in
        z = zxd[..., :D_INNER]
        xbc = jax.nn.silu(dwconv_centred(zxd[..., D_INNER:D_INNER + CONV_DIM], conv_w, conv_b))
        dt_raw = zxd[..., D_INNER + CONV_DIM:].astype(F32)
        xs = xbc[..., :D_INNER].reshape(bb, ll, SSM_HEADS, SSM_HEAD_DIM)
        bm = xbc[..., D_INNER:D_INNER + GN].reshape(bb, ll, SSM_GROUPS, D_STATE)
        cm = xbc[..., D_INNER + GN:].reshape(bb, ll, SSM_GROUPS, D_STATE)
        dt_f = jax.nn.softplus(dt_raw[..., :SSM_HEADS] + dt_bias[0].astype(F32))
        dt_b = jax.nn.softplus(dt_raw[..., SSM_HEADS:] + dt_bias[1].astype(F32))
        return z, xs, bm, cm, dt_f, dt_b

    def flip(t):
        return jnp.flip(t, axis=1)

    a_f = -jnp.exp(a_log[0].astype(F32))
    a_b = -jnp.exp(a_log[1].astype(F32))
    b = h_lat.shape[0]
    zl, xl, bl, cl, dfl, dbl = project(h_lat)
    zc, xc, bcx, ccx, dfc, dbc = project(h_ctx)
    zero = jnp.zeros((b, SSM_HEADS, SSM_HEAD_DIM, D_STATE), F32)
    yfc, s_f = ssd_chunked(xc, dfc, a_f, bcx, ccx, zero, with_ctx_out)
    ybc, s_b = ssd_chunked(flip(xc), flip(dbc), a_b, flip(bcx), flip(ccx), zero, with_ctx_out)
    yfl, _ = ssd_chunked(xl, dfl, a_f, bl, cl, s_f, True)
    ybl, _ = ssd_chunked(flip(xl), flip(dbl), a_b, flip(bl), flip(cl), s_b, True)

    def finish(y_f, y_b_rev, xs, z):
        bb, ll = z.shape[:2]
        y = y_f + flip(y_b_rev) + d_skip.astype(F32)[:, None] * xs.astype(F32)
        y = y.reshape(bb, ll, D_INNER) * jax.nn.silu(z.astype(F32))
        yg = y.reshape(bb, ll, SSM_GROUPS, D_INNER // SSM_GROUPS)
        yg = yg * lax.rsqrt(jnp.mean(yg * yg, axis=-1, keepdims=True) + NORM_EPS)
        y = (yg.reshape(bb, ll, D_INNER) * norm_g.astype(F32)).astype(z.dtype)
        return y @ w_out

    o_l = finish(yfl, ybl, xl, zl)
    o_c = finish(yfc, ybc, xc, zc) if with_ctx_out else None
    return o_l, o_c


def moe_ffn(h, router_w, router_b, w_gate, w_up, w_down, ws_gate, ws_up, ws_down):
    bb, ll, d = h.shape
    t = h.reshape(-1, d)
    n = t.shape[0]
    scores = jax.nn.sigmoid((t @ router_w).astype(F32))
    sel = (scores + router_b.astype(F32)).reshape(n, N_GROUPS, EXPERTS_PER_GROUP)
    group_score = lax.top_k(sel, 2)[0].sum(-1)
    _, g_idx = lax.top_k(group_score, TOPK_GROUP)
    g_mask = jax.nn.one_hot(g_idx, N_GROUPS, dtype=F32).sum(-2)
    masked = jnp.where(g_mask[..., None] > 0, sel, -jnp.inf).reshape(n, N_EXPERTS)
    _, e_idx = lax.top_k(masked, TOP_K)
    w = jnp.take_along_axis(scores, e_idx, axis=-1)
    w = w / w.sum(-1, keepdims=True)
    gates = jnp.einsum('nk,nke->ne', w, jax.nn.one_hot(e_idx, N_EXPERTS, dtype=F32)).astype(t.dtype)
    hg = jnp.einsum('nd,edf->nef', t, w_gate)
    hu = jnp.einsum('nd,edf->nef', t, w_up)
    act = jax.nn.silu(hg) * hu * gates[..., None]
    y = jnp.einsum('nef,efd->nd', act, w_down)
    shared = (jax.nn.silu(t @ ws_gate) * (t @ ws_up)) @ ws_down
    return (y + shared).reshape(bb, ll, d)


def setup_inputs(seed: int = 0) -> dict:
    key = jax.random.key(seed)
    ks = jax.random.split(key, 40)

    def normal(k, shape, scale):
        return jax.random.normal(k, shape, F32) * scale

    dt0 = jnp.exp(jax.random.uniform(ks[30], (N_SSM_LAYERS, 2, SSM_HEADS), F32,
                                     math.log(1e-3), math.log(1e-1)))
    return {
        "x": normal(ks[0], (BATCH, SEQ, D_MODEL), 1.0),
        "c": normal(ks[1], (BATCH, D_MODEL), 1.0),
        "ctx": normal(ks[2], (BATCH, CTX_LEN, D_MODEL), 1.0),
        "c_ctx": normal(ks[3], (D_MODEL,), 1.0),
        "ada_down": normal(ks[4], (DEPTH, D_MODEL, ADA_RANK), D_MODEL ** -0.5),
        "ada_up": normal(ks[5], (DEPTH, ADA_RANK, N_MOD * D_MODEL), 0.5 * ADA_RANK ** -0.5),
        "ada_bias": normal(ks[6], (DEPTH, N_MOD * D_MODEL), 0.02),
        "norm1_g": 1.0 + normal(ks[7], (DEPTH, D_MODEL), 0.02),
        "norm2_g": 1.0 + normal(ks[8], (DEPTH, D_MODEL), 0.02),
        "final_norm_g": 1.0 + normal(ks[9], (D_MODEL,), 0.02),
        "mla_w_in": normal(ks[10], (N_ATTN_LAYERS, D_MODEL, MLA_IN_DIM), D_MODEL ** -0.5),
        "mla_q_norm": 1.0 + normal(ks[11], (N_ATTN_LAYERS, Q_RANK), 0.02),
        "mla_kv_norm": 1.0 + normal(ks[12], (N_ATTN_LAYERS, KV_RANK), 0.02),
        "mla_w_uq": normal(ks[13], (N_ATTN_LAYERS, Q_RANK, MLA_HEADS * (QK_NOPE + QK_ROPE)), Q_RANK ** -0.5),
        "mla_w_ukv": normal(ks[14], (N_ATTN_LAYERS, KV_RANK, MLA_HEADS * (QK_NOPE + V_HEAD)), KV_RANK ** -0.5),
        "mla_w_o": normal(ks[15], (N_ATTN_LAYERS, MLA_HEADS * V_HEAD, D_MODEL), (MLA_HEADS * V_HEAD) ** -0.5),
        "ssm_w_in": normal(ks[16], (N_SSM_LAYERS, D_MODEL, SSM_IN_DIM), D_MODEL ** -0.5),
        "ssm_conv_w": normal(ks[17], (N_SSM_LAYERS, D_CONV, CONV_DIM), D_CONV ** -0.5),
        "ssm_conv_b": normal(ks[18], (N_SSM_LAYERS, CONV_DIM), 0.02),
        "ssm_dt_bias": dt0 + jnp.log(-jnp.expm1(-dt0)),
        "ssm_a_log": jnp.log(jax.random.uniform(ks[19], (N_SSM_LAYERS, 2, SSM_HEADS), F32, 1.0, 16.0)),
        "ssm_d": 1.0 + normal(ks[20], (N_SSM_LAYERS, SSM_HEADS), 0.1),
        "ssm_norm_g": 1.0 + normal(ks[21], (N_SSM_LAYERS, D_INNER), 0.02),
        "ssm_w_out": normal(ks[22], (N_SSM_LAYERS, D_INNER, D_MODEL), D_INNER ** -0.5),
        "router_w": normal(ks[23], (D_MODEL, N_EXPERTS), D_MODEL ** -0.5),
        "router_b": normal(ks[24], (N_EXPERTS,), 0.01),
        "moe_w_gate": normal(ks[25], (DEPTH, N_EXPERTS, D_MODEL, D_EXPERT), D_MODEL ** -0.5),
        "moe_w_up": normal(ks[26], (DEPTH, N_EXPERTS, D_MODEL, D_EXPERT), D_MODEL ** -0.5),
        "moe_w_down": normal(ks[27], (DEPTH, N_EXPERTS, D_EXPERT, D_MODEL), D_EXPERT ** -0.5),
        "shared_w_gate": normal(ks[28], (DEPTH, D_MODEL, D_SHARED), D_MODEL ** -0.5),
        "shared_w_up": normal(ks[29], (DEPTH, D_MODEL, D_SHARED), D_MODEL ** -0.5),
        "shared_w_down": normal(ks[31], (DEPTH, D_SHARED, D_MODEL), D_SHARED ** -0.5),
    }


def reference(x, c, ctx, c_ctx, ada_down, ada_up, ada_bias, norm1_g, norm2_g, final_norm_g,
              mla_w_in, mla_q_norm, mla_kv_norm, mla_w_uq, mla_w_ukv, mla_w_o,
              ssm_w_in, ssm_conv_w, ssm_conv_b, ssm_dt_bias, ssm_a_log, ssm_d, ssm_norm_g, ssm_w_out,
              router_w, router_b, moe_w_gate, moe_w_up, moe_w_down,
              shared_w_gate, shared_w_up, shared_w_down):
    n_ctx = ctx.shape[1]
    h_c = ctx
    for i in range(DEPTH):
        last = i == DEPTH - 1
        sh1, sc1, g1, sh2, sc2, g2 = ada_modulation(c, ada_down[i], ada_up[i], ada_bias[i])
        csh1, csc1, cg1, csh2, csc2, cg2 = ada_modulation(c_ctx[None], ada_down[i], ada_up[i], ada_bias[i])
        hl = modulate(rms_norm(x, norm1_g[i]), sh1, sc1)
        hc = modulate(rms_norm(h_c, norm1_g[i]), csh1, csc1)
        j = i // N_MIXERS
        if i % N_MIXERS == 0:
            o_l, o_c = mla_mixer(hl, hc, mla_w_in[j], mla_q_norm[j], mla_kv_norm[j],
                                 mla_w_uq[j], mla_w_ukv[j], mla_w_o[j], not last)
        else:
            o_l, o_c = mamba2_mixer(hl, hc, ssm_w_in[j], ssm_conv_w[j], ssm_conv_b[j], ssm_dt_bias[j],
                                    ssm_a_log[j], ssm_d[j], ssm_norm_g[j], ssm_w_out[j], not last)
        x = x + g1 * o_l
        hl2 = modulate(rms_norm(x, norm2_g[i]), sh2, sc2)
        if last:
            x = x + g2 * moe_ffn(hl2, router_w, router_b, moe_w_gate[i], moe_w_up[i], moe_w_down[i],
                                 shared_w_gate[i], shared_w_up[i], shared_w_down[i])
        else:
            h_c = h_c + cg1 * o_c
            hc2 = modulate(rms_norm(h_c, norm2_g[i]), csh2, csc2)
            y = moe_ffn(jnp.concatenate([hc2, hl2], axis=1), router_w, router_b,
                        moe_w_gate[i], moe_w_up[i], moe_w_down[i],
                        shared_w_gate[i], shared_w_up[i], shared_w_down[i])
            h_c = h_c + cg2 * y[:, :n_ctx]
            x = x + g2 * y[:, n_ctx:]
    return rms_norm(x, final_norm_g)
```

```python
import functools
import math

import jax
import jax.numpy as jnp
from jax import lax
from jax.experimental import pallas as pl
from jax.experimental.pallas import tpu as pltpu

F32 = jnp.float32
BF16 = jnp.bfloat16

GRID_W = 64
ROPE_THETA = 10000.0
NORM_EPS = 1e-6
N_MOD = 6
QK_NOPE = 128
QK_ROPE = 64
V_HEAD = 128
SSM_HEAD_DIM = 64
SSM_GROUPS = 8
D_STATE = 128
SSM_CHUNK = 128
N_GROUPS = 4
TOP_K = 2

LANE = 128
VMEM_LIMIT_BYTES = 56 * 1024 * 1024


def _cparams(*sem):
    return pltpu.CompilerParams(dimension_semantics=sem, vmem_limit_bytes=VMEM_LIMIT_BYTES)


def _pick(n, prefs):
    for p in prefs:
        if p <= n and n % p == 0:
            return p
    return n


def _row_select(row0, tm, n_ctx, ref):
    row = row0 + lax.broadcasted_iota(jnp.int32, (tm, 1), 0)
    return jnp.where(row < n_ctx, ref[0:1, :], ref[1:2, :])


def _silu(v):
    return v * (1.0 / (1.0 + jnp.exp(-v)))


def _ada_kernel(c_ref, wd_ref, wu_ref, b_ref, o_ref):
    cv = c_ref[...]
    h = jnp.dot(_silu(cv), wd_ref[...], preferred_element_type=F32, precision=lax.Precision.HIGHEST)
    o_ref[...] = jnp.dot(h, wu_ref[...], preferred_element_type=F32,
                         precision=lax.Precision.HIGHEST) + b_ref[...]


def ada_modulation(cvecs, w_down, w_up, bias):
    depth, d, r = w_down.shape
    n = w_up.shape[-1]
    tn = _pick(n, (2048, 1024, 512, 256, 128))
    return pl.pallas_call(
        _ada_kernel,
        grid=(depth, n // tn),
        in_specs=[pl.BlockSpec((8, d), lambda l, j: (0, 0)),
                  pl.BlockSpec((None, d, r), lambda l, j: (l, 0, 0)),
                  pl.BlockSpec((None, r, tn), lambda l, j: (l, 0, j)),
                  pl.BlockSpec((None, 1, tn), lambda l, j: (l, 0, j))],
        out_specs=pl.BlockSpec((None, 8, tn), lambda l, j: (l, 0, j)),
        out_shape=jax.ShapeDtypeStruct((depth, 8, n), F32),
        compiler_params=_cparams("parallel", "parallel"),
    )(cvecs, w_down, w_up, bias.reshape(depth, 1, n))


def _norm_mod_kernel(x_ref, g_ref, sh_ref, sc_ref, o_ref, *, tm, n_ctx):
    x = x_ref[...]
    y = x * lax.rsqrt(jnp.mean(x * x, axis=-1, keepdims=True) + NORM_EPS) * g_ref[...]
    row0 = pl.program_id(0) * tm
    sh = _row_select(row0, tm, n_ctx, sh_ref)
    sc = _row_select(row0, tm, n_ctx, sc_ref)
    o_ref[...] = (y * (1.0 + sc) + sh).astype(o_ref.dtype)


def _norm_mod_router_kernel(x_ref, g_ref, sh_ref, sc_ref, rw_ref, o_ref, lt_ref, *, tm, n_ctx):
    x = x_ref[...]
    y = x * lax.rsqrt(jnp.mean(x * x, axis=-1, keepdims=True) + NORM_EPS) * g_ref[...]
    row0 = pl.program_id(0) * tm
    sh = _row_select(row0, tm, n_ctx, sh_ref)
    sc = _row_select(row0, tm, n_ctx, sc_ref)
    t = y * (1.0 + sc) + sh
    o_ref[...] = t.astype(o_ref.dtype)
    lt_ref[...] = lax.dot_general(rw_ref[...], t, (((1,), (1,)), ((), ())),
                                  preferred_element_type=F32, precision=lax.Precision.HIGHEST)


def norm_modulate(x, g, shift2, scale2, n_ctx, router_wt=None):
    t, d = x.shape
    tm = _pick(t, (256, 128))
    vec = pl.BlockSpec((1, d), lambda i: (0, 0))
    two = pl.BlockSpec((2, d), lambda i: (0, 0))
    xs = pl.BlockSpec((tm, d), lambda i: (i, 0))
    if router_wt is None:
        return pl.pallas_call(
            functools.partial(_norm_mod_kernel, tm=tm, n_ctx=n_ctx),
            grid=(t // tm,), in_specs=[xs, vec, two, two], out_specs=xs,
            out_shape=jax.ShapeDtypeStruct((t, d), BF16),
            compiler_params=_cparams("parallel"),
        )(x, g.reshape(1, d), shift2, scale2)
    e = router_wt.shape[0]
    return pl.pallas_call(
        functools.partial(_norm_mod_router_kernel, tm=tm, n_ctx=n_ctx),
        grid=(t // tm,),
        in_specs=[xs, vec, two, two, pl.BlockSpec((e, d), lambda i: (0, 0))],
        out_specs=[xs, pl.BlockSpec((e, tm), lambda i: (0, i))],
        out_shape=[jax.ShapeDtypeStruct((t, d), BF16), jax.ShapeDtypeStruct((e, t), F32)],
        compiler_params=_cparams("parallel"),
    )(x, g.reshape(1, d), shift2, scale2, router_wt)


def _final_norm_kernel(x_ref, g_ref, o_ref):
    x = x_ref[...]
    o_ref[...] = x * lax.rsqrt(jnp.mean(x * x, axis=-1, keepdims=True) + NORM_EPS) * g_ref[...]


def final_norm(x, g, row_start):
    t, d = x.shape
    tm = _pick(t, (256, 128))
    assert row_start % tm == 0
    off = row_start // tm
    return pl.pallas_call(
        _final_norm_kernel, grid=((t - row_start) // tm,),
        in_specs=[pl.BlockSpec((tm, d), lambda i: (i + off, 0)), pl.BlockSpec((1, d), lambda i: (0, 0))],
        out_specs=pl.BlockSpec((tm, d), lambda i: (i, 0)),
        out_shape=jax.ShapeDtypeStruct((t - row_start, d), F32),
        compiler_params=_cparams("parallel"),
    )(x, g.reshape(1, d))


def _mm_kernel(a_ref, b_ref, o_ref):
    o_ref[...] = jnp.dot(a_ref[...], b_ref[...], preferred_element_type=F32).astype(o_ref.dtype)


def _mm_tiles(m, k, n):
    tm = _pick(m, (1280, 1024, 640, 512, 256, 128))
    if k > 4096:
        tm = _pick(m, (640, 512, 256, 128))
    tn = _pick(n, (512, 256, 128))
    return tm, tn


def matmul(a, b, out_dtype):
    m, k = a.shape
    n = b.shape[1]
    tm, tn = _mm_tiles(m, k, n)
    return pl.pallas_call(
        _mm_kernel, grid=(m // tm, n // tn),
        in_specs=[pl.BlockSpec((tm, k), lambda i, j: (i, 0)), pl.BlockSpec((k, tn), lambda i, j: (0, j))],
        out_specs=pl.BlockSpec((tm, tn), lambda i, j: (i, j)),
        out_shape=jax.ShapeDtypeStruct((m, n), out_dtype),
        compiler_params=_cparams("parallel", "parallel"),
    )(a, b)


def _mm_res_kernel(a_ref, b_ref, r_ref, g_ref, o_ref, *, tm, n_ctx):
    acc = jnp.dot(a_ref[...], b_ref[...], preferred_element_type=F32)
    gate = _row_select(pl.program_id(0) * tm, tm, n_ctx, g_ref)
    o_ref[...] = r_ref[...] + gate * acc


def matmul_residual(a, b, resid, gate2, n_ctx):
    m, k = a.shape
    n = b.shape[1]
    tm, tn = _mm_tiles(m, k, n)
    return pl.pallas_call(
        functools.partial(_mm_res_kernel, tm=tm, n_ctx=n_ctx), grid=(m // tm, n // tn),
        in_specs=[pl.BlockSpec((tm, k), lambda i, j: (i, 0)), pl.BlockSpec((k, tn), lambda i, j: (0, j)),
                  pl.BlockSpec((tm, tn), lambda i, j: (i, j)), pl.BlockSpec((2, tn), lambda i, j: (0, j))],
        out_specs=pl.BlockSpec((tm, tn), lambda i, j: (i, j)),
        out_shape=jax.ShapeDtypeStruct((m, n), F32),
        compiler_params=_cparams("parallel", "parallel"),
    )(a, b, resid, gate2)


def _rope_tables(n_ctx, n_lat):
    rows = n_lat // GRID_W
    row = jnp.broadcast_to(jnp.arange(rows, dtype=F32)[:, None], (rows, GRID_W)).reshape(-1)
    col = jnp.broadcast_to(jnp.arange(GRID_W, dtype=F32)[None, :], (rows, GRID_W)).reshape(-1)
    n_freq = QK_ROPE // 4
    inv = ROPE_THETA ** (-jnp.arange(n_freq, dtype=F32) / n_freq)
    ang = jnp.concatenate([row[:, None] * inv, col[:, None] * inv], axis=-1)
    ang = jnp.concatenate([jnp.zeros((n_ctx, QK_ROPE // 2), F32), ang], axis=0)
    return jnp.tile(jnp.cos(ang), (1, 4)), jnp.tile(jnp.sin(ang), (1, 4))


def _rot_cols(w):
    h = QK_ROPE // 2
    return jnp.concatenate([-w[..., h:], w[..., :h]], axis=-1)


def _mla_lat_post_kernel(lat_ref, qg_ref, kg_ref, cc_ref, ss_ref, cq_ref, ckv_ref, kr_ref, *, q_rank, kv_rank):
    cq = lat_ref[:, :q_rank]
    cq_ref[...] = (cq * lax.rsqrt(jnp.mean(cq * cq, axis=-1, keepdims=True) + NORM_EPS)
                   * qg_ref[...]).astype(cq_ref.dtype)
    ckv = lat_ref[:, q_rank:q_rank + kv_rank]
    ckv_ref[...] = (ckv * lax.rsqrt(jnp.mean(ckv * ckv, axis=-1, keepdims=True) + NORM_EPS)
                    * kg_ref[...]).astype(ckv_ref.dtype)
    base = q_rank + kv_rank
    cc = cc_ref[...]
    ss = ss_ref[...]
    lo = lat_ref[:, base:base + LANE] * cc + lat_ref[:, base + LANE:base + 2 * LANE] * ss
    hi = lat_ref[:, base + 2 * LANE:base + 3 * LANE] * cc + lat_ref[:, base + 3 * LANE:base + 4 * LANE] * ss
    kr_ref[:, :LANE] = lo.astype(kr_ref.dtype)
    kr_ref[:, LANE:] = hi.astype(kr_ref.dtype)


def mla_lat_post(lat, q_norm, kv_norm, cc, ss):
    t = lat.shape[0]
    q_rank, kv_rank = q_norm.shape[0], kv_norm.shape[0]
    tm = _pick(t, (256, 128))
    row = lambda w: pl.BlockSpec((tm, w), lambda i: (i, 0))
    return pl.pallas_call(
        functools.partial(_mla_lat_post_kernel, q_rank=q_rank, kv_rank=kv_rank), grid=(t // tm,),
        in_specs=[row(lat.shape[1]), pl.BlockSpec((1, q_rank), lambda i: (0, 0)),
                  pl.BlockSpec((1, kv_rank), lambda i: (0, 0)), row(LANE), row(LANE)],
        out_specs=[row(q_rank), row(kv_rank), row(2 * LANE)],
        out_shape=[jax.ShapeDtypeStruct((t, q_rank), BF16), jax.ShapeDtypeStruct((t, kv_rank), BF16),
                   jax.ShapeDtypeStruct((t, 2 * LANE), BF16)],
        compiler_params=_cparams("parallel"),
    )(lat, q_norm.reshape(1, -1), kv_norm.reshape(1, -1), cc, ss)


def _q_proj_kernel(a_ref, b_ref, cc_ref, ss_ref, qn_ref, qr_ref):
    acc = jnp.dot(a_ref[...], b_ref[...], preferred_element_type=F32)
    qn_ref[...] = acc[:, :2 * LANE].astype(qn_ref.dtype)
    qr_ref[...] = (acc[:, 2 * LANE:3 * LANE] * cc_ref[...]
                   + acc[:, 3 * LANE:4 * LANE] * ss_ref[...]).astype(qr_ref.dtype)


def q_proj(cqn, w_pairs, cc, ss):
    t, k = cqn.shape
    n_pairs = w_pairs.shape[1] // (4 * LANE)
    tm = _pick(t, (1280, 1024, 640, 512, 256, 128))
    return pl.pallas_call(
        _q_proj_kernel, grid=(t // tm, n_pairs),
        in_specs=[pl.BlockSpec((tm, k), lambda i, j: (i, 0)), pl.BlockSpec((k, 4 * LANE), lambda i, j: (0, j)),
                  pl.BlockSpec((tm, LANE), lambda i, j: (i, 0)), pl.BlockSpec((tm, LANE), lambda i, j: (i, 0))],
        out_specs=[pl.BlockSpec((tm, 2 * LANE), lambda i, j: (i, j)), pl.BlockSpec((tm, LANE), lambda i, j: (i, j))],
        out_shape=[jax.ShapeDtypeStruct((t, n_pairs * 2 * LANE), BF16),
                   jax.ShapeDtypeStruct((t, n_pairs * LANE), BF16)],
        compiler_params=_cparams("parallel", "parallel"),
    )(cqn, w_pairs, cc, ss)


def _attn_kernel(qn_ref, qr_ref, kn_ref, kr_ref, v_ref, o_ref, *, tq, tk, n_ctx, n_all):
    q = jnp.concatenate([qn_ref[...], qr_ref[...]], axis=1)

    def chunk(off, size, carry):
        m, l, acc = carry
        k = jnp.concatenate([kn_ref[pl.ds(off, size), :], kr_ref[pl.ds(off, size), :]], axis=1)
        s = lax.dot_general(q, k, (((1,), (1,)), ((), ())), preferred_element_type=F32)
        m_new = jnp.maximum(m, jnp.max(s, axis=1, keepdims=True))
        alpha = jnp.exp(m - m_new)
        p = jnp.exp(s - m_new)
        l = alpha * l + jnp.sum(p, axis=1, keepdims=True)
        acc = alpha * acc + jnp.dot(p.astype(BF16), v_ref[pl.ds(off, size), :], preferred_element_type=F32)
        return m_new, l, acc

    init = (jnp.full((tq, 1), -jnp.inf, F32), jnp.zeros((tq, 1), F32), jnp.zeros((tq, V_HEAD), F32))
    is_ctx = pl.program_id(1) * tq < n_ctx

    @pl.when(is_ctx)
    def _():
        _, l, acc = chunk(0, n_ctx, init)
        o_ref[...] = (acc / l).astype(o_ref.dtype)

    @pl.when(jnp.logical_not(is_ctx))
    def _():
        def body(c, carry):
            return chunk(pl.multiple_of(c * tk, tk), tk, carry)
        _, l, acc = lax.fori_loop(0, n_all // tk, body, init)
        o_ref[...] = (acc / l).astype(o_ref.dtype)


def attention(qn, qr, kv, kr2, n_ctx):
    t = qn.shape[0]
    heads = qn.shape[1] // QK_NOPE
    tq = _pick(n_ctx, (256, 128))
    tk = _pick(t, (1280, 640, 512, 256, 128))
    assert n_ctx % tq == 0 and t % tq == 0
    return pl.pallas_call(
        functools.partial(_attn_kernel, tq=tq, tk=tk, n_ctx=n_ctx, n_all=t),
        grid=(heads, t // tq),
        in_specs=[pl.BlockSpec((tq, LANE), lambda h, i: (i, h)),
                  pl.BlockSpec((tq, LANE), lambda h, i: (i, h // 2)),
                  pl.BlockSpec((t, LANE), lambda h, i: (0, h)),
                  pl.BlockSpec((t, LANE), lambda h, i: (0, h % 2)),
                  pl.BlockSpec((t, LANE), lambda h, i: (0, heads + h))],
        out_specs=pl.BlockSpec((tq, LANE), lambda h, i: (i, h)),
        out_shape=jax.ShapeDtypeStruct((t, heads * V_HEAD), BF16),
        compiler_params=_cparams("parallel", "arbitrary"),
    )(qn, qr, kv, kr2, kv)


def mla_layer(x, hn, p, gate2, n_ctx, cc, ss):
    heads = p["heads"]
    lat = matmul(hn, p["w_in"], F32)
    cqn, ckvn, kr2 = mla_lat_post(lat, p["q_norm"], p["kv_norm"], cc, ss)
    qn, qr = q_proj(cqn, p["w_uq"], cc, ss)
    kv = matmul(ckvn, p["w_ukv"], BF16)
    o = attention(qn, qr, kv, kr2, n_ctx)
    return matmul_residual(o, p["w_o"], x, gate2, n_ctx)


def prep_mla_weights(w_in, q_norm, kv_norm, w_uq, w_ukv, w_o):
    d = w_in.shape[0]
    heads = d // 128
    q_rank, kv_rank = q_norm.shape[0], kv_norm.shape[0]
    z64 = jnp.zeros((d, QK_ROPE), F32)
    wkr = w_in[:, q_rank + kv_rank:]
    wkr_rot = _rot_cols(wkr)
    w_in_p = jnp.concatenate([w_in[:, :q_rank + kv_rank], wkr, z64, wkr_rot, z64, z64, wkr, z64, wkr_rot], axis=1)
    scale = (QK_NOPE + QK_ROPE) ** -0.5
    wq = (w_uq * scale).reshape(q_rank, heads, QK_NOPE + QK_ROPE)
    nope = wq[..., :QK_NOPE].reshape(q_rank, heads // 2, 2 * QK_NOPE)
    rope = wq[..., QK_NOPE:]
    rope_a = rope.reshape(q_rank, heads // 2, 2 * QK_ROPE)
    rope_b = _rot_cols(rope).reshape(q_rank, heads // 2, 2 * QK_ROPE)
    w_uq_p = jnp.concatenate([nope, rope_a, rope_b], axis=-1).reshape(q_rank, heads // 2 * 4 * LANE)
    wkv = w_ukv.reshape(kv_rank, heads, QK_NOPE + V_HEAD)
    w_ukv_p = jnp.concatenate([wkv[..., :QK_NOPE].reshape(kv_rank, heads * QK_NOPE),
                               wkv[..., QK_NOPE:].reshape(kv_rank, heads * V_HEAD)], axis=1)
    return dict(heads=heads, w_in=w_in_p.astype(BF16), q_norm=q_norm, kv_norm=kv_norm,
                w_uq=w_uq_p.astype(BF16), w_ukv=w_ukv_p.astype(BF16), w_o=w_o.astype(BF16))


def _conv_silu_kernel(prev_ref, cur_ref, next_ref, w_ref, b_ref, o_ref, win_ref, *, tm, n_ctx, n_tiles):
    i = pl.program_id(0)
    row0 = i * tm
    has_prev = jnp.logical_and(i > 0, row0 != n_ctx)
    has_next = jnp.logical_and(i < n_tiles - 1, row0 + tm != n_ctx)
    win_ref[0:8, :] = jnp.where(has_prev, prev_ref[...].astype(F32), 0.0)
    win_ref[8:8 + tm, :] = cur_ref[...].astype(F32)
    win_ref[8 + tm:16 + tm, :] = jnp.where(has_next, next_ref[...].astype(F32), 0.0)
    pad = w_ref.shape[0] // 2
    acc = jnp.zeros(o_ref.shape, F32) + b_ref[...]
    for k in range(w_ref.shape[0]):
        acc = acc + w_ref[k:k + 1, :] * win_ref[pl.ds(8 - pad + k, tm), :]
    o_ref[...] = _silu(acc).astype(o_ref.dtype)


def conv_silu(u, w, b, n_ctx):
    t, c = u.shape
    tm = _pick(n_ctx, (256, 128))
    tn = _pick(c, (2048, 1024, 512, 256, 128))
    n_tiles = t // tm
    r8 = tm // 8
    nb8 = t // 8
    return pl.pallas_call(
        functools.partial(_conv_silu_kernel, tm=tm, n_ctx=n_ctx, n_tiles=n_tiles),
        grid=(n_tiles, c // tn),
        in_specs=[pl.BlockSpec((8, tn), lambda i, j: (jnp.maximum(i * r8 - 1, 0), j)),
                  pl.BlockSpec((tm, tn), lambda i, j: (i, j)),
                  pl.BlockSpec((8, tn), lambda i, j: (jnp.minimum((i + 1) * r8, nb8 - 1), j)),
                  pl.BlockSpec((w.shape[0], tn), lambda i, j: (0, j)),
                  pl.BlockSpec((1, tn), lambda i, j: (0, j))],
        out_specs=pl.BlockSpec((tm, tn), lambda i, j: (i, j)),
        out_shape=jax.ShapeDtypeStruct((t, c), BF16),
        scratch_shapes=[pltpu.VMEM((tm + 16, tn), F32)],
        compiler_params=_cparams("parallel", "parallel"),
    )(u, u, u, w, b.reshape(1, c))


def _dt_prep_kernel(raw_ref, bias_ref, a_ref, dtf_ref, cf_ref, dtb_ref, cb_ref, *, nh, q):
    raw = raw_ref[...]
    r = lax.broadcasted_iota(jnp.int32, (q, q), 0)
    c = lax.broadcasted_iota(jnp.int32, (q, q), 1)
    for d, (dt_ref, c_ref) in enumerate(((dtf_ref, cf_ref), (dtb_ref, cb_ref))):
        v = raw[:, d * nh:(d + 1) * nh] + bias_ref[d:d + 1, :]
        dt = jnp.maximum(v, 0.0) + jnp.log1p(jnp.exp(-jnp.abs(v)))
        da = dt * a_ref[d:d + 1, :]
        tri = jnp.where(r >= c, 1.0, 0.0) if d == 0 else jnp.where(r <= c, 1.0, 0.0)
        cum = jnp.dot(tri.astype(F32), da, preferred_element_type=F32, precision=lax.Precision.HIGHEST)
        dt_ref[...] = dt.T
        c_ref[...] = cum.T


def dt_prep(dt_raw, dt_bias, a_log):
    t = dt_raw.shape[0]
    nh = dt_raw.shape[1] // 2
    q = SSM_CHUNK
    a = -jnp.exp(a_log.astype(F32))
    out = jax.ShapeDtypeStruct((nh, t), F32)
    ob = pl.BlockSpec((nh, q), lambda i: (0, i))
    return pl.pallas_call(
        functools.partial(_dt_prep_kernel, nh=nh, q=q), grid=(t // q,),
        in_specs=[pl.BlockSpec((q, 2 * nh), lambda i: (i, 0)), pl.BlockSpec((2, nh), lambda i: (0, 0)),
                  pl.BlockSpec((2, nh), lambda i: (0, 0))],
        out_specs=[ob, ob, ob, ob], out_shape=[out, out, out, out],
        compiler_params=_cparams("parallel"),
    )(dt_raw, dt_bias.astype(F32), a)


def _ssd_kernel(x_ref, b_ref, c_ref, dt_ref, cum_ref, y_ref, s_ref, *, hg, reverse):
    q = SSM_CHUNK
    n_pairs = hg // 2

    @pl.when(pl.program_id(1) == 0)
    def _():
        s_ref[...] = jnp.zeros(s_ref.shape, s_ref.dtype)

    bmat = b_ref[...].astype(F32)
    cmat = c_ref[...].astype(F32)
    cb = lax.dot_general(c_ref[...], b_ref[...], (((1,), (1,)), ((), ())), preferred_element_type=F32)
    bt = bmat.T
    ii = lax.broadcasted_iota(jnp.int32, (q, q), 0)
    jj = lax.broadcasted_iota(jnp.int32, (q, q), 1)
    mask = (jj >= ii) if reverse else (ii >= jj)
    lane_lo = lax.broadcasted_iota(jnp.int32, (q, 2 * SSM_HEAD_DIM), 1) < SSM_HEAD_DIM
    last = 0 if reverse else q - 1

    for pr in range(n_pairs):
        xp = x_ref[:, pr * LANE:(pr + 1) * LANE]
        sp = s_ref[:, pr * LANE:(pr + 1) * LANE]
        rhs = jnp.concatenate([xp, sp.astype(BF16)], axis=0)
        ys, ss, decs = [], [], []
        for hh in range(2):
            h = pr * 2 + hh
            crow = jnp.broadcast_to(cum_ref[h:h + 1, :], (q, q))
            ccol = crow.T
            dtrow = dt_ref[h:h + 1, :]
            seg = jnp.where(mask, ccol - crow, -jnp.inf)
            mm = cb * jnp.exp(seg) * dtrow
            coff = cmat * jnp.exp(ccol)
            lhs = jnp.concatenate([mm.astype(BF16), coff.astype(BF16)], axis=1)
            ys.append(jnp.dot(lhs, rhs, preferred_element_type=F32))
            tot = cum_ref[h:h + 1, last:last + 1]
            wrow = jnp.exp(tot - cum_ref[h:h + 1, :]) * dtrow
            ss.append(jnp.dot((bt * wrow).astype(BF16), xp, preferred_element_type=F32))
            decs.append(jnp.exp(tot))
        y_ref[:, pr * LANE:(pr + 1) * LANE] = jnp.where(lane_lo, ys[0], ys[1]).astype(y_ref.dtype)
        dec = jnp.where(lane_lo[0:1, :], decs[0], decs[1])
        s_ref[:, pr * LANE:(pr + 1) * LANE] = sp * dec + jnp.where(lane_lo, ss[0], ss[1])


def ssd_scan(xbc, dt_t, cum_t, d_inner, n_ctx, reverse):
    t = xbc.shape[0]
    q = SSM_CHUNK
    nh = d_inner // SSM_HEAD_DIM
    hg = nh // SSM_GROUPS
    gw = hg * SSM_HEAD_DIM
    assert gw % LANE == 0 and hg % 8 == 0
    nc = t // q
    ncc = n_ctx // q
    xoff = d_inner // LANE
    if reverse:
        cidx = lambda s: jnp.where(s < ncc, ncc - 1 - s, nc - 1 + ncc - s)
    else:
        cidx = lambda s: s
    return pl.pallas_call(
        functools.partial(_ssd_kernel, hg=hg, reverse=reverse),
        grid=(SSM_GROUPS, nc),
        in_specs=[pl.BlockSpec((q, gw), lambda g, s: (cidx(s), g)),
                  pl.BlockSpec((q, D_STATE), lambda g, s: (cidx(s), xoff + g)),
                  pl.BlockSpec((q, D_STATE), lambda g, s: (cidx(s), xoff + SSM_GROUPS + g)),
                  pl.BlockSpec((hg, q), lambda g, s: (g, cidx(s))),
                  pl.BlockSpec((hg, q), lambda g, s: (g, cidx(s)))],
        out_specs=pl.BlockSpec((q, gw), lambda g, s: (cidx(s), g)),
        out_shape=jax.ShapeDtypeStruct((t, d_inner), F32),
        scratch_shapes=[pltpu.VMEM((D_STATE, gw), F32)],
        compiler_params=_cparams("parallel", "arbitrary"),
    )(xbc, xbc, xbc, dt_t, cum_t)


def _ssm_finish_kernel(yf_ref, yb_ref, x_ref, z_ref, d_ref, g_ref, o_ref, *, n_groups):
    z = z_ref[...].astype(F32)
    y = (yf_ref[...] + yb_ref[...] + d_ref[...] * x_ref[...].astype(F32)) * _silu(z)
    gw = y.shape[1] // n_groups
    for g in range(n_groups):
        yg = y[:, g * gw:(g + 1) * gw]
        yn = yg * lax.rsqrt(jnp.mean(yg * yg, axis=-1, keepdims=True) + NORM_EPS)
        o_ref[:, g * gw:(g + 1) * gw] = (yn * g_ref[:, g * gw:(g + 1) * gw]).astype(o_ref.dtype)


def ssm_finish(yf, yb, xbc, z, d_exp, norm_g):
    t, di = yf.shape
    tm = _pick(t, (128,))
    row = pl.BlockSpec((tm, di), lambda i: (i, 0))
    vec = pl.BlockSpec((1, di), lambda i: (0, 0))
    return pl.pallas_call(
        functools.partial(_ssm_finish_kernel, n_groups=SSM_GROUPS), grid=(t // tm,),
        in_specs=[row, row, row, row, vec, vec], out_specs=row,
        out_shape=jax.ShapeDtypeStruct((t, di), BF16),
        compiler_params=_cparams("parallel"),
    )(yf, yb, xbc, z, d_exp, norm_g.reshape(1, di))


def mamba_layer(x, hn, p, gate2, n_ctx):
    di = p["d_inner"]
    z = matmul(hn, p["w_z"], BF16)
    xbc_raw = matmul(hn, p["w_xbc"], F32)
    dt_raw = matmul(hn, p["w_dt"], F32)
    xbc = conv_silu(xbc_raw, p["conv_w"], p["conv_b"], n_ctx)
    dtf, cf, dtb, cb = dt_prep(dt_raw, p["dt_bias"], p["a_log"])
    yf = ssd_scan(xbc, dtf, cf, di, n_ctx, reverse=False)
    yb = ssd_scan(xbc, dtb, cb, di, n_ctx, reverse=True)
    yn = ssm_finish(yf, yb, xbc, z, p["d_exp"], p["norm_g"])
    return matmul_residual(yn, p["w_out"], x, gate2, n_ctx)


def prep_mamba_weights(w_in, conv_w, conv_b, dt_bias, a_log, d_skip, norm_g, w_out):
    di = norm_g.shape[0]
    conv_dim = conv_w.shape[1]
    nh = d_skip.shape[0]
    w_dt = w_in[:, di + conv_dim:]
    pad = (-w_dt.shape[1]) % LANE
    if pad:
        w_dt = jnp.concatenate([w_dt, jnp.zeros((w_in.shape[0], pad), F32)], axis=1)
    return dict(d_inner=di, w_z=w_in[:, :di].astype(BF16), w_xbc=w_in[:, di:di + conv_dim].astype(BF16),
                w_dt=w_dt.astype(BF16), conv_w=conv_w, conv_b=conv_b, dt_bias=dt_bias, a_log=a_log,
                d_exp=jnp.repeat(d_skip.astype(F32), SSM_HEAD_DIM).reshape(1, di), norm_g=norm_g,
                w_out=w_out.astype(BF16), n_heads=nh)


def _gating_kernel(lt_ref, b_ref, g_ref, *, n_experts):
    epg = n_experts // N_GROUPS
    s = [1.0 / (1.0 + jnp.exp(-lt_ref[e])) for e in range(n_experts)]
    sel = [s[e] + b_ref[e] for e in range(n_experts)]
    best_val, best_idx = None, None
    for g in range(N_GROUPS):
        mem = sel[g * epg:(g + 1) * epg]
        gs = None
        for a in range(epg):
            for b in range(a + 1, epg):
                pair = mem[a] + mem[b]
                gs = pair if gs is None else jnp.maximum(gs, pair)
        if g == 0:
            best_val, best_idx = gs, jnp.zeros(gs.shape, jnp.int32)
        else:
            better = gs > best_val
            best_val = jnp.where(better, gs, best_val)
            best_idx = jnp.where(better, g, best_idx)
    chosen, denom = [], None
    for e in range(n_experts):
        g = e // epg
        rank = jnp.zeros(best_idx.shape, jnp.int32)
        for m in range(g * epg, (g + 1) * epg):
            if m == e:
                continue
            ahead = (sel[m] > sel[e]) if m > e else (sel[m] >= sel[e])
            rank = rank + ahead.astype(jnp.int32)
        pick = jnp.logical_and(best_idx == g, rank < TOP_K)
        w = jnp.where(pick, s[e], 0.0)
        chosen.append(w)
        denom = w if denom is None else denom + w
    inv = 1.0 / denom
    for e in range(n_experts):
        g_ref[e] = chosen[e] * inv


def gating(logits_t, router_b):
    e, t = logits_t.shape
    r = t // LANE
    lt3 = logits_t.reshape(e, r, LANE)
    out = pl.pallas_call(
        functools.partial(_gating_kernel, n_experts=e), grid=(1,),
        in_specs=[pl.BlockSpec((e, r, LANE), lambda i: (0, 0, 0)),
                  pl.BlockSpec(memory_space=pltpu.SMEM)],
        out_specs=pl.BlockSpec((e, r, LANE), lambda i: (0, 0, 0)),
        out_shape=jax.ShapeDtypeStruct((e, r, LANE), F32),
        compiler_params=_cparams("arbitrary"),
    )(lt3, router_b.astype(F32))
    return out.reshape(e, t)


def _ffn_kernel(x_ref, wg_ref, wu_ref, wd_ref, gates_ref, g2_ref, res_hbm, o_ref, sem, *, tm, n_ctx):
    i = pl.program_id(0)
    e = pl.program_id(1)
    f = pl.program_id(2)

    @pl.when(jnp.logical_and(e == 0, f == 0))
    def _():
        cp = pltpu.make_async_copy(res_hbm.at[pl.ds(i * tm, tm), :], o_ref, sem)
        cp.start()
        cp.wait()

    x = x_ref[...]
    hgate = jnp.dot(x, wg_ref[...], preferred_element_type=F32)
    hup = jnp.dot(x, wu_ref[...], preferred_element_type=F32)
    gts = gates_ref[...]
    lane = lax.broadcasted_iota(jnp.int32, gts.shape, 1)
    gcol = jnp.sum(jnp.where(lane == e, gts, 0.0), axis=1, keepdims=True)
    act = (_silu(hgate) * hup * gcol).astype(BF16)
    g2 = _row_select(i * tm, tm, n_ctx, g2_ref)
    d = o_ref.shape[1]
    tc = min(d, 1024)
    for c0 in range(0, d, tc):
        y = jnp.dot(act, wd_ref[:, c0:c0 + tc], preferred_element_type=F32)
        o_ref[:, c0:c0 + tc] += g2[:, c0:c0 + tc] * y


def gated_ffn(hn, w_gate, w_up, w_down, gates, gate2, resid, n_ctx):
    t, d = hn.shape
    n_e, _, df = w_gate.shape
    tm = _pick(t, (640, 512, 256, 128))
    tf = _pick(df, (256, 128))
    return pl.pallas_call(
        functools.partial(_ffn_kernel, tm=tm, n_ctx=n_ctx),
        grid=(t // tm, n_e, df // tf),
        in_specs=[pl.BlockSpec((tm, d), lambda i, e, f: (i, 0)),
                  pl.BlockSpec((None, d, tf), lambda i, e, f: (e, 0, f)),
                  pl.BlockSpec((None, d, tf), lambda i, e, f: (e, 0, f)),
                  pl.BlockSpec((None, tf, d), lambda i, e, f: (e, f, 0)),
                  pl.BlockSpec((tm, n_e), lambda i, e, f: (i, 0)),
                  pl.BlockSpec((2, d), lambda i, e, f: (0, 0)),
                  pl.BlockSpec(memory_space=pl.ANY)],
        out_specs=pl.BlockSpec((tm, d), lambda i, e, f: (i, 0)),
        out_shape=jax.ShapeDtypeStruct((t, d), F32),
        scratch_shapes=[pltpu.SemaphoreType.DMA(())],
        compiler_params=_cparams("parallel", "arbitrary", "arbitrary"),
    )(hn, w_gate, w_up, w_down, gates, gate2, resid)


def moe_layer(x, hn, logits_t, router_b, p, gate2, n_ctx):
    gates = gating(logits_t, router_b).T
    ones = jnp.ones((x.shape[0], 1), F32)
    x = gated_ffn(hn, p["ws_gate"], p["ws_up"], p["ws_down"], ones, gate2, x, n_ctx)
    return gated_ffn(hn, p["w_gate"], p["w_up"], p["w_down"], gates, gate2, x, n_ctx)


def kernel(x, c, ctx, c_ctx, ada_down, ada_up, ada_bias, norm1_g, norm2_g, final_norm_g, mla_w_in, mla_q_norm, mla_kv_norm, mla_w_uq, mla_w_ukv, mla_w_o, ssm_w_in, ssm_conv_w, ssm_conv_b, ssm_dt_bias, ssm_a_log, ssm_d, ssm_norm_g, ssm_w_out, router_w, router_b, moe_w_gate, moe_w_up, moe_w_down, shared_w_gate, shared_w_up, shared_w_down):
    assert x.shape[0] == 1 and ctx.shape[0] == 1
    depth = ada_down.shape[0]
    n_lat, d = x.shape[1], x.shape[2]
    n_ctx = ctx.shape[1]
    xs = jnp.concatenate([ctx[0], x[0]], axis=0).astype(F32)

    cvecs = jnp.zeros((8, d), F32).at[0].set(c_ctx).at[1].set(c[0])
    mods = ada_modulation(cvecs, ada_down, ada_up, ada_bias)[:, :2]
    mods = mods.reshape(depth, 2, N_MOD, d)
    cc, ss = _rope_tables(n_ctx, n_lat)
    router_wt = router_w.T

    for i in range(depth):
        sh1, sc1, g1, sh2, sc2, g2 = (mods[i, :, k] for k in range(N_MOD))
        j = i // 2
        hn = norm_modulate(xs, norm1_g[i], sh1, sc1, n_ctx)
        if i % 2 == 0:
            p = prep_mla_weights(mla_w_in[j], mla_q_norm[j], mla_kv_norm[j], mla_w_uq[j], mla_w_ukv[j], mla_w_o[j])
            xs = mla_layer(xs, hn, p, g1, n_ctx, cc, ss)
        else:
            p = prep_mamba_weights(ssm_w_in[j], ssm_conv_w[j], ssm_conv_b[j], ssm_dt_bias[j], ssm_a_log[j],
                                   ssm_d[j], ssm_norm_g[j], ssm_w_out[j])
            xs = mamba_layer(xs, hn, p, g1, n_ctx)
        hn2, logits_t = norm_modulate(xs, norm2_g[i], sh2, sc2, n_ctx, router_wt=router_wt)
        pm = dict(w_gate=moe_w_gate[i].astype(BF16), w_up=moe_w_up[i].astype(BF16),
                  w_down=moe_w_down[i].astype(BF16), ws_gate=shared_w_gate[i].astype(BF16)[None],
                  ws_up=shared_w_up[i].astype(BF16)[None], ws_down=shared_w_down[i].astype(BF16)[None])
        xs = moe_layer(xs, hn2, logits_t, router_b, pm, g2, n_ctx)
    return final_norm(xs, final_norm_g, n_ctx)[None]
```

```python
import functools
import math

import jax
import jax.numpy as jnp
from jax import lax
from jax.experimental import pallas as pl
from jax.experimental.pallas import tpu as pltpu

F32 = jnp.float32
BF16 = jnp.bfloat16

GRID_W = 64
ROPE_THETA = 10000.0
NORM_EPS = 1e-6
N_MOD = 6
QK_NOPE = 128
QK_ROPE = 64
V_HEAD = 128
SSM_HEAD_DIM = 64
SSM_GROUPS = 8
D_STATE = 128
SSM_CHUNK = 128
N_GROUPS = 4
TOP_K = 2

LANE = 128
VMEM_LIMIT_BYTES = 56 * 1024 * 1024


def _cparams(*sem):
    return pltpu.CompilerParams(dimension_semantics=sem, vmem_limit_bytes=VMEM_LIMIT_BYTES)


def _pick(n, prefs):
    for p in prefs:
        if p <= n and n % p == 0:
            return p
    return n


def _row_select(row0, tm, n_ctx, ref):
    row = row0 + lax.broadcasted_iota(jnp.int32, (tm, 1), 0)
    return jnp.where(row < n_ctx, ref[0:1, :], ref[1:2, :])


def _silu(v):
    return v * (1.0 / (1.0 + jnp.exp(-v)))


def _ada_kernel(c_ref, wd_ref, wu_ref, b_ref, o_ref):
    cv = c_ref[...]
    h = jnp.dot(_silu(cv), wd_ref[...], preferred_element_type=F32, precision=lax.Precision.HIGHEST)
    o_ref[...] = jnp.dot(h, wu_ref[...], preferred_element_type=F32,
                         precision=lax.Precision.HIGHEST) + b_ref[...]


def ada_modulation(cvecs, w_down, w_up, bias):
    depth, d, r = w_down.shape
    n = w_up.shape[-1]
    tn = _pick(n, (2048, 1024, 512, 256, 128))
    return pl.pallas_call(
        _ada_kernel,
        grid=(depth, n // tn),
        in_specs=[pl.BlockSpec((8, d), lambda l, j: (0, 0)),
                  pl.BlockSpec((None, d, r), lambda l, j: (l, 0, 0)),
                  pl.BlockSpec((None, r, tn), lambda l, j: (l, 0, j)),
                  pl.BlockSpec((None, 1, tn), lambda l, j: (l, 0, j))],
        out_specs=pl.BlockSpec((None, 8, tn), lambda l, j: (l, 0, j)),
        out_shape=jax.ShapeDtypeStruct((depth, 8, n), F32),
        compiler_params=_cparams("parallel", "parallel"),
    )(cvecs, w_down, w_up, bias.reshape(depth, 1, n))


def _norm_mod_kernel(x_ref, g_ref, sh_ref, sc_ref, o_ref, *, tm, n_ctx):
    x = x_ref[...]
    y = x * lax.rsqrt(jnp.mean(x * x, axis=-1, keepdims=True) + NORM_EPS) * g_ref[...]
    row0 = pl.program_id(0) * tm
    sh = _row_select(row0, tm, n_ctx, sh_ref)
    sc = _row_select(row0, tm, n_ctx, sc_ref)
    o_ref[...] = (y * (1.0 + sc) + sh).astype(o_ref.dtype)


def _norm_mod_router_kernel(x_ref, g_ref, sh_ref, sc_ref, rw_ref, o_ref, lt_ref, *, tm, n_ctx):
    x = x_ref[...]
    y = x * lax.rsqrt(jnp.mean(x * x, axis=-1, keepdims=True) + NORM_EPS) * g_ref[...]
    row0 = pl.program_id(0) * tm
    sh = _row_select(row0, tm, n_ctx, sh_ref)
    sc = _row_select(row0, tm, n_ctx, sc_ref)
    t = y * (1.0 + sc) + sh
    o_ref[...] = t.astype(o_ref.dtype)
    lt_ref[...] = lax.dot_general(rw_ref[...], t, (((1,), (1,)), ((), ())),
                                  preferred_element_type=F32, precision=lax.Precision.HIGHEST)


def norm_modulate(x, g, shift2, scale2, n_ctx, router_wt=None):
    t, d = x.shape
    tm = _pick(t, (256, 128))
    vec = pl.BlockSpec((1, d), lambda i: (0, 0))
    two = pl.BlockSpec((2, d), lambda i: (0, 0))
    xs = pl.BlockSpec((tm, d), lambda i: (i, 0))
    if router_wt is None:
        return pl.pallas_call(
            functools.partial(_norm_mod_kernel, tm=tm, n_ctx=n_ctx),
            grid=(t // tm,), in_specs=[xs, vec, two, two], out_specs=xs,
            out_shape=jax.ShapeDtypeStruct((t, d), BF16),
            compiler_params=_cparams("parallel"),
        )(x, g.reshape(1, d), shift2, scale2)
    e = router_wt.shape[0]
    return pl.pallas_call(
        functools.partial(_norm_mod_router_kernel, tm=tm, n_ctx=n_ctx),
        grid=(t // tm,),
        in_specs=[xs, vec, two, two, pl.BlockSpec((e, d), lambda i: (0, 0))],
        out_specs=[xs, pl.BlockSpec((e, tm), lambda i: (0, i))],
        out_shape=[jax.ShapeDtypeStruct((t, d), BF16), jax.ShapeDtypeStruct((e, t), F32)],
        compiler_params=_cparams("parallel"),
    )(x, g.reshape(1, d), shift2, scale2, router_wt)


def _final_norm_kernel(x_ref, g_ref, o_ref):
    x = x_ref[...]
    o_ref[...] = x * lax.rsqrt(jnp.mean(x * x, axis=-1, keepdims=True) + NORM_EPS) * g_ref[...]


def final_norm(x, g, row_start):
    t, d = x.shape
    tm = _pick(t, (256, 128))
    assert row_start % tm == 0
    off = row_start // tm
    return pl.pallas_call(
        _final_norm_kernel, grid=((t - row_start) // tm,),
        in_specs=[pl.BlockSpec((tm, d), lambda i: (i + off, 0)), pl.BlockSpec((1, d), lambda i: (0, 0))],
        out_specs=pl.BlockSpec((tm, d), lambda i: (i, 0)),
        out_shape=jax.ShapeDtypeStruct((t - row_start, d), F32),
        compiler_params=_cparams("parallel"),
    )(x, g.reshape(1, d))


def _mm_kernel(a_ref, b_ref, o_ref):
    o_ref[...] = jnp.dot(a_ref[...], b_ref[...], preferred_element_type=F32).astype(o_ref.dtype)


def _mm_tiles(m, k, n):
    tm = _pick(m, (1280, 1024, 640, 512, 256, 128))
    if k > 4096:
        tm = _pick(m, (640, 512, 256, 128))
    tn = _pick(n, (512, 256, 128))
    return tm, tn


def matmul(a, b, out_dtype):
    m, k = a.shape
    n = b.shape[1]
    tm, tn = _mm_tiles(m, k, n)
    return pl.pallas_call(
        _mm_kernel, grid=(m // tm, n // tn),
        in_specs=[pl.BlockSpec((tm, k), lambda i, j: (i, 0)), pl.BlockSpec((k, tn), lambda i, j: (0, j))],
        out_specs=pl.BlockSpec((tm, tn), lambda i, j: (i, j)),
        out_shape=jax.ShapeDtypeStruct((m, n), out_dtype),
        compiler_params=_cparams("parallel", "parallel"),
    )(a, b)


def _mm_res_kernel(a_ref, b_ref, r_ref, g_ref, o_ref, *, tm, n_ctx):
    acc = jnp.dot(a_ref[...], b_ref[...], preferred_element_type=F32)
    gate = _row_select(pl.program_id(0) * tm, tm, n_ctx, g_ref)
    o_ref[...] = r_ref[...] + gate * acc


def matmul_residual(a, b, resid, gate2, n_ctx):
    m, k = a.shape
    n = b.shape[1]
    tm, tn = _mm_tiles(m, k, n)
    return pl.pallas_call(
        functools.partial(_mm_res_kernel, tm=tm, n_ctx=n_ctx), grid=(m // tm, n // tn),
        in_specs=[pl.BlockSpec((tm, k), lambda i, j: (i, 0)), pl.BlockSpec((k, tn), lambda i, j: (0, j)),
                  pl.BlockSpec((tm, tn), lambda i, j: (i, j)), pl.BlockSpec((2, tn), lambda i, j: (0, j))],
        out_specs=pl.BlockSpec((tm, tn), lambda i, j: (i, j)),
        out_shape=jax.ShapeDtypeStruct((m, n), F32),
        compiler_params=_cparams("parallel", "parallel"),
    )(a, b, resid, gate2)


def _rope_tables(n_ctx, n_lat):
    rows = n_lat // GRID_W
    row = jnp.broadcast_to(jnp.arange(rows, dtype=F32)[:, None], (rows, GRID_W)).reshape(-1)
    col = jnp.broadcast_to(jnp.arange(GRID_W, dtype=F32)[None, :], (rows, GRID_W)).reshape(-1)
    n_freq = QK_ROPE // 4
    inv = ROPE_THETA ** (-jnp.arange(n_freq, dtype=F32) / n_freq)
    ang = jnp.concatenate([row[:, None] * inv, col[:, None] * inv], axis=-1)
    ang = jnp.concatenate([jnp.zeros((n_ctx, QK_ROPE // 2), F32), ang], axis=0)
    return jnp.tile(jnp.cos(ang), (1, 4)), jnp.tile(jnp.sin(ang), (1, 4))


def _rot_cols(w):
    h = QK_ROPE // 2
    return jnp.concatenate([-w[..., h:], w[..., :h]], axis=-1)


def _mla_lat_post_kernel(lat_ref, qg_ref, kg_ref, cc_ref, ss_ref, cq_ref, ckv_ref, kr_ref, *, q_rank, kv_rank):
    cq = lat_ref[:, :q_rank]
    cq_ref[...] = (cq * lax.rsqrt(jnp.mean(cq * cq, axis=-1, keepdims=True) + NORM_EPS)
                   * qg_ref[...]).astype(cq_ref.dtype)
    ckv = lat_ref[:, q_rank:q_rank + kv_rank]
    ckv_ref[...] = (ckv * lax.rsqrt(jnp.mean(ckv * ckv, axis=-1, keepdims=True) + NORM_EPS)
                    * kg_ref[...]).astype(ckv_ref.dtype)
    base = q_rank + kv_rank
    cc = cc_ref[...]
    ss = ss_ref[...]
    lo = lat_ref[:, base:base + LANE] * cc + lat_ref[:, base + LANE:base + 2 * LANE] * ss
    hi = lat_ref[:, base + 2 * LANE:base + 3 * LANE] * cc + lat_ref[:, base + 3 * LANE:base + 4 * LANE] * ss
    kr_ref[:, :LANE] = lo.astype(kr_ref.dtype)
    kr_ref[:, LANE:] = hi.astype(kr_ref.dtype)


def mla_lat_post(lat, q_norm, kv_norm, cc, ss):
    t = lat.shape[0]
    q_rank, kv_rank = q_norm.shape[0], kv_norm.shape[0]
    tm = _pick(t, (256, 128))
    row = lambda w: pl.BlockSpec((tm, w), lambda i: (i, 0))
    return pl.pallas_call(
        functools.partial(_mla_lat_post_kernel, q_rank=q_rank, kv_rank=kv_rank), grid=(t // tm,),
        in_specs=[row(lat.shape[1]), pl.BlockSpec((1, q_rank), lambda i: (0, 0)),
                  pl.BlockSpec((1, kv_rank), lambda i: (0, 0)), row(LANE), row(LANE)],
        out_specs=[row(q_rank), row(kv_rank), row(2 * LANE)],
        out_shape=[jax.ShapeDtypeStruct((t, q_rank), BF16), jax.ShapeDtypeStruct((t, kv_rank), BF16),
                   jax.ShapeDtypeStruct((t, 2 * LANE), BF16)],
        compiler_params=_cparams("parallel"),
    )(lat, q_norm.reshape(1, -1), kv_norm.reshape(1, -1), cc, ss)


def _q_proj_kernel(a_ref, b_ref, cc_ref, ss_ref, qn_ref, qr_ref):
    acc = jnp.dot(a_ref[...], b_ref[...], preferred_element_type=F32)
    qn_ref[...] = acc[:, :2 * LANE].astype(qn_ref.dtype)
    qr_ref[...] = (acc[:, 2 * LANE:3 * LANE] * cc_ref[...]
                   + acc[:, 3 * LANE:4 * LANE] * ss_ref[...]).astype(qr_ref.dtype)


def q_proj(cqn, w_pairs, cc, ss):
    t, k = cqn.shape
    n_pairs = w_pairs.shape[1] // (4 * LANE)
    tm = _pick(t, (1280, 1024, 640, 512, 256, 128))
    return pl.pallas_call(
        _q_proj_kernel, grid=(t // tm, n_pairs),
        in_specs=[pl.BlockSpec((tm, k), lambda i, j: (i, 0)), pl.BlockSpec((k, 4 * LANE), lambda i, j: (0, j)),
                  pl.BlockSpec((tm, LANE), lambda i, j: (i, 0)), pl.BlockSpec((tm, LANE), lambda i, j: (i, 0))],
        out_specs=[pl.BlockSpec((tm, 2 * LANE), lambda i, j: (i, j)), pl.BlockSpec((tm, LANE), lambda i, j: (i, j))],
        out_shape=[jax.ShapeDtypeStruct((t, n_pairs * 2 * LANE), BF16),
                   jax.ShapeDtypeStruct((t, n_pairs * LANE), BF16)],
        compiler_params=_cparams("parallel", "parallel"),
    )(cqn, w_pairs, cc, ss)


def _attn_kernel(qn_ref, qr_ref, kn_ref, kr_ref, vt_ref, o_ref, s_scr, acc_scr, *, tq, tk, unroll, n_ctx, n_all):
    q = jnp.concatenate([qn_ref[...], qr_ref[...]], axis=1)
    qt = q.astype(F32).T.astype(BF16)

    def scores(slot, off, size):
        k = jnp.concatenate([kn_ref[pl.ds(off, size), :], kr_ref[pl.ds(off, size), :]], axis=1)
        s = jnp.dot(k, qt, preferred_element_type=F32)
        s_scr[slot, 0:size, :] = s
        return jnp.max(s, axis=0, keepdims=True)

    def update(slot, mx, off, size, carry):
        m, l = carry
        m_new = jnp.maximum(m, mx)
        alpha = jnp.exp2(m - m_new)
        p = jnp.exp2(s_scr[slot, 0:size, :] - m_new)
        l = alpha * l + jnp.sum(p, axis=0, keepdims=True)
        acc_scr[...] = alpha * acc_scr[...] + jnp.dot(vt_ref[:, pl.ds(off, size)], p.astype(BF16),
                                                      preferred_element_type=F32)
        return m_new, l

    def finish(carry):
        o_ref[...] = (acc_scr[...] / carry[1]).T.astype(o_ref.dtype)

    acc_scr[...] = jnp.zeros(acc_scr.shape, F32)
    init = (jnp.full((1, tq), -jnp.inf, F32), jnp.zeros((1, tq), F32))
    is_ctx = pl.program_id(1) * tq < n_ctx

    @pl.when(is_ctx)
    def _():
        finish(update(0, scores(0, 0, n_ctx), 0, n_ctx, init))

    @pl.when(jnp.logical_not(is_ctx))
    def _():
        n_chunks = n_all // tk

        def run(base, count, carry, lookahead):
            mx, st = carry[0], carry[1:]
            for u in range(count):
                off = base + u * tk
                more = u + 1 < count or lookahead
                mx_next = scores((u + 1) % 2, off + tk, tk) if more else None
                st = update(u % 2, mx, off, tk, st)
                mx = mx_next
            return (mx,) + st

        def body(i, carry):
            return run(pl.multiple_of(i * (unroll * tk), tk), unroll, carry, True)

        n_trips = (n_chunks - 1) // unroll
        carry = lax.fori_loop(0, n_trips, body, (scores(0, 0, tk),) + init)
        finish(run(n_trips * unroll * tk, n_chunks - n_trips * unroll, carry, False)[1:])


def attention(qn, qr, kn, kr2, vt, n_ctx):
    t = qn.shape[0]
    heads = qn.shape[1] // QK_NOPE
    tq = _pick(n_ctx, (256, 128))
    tk = _pick(t, (1280, 512, 256, 128))
    assert n_ctx % tq == 0 and t % tq == 0 and n_ctx <= tk
    return pl.pallas_call(
        functools.partial(_attn_kernel, tq=tq, tk=tk, unroll=6, n_ctx=n_ctx, n_all=t),
        grid=(heads, t // tq),
        in_specs=[pl.BlockSpec((tq, LANE), lambda h, i: (i, h)),
                  pl.BlockSpec((tq, LANE), lambda h, i: (i, h // 2)),
                  pl.BlockSpec((t, LANE), lambda h, i: (0, h)),
                  pl.BlockSpec((t, LANE), lambda h, i: (0, h % 2)),
                  pl.BlockSpec((V_HEAD, t), lambda h, i: (h, 0))],
        out_specs=pl.BlockSpec((tq, LANE), lambda h, i: (i, h)),
        out_shape=jax.ShapeDtypeStruct((t, heads * V_HEAD), BF16),
        scratch_shapes=[pltpu.VMEM((2, tk, tq), F32), pltpu.VMEM((V_HEAD, tq), F32)],
        compiler_params=_cparams("parallel", "arbitrary"),
        name="mla_attention",
    )(qn, qr, kn, kr2, vt)


def _mm_nt_kernel(a_ref, b_ref, o_ref):
    o_ref[...] = lax.dot_general(a_ref[...], b_ref[...], (((1,), (1,)), ((), ())),
                                 preferred_element_type=F32).astype(o_ref.dtype)


def matmul_nt(a, b, out_dtype):
    m, k = a.shape
    n = b.shape[0]
    tm = _pick(m, (1024, 512, 256, 128))
    tn = _pick(n, (1280, 1024, 640, 512, 256, 128))
    return pl.pallas_call(
        _mm_nt_kernel, grid=(m // tm, n // tn),
        in_specs=[pl.BlockSpec((tm, k), lambda i, j: (i, 0)), pl.BlockSpec((tn, k), lambda i, j: (j, 0))],
        out_specs=pl.BlockSpec((tm, tn), lambda i, j: (i, j)),
        out_shape=jax.ShapeDtypeStruct((m, n), out_dtype),
        compiler_params=_cparams("parallel", "parallel"),
        name="matmul_nt",
    )(a, b)


def mla_layer(x, hn, p, gate2, n_ctx, cc, ss):
    lat = matmul(hn, p["w_in"], F32)
    cqn, ckvn, kr2 = mla_lat_post(lat, p["q_norm"], p["kv_norm"], cc, ss)
    qn, qr = q_proj(cqn, p["w_uq"], cc, ss)
    kn = matmul(ckvn, p["w_uk"], BF16)
    vt = matmul_nt(p["w_uv_t"], ckvn, BF16)
    o = attention(qn, qr, kn, kr2, vt, n_ctx)
    return matmul_residual(o, p["w_o"], x, gate2, n_ctx)


def prep_mla_weights(w_in, q_norm, kv_norm, w_uq, w_ukv, w_o):
    d = w_in.shape[0]
    heads = d // 128
    q_rank, kv_rank = q_norm.shape[0], kv_norm.shape[0]
    z64 = jnp.zeros((d, QK_ROPE), F32)
    wkr = w_in[:, q_rank + kv_rank:]
    wkr_rot = _rot_cols(wkr)
    w_in_p = jnp.concatenate([w_in[:, :q_rank + kv_rank], wkr, z64, wkr_rot, z64, z64, wkr, z64, wkr_rot], axis=1)
    scale = (QK_NOPE + QK_ROPE) ** -0.5 * math.log2(math.e)
    wq = (w_uq * scale).reshape(q_rank, heads, QK_NOPE + QK_ROPE)
    nope = wq[..., :QK_NOPE].reshape(q_rank, heads // 2, 2 * QK_NOPE)
    rope = wq[..., QK_NOPE:]
    rope_a = rope.reshape(q_rank, heads // 2, 2 * QK_ROPE)
    rope_b = _rot_cols(rope).reshape(q_rank, heads // 2, 2 * QK_ROPE)
    w_uq_p = jnp.concatenate([nope, rope_a, rope_b], axis=-1).reshape(q_rank, heads // 2 * 4 * LANE)
    wkv = w_ukv.reshape(kv_rank, heads, QK_NOPE + V_HEAD)
    w_uk = wkv[..., :QK_NOPE].reshape(kv_rank, heads * QK_NOPE)
    w_uv_t = wkv[..., QK_NOPE:].reshape(kv_rank, heads * V_HEAD).T
    return dict(w_in=w_in_p.astype(BF16), q_norm=q_norm, kv_norm=kv_norm, w_uq=w_uq_p.astype(BF16),
                w_uk=w_uk.astype(BF16), w_uv_t=w_uv_t.astype(BF16), w_o=w_o.astype(BF16))


def _conv_silu_kernel(prev_ref, cur_ref, next_ref, w_ref, b_ref, o_ref, win_ref, *, tm, n_ctx, n_tiles):
    i = pl.program_id(0)
    row0 = i * tm
    has_prev = jnp.logical_and(i > 0, row0 != n_ctx)
    has_next = jnp.logical_and(i < n_tiles - 1, row0 + tm != n_ctx)
    win_ref[0:8, :] = jnp.where(has_prev, prev_ref[...].astype(F32), 0.0)
    win_ref[8:8 + tm, :] = cur_ref[...].astype(F32)
    win_ref[8 + tm:16 + tm, :] = jnp.where(has_next, next_ref[...].astype(F32), 0.0)
    pad = w_ref.shape[0] // 2
    acc = jnp.zeros(o_ref.shape, F32) + b_ref[...]
    for k in range(w_ref.shape[0]):
        acc = acc + w_ref[k:k + 1, :] * win_ref[pl.ds(8 - pad + k, tm), :]
    o_ref[...] = _silu(acc).astype(o_ref.dtype)


def conv_silu(u, w, b, n_ctx):
    t, c = u.shape
    tm = _pick(n_ctx, (256, 128))
    tn = _pick(c, (2048, 1024, 512, 256, 128))
    n_tiles = t // tm
    r8 = tm // 8
    nb8 = t // 8
    return pl.pallas_call(
        functools.partial(_conv_silu_kernel, tm=tm, n_ctx=n_ctx, n_tiles=n_tiles),
        grid=(n_tiles, c // tn),
        in_specs=[pl.BlockSpec((8, tn), lambda i, j: (jnp.maximum(i * r8 - 1, 0), j)),
                  pl.BlockSpec((tm, tn), lambda i, j: (i, j)),
                  pl.BlockSpec((8, tn), lambda i, j: (jnp.minimum((i + 1) * r8, nb8 - 1), j)),
                  pl.BlockSpec((w.shape[0], tn), lambda i, j: (0, j)),
                  pl.BlockSpec((1, tn), lambda i, j: (0, j))],
        out_specs=pl.BlockSpec((tm, tn), lambda i, j: (i, j)),
        out_shape=jax.ShapeDtypeStruct((t, c), BF16),
        scratch_shapes=[pltpu.VMEM((tm + 16, tn), F32)],
        compiler_params=_cparams("parallel", "parallel"),
    )(u, u, u, w, b.reshape(1, c))


def _dt_prep_kernel(raw_ref, bias_ref, a_ref, dtf_ref, cf_ref, dtb_ref, cb_ref, *, nh, q):
    raw = raw_ref[...]
    r = lax.broadcasted_iota(jnp.int32, (q, q), 0)
    c = lax.broadcasted_iota(jnp.int32, (q, q), 1)
    for d, (dt_ref, c_ref) in enumerate(((dtf_ref, cf_ref), (dtb_ref, cb_ref))):
        v = raw[:, d * nh:(d + 1) * nh] + bias_ref[d:d + 1, :]
        dt = jnp.maximum(v, 0.0) + jnp.log1p(jnp.exp(-jnp.abs(v)))
        da = dt * a_ref[d:d + 1, :]
        tri = jnp.where(r >= c, 1.0, 0.0) if d == 0 else jnp.where(r <= c, 1.0, 0.0)
        cum = jnp.dot(tri.astype(F32), da, preferred_element_type=F32, precision=lax.Precision.HIGHEST)
        dt_ref[...] = dt.T
        c_ref[...] = cum.T


def dt_prep(dt_raw, dt_bias, a_log):
    t = dt_raw.shape[0]
    nh = dt_raw.shape[1] // 2
    q = SSM_CHUNK
    a = -jnp.exp(a_log.astype(F32))
    out = jax.ShapeDtypeStruct((nh, t), F32)
    ob = pl.BlockSpec((nh, q), lambda i: (0, i))
    return pl.pallas_call(
        functools.partial(_dt_prep_kernel, nh=nh, q=q), grid=(t // q,),
        in_specs=[pl.BlockSpec((q, 2 * nh), lambda i: (i, 0)), pl.BlockSpec((2, nh), lambda i: (0, 0)),
                  pl.BlockSpec((2, nh), lambda i: (0, 0))],
        out_specs=[ob, ob, ob, ob], out_shape=[out, out, out, out],
        compiler_params=_cparams("parallel"),
    )(dt_raw, dt_bias.astype(F32), a)


def _ssd_kernel(x_ref, b_ref, c_ref, dt_ref, cum_ref, y_ref, s_ref, *, hg, reverse):
    q = SSM_CHUNK
    n_pairs = hg // 2

    @pl.when(pl.program_id(1) == 0)
    def _():
        s_ref[...] = jnp.zeros(s_ref.shape, s_ref.dtype)

    bmat = b_ref[...].astype(F32)
    cmat = c_ref[...].astype(F32)
    cb = lax.dot_general(c_ref[...], b_ref[...], (((1,), (1,)), ((), ())), preferred_element_type=F32)
    bt = bmat.T
    ii = lax.broadcasted_iota(jnp.int32, (q, q), 0)
    jj = lax.broadcasted_iota(jnp.int32, (q, q), 1)
    mask = (jj >= ii) if reverse else (ii >= jj)
    lane_lo = lax.broadcasted_iota(jnp.int32, (q, 2 * SSM_HEAD_DIM), 1) < SSM_HEAD_DIM
    last = 0 if reverse else q - 1

    for pr in range(n_pairs):
        xp = x_ref[:, pr * LANE:(pr + 1) * LANE]
        sp = s_ref[:, pr * LANE:(pr + 1) * LANE]
        rhs = jnp.concatenate([xp, sp.astype(BF16)], axis=0)
        ys, ss, decs = [], [], []
        for hh in range(2):
            h = pr * 2 + hh
            crow = jnp.broadcast_to(cum_ref[h:h + 1, :], (q, q))
            ccol = crow.T
            dtrow = dt_ref[h:h + 1, :]
            seg = jnp.where(mask, ccol - crow, -jnp.inf)
            mm = cb * jnp.exp(seg) * dtrow
            coff = cmat * jnp.exp(ccol)
            lhs = jnp.concatenate([mm.astype(BF16), coff.astype(BF16)], axis=1)
            ys.append(jnp.dot(lhs, rhs, preferred_element_type=F32))
            tot = cum_ref[h:h + 1, last:last + 1]
            wrow = jnp.exp(tot - cum_ref[h:h + 1, :]) * dtrow
            ss.append(jnp.dot((bt * wrow).astype(BF16), xp, preferred_element_type=F32))
            decs.append(jnp.exp(tot))
        y_ref[:, pr * LANE:(pr + 1) * LANE] = jnp.where(lane_lo, ys[0], ys[1]).astype(y_ref.dtype)
        dec = jnp.where(lane_lo[0:1, :], decs[0], decs[1])
        s_ref[:, pr * LANE:(pr + 1) * LANE] = sp * dec + jnp.where(lane_lo, ss[0], ss[1])


def ssd_scan(xbc, dt_t, cum_t, d_inner, n_ctx, reverse):
    t = xbc.shape[0]
    q = SSM_CHUNK
    nh = d_inner // SSM_HEAD_DIM
    hg = nh // SSM_GROUPS
    gw = hg * SSM_HEAD_DIM
    assert gw % LANE == 0 and hg % 8 == 0
    nc = t // q
    ncc = n_ctx // q
    xoff = d_inner // LANE
    if reverse:
        cidx = lambda s: jnp.where(s < ncc, ncc - 1 - s, nc - 1 + ncc - s)
    else:
        cidx = lambda s: s
    return pl.pallas_call(
        functools.partial(_ssd_kernel, hg=hg, reverse=reverse),
        grid=(SSM_GROUPS, nc),
        in_specs=[pl.BlockSpec((q, gw), lambda g, s: (cidx(s), g)),
                  pl.BlockSpec((q, D_STATE), lambda g, s: (cidx(s), xoff + g)),
                  pl.BlockSpec((q, D_STATE), lambda g, s: (cidx(s), xoff + SSM_GROUPS + g)),
                  pl.BlockSpec((hg, q), lambda g, s: (g, cidx(s))),
                  pl.BlockSpec((hg, q), lambda g, s: (g, cidx(s)))],
        out_specs=pl.BlockSpec((q, gw), lambda g, s: (cidx(s), g)),
        out_shape=jax.ShapeDtypeStruct((t, d_inner), F32),
        scratch_shapes=[pltpu.VMEM((D_STATE, gw), F32)],
        compiler_params=_cparams("parallel", "arbitrary"),
    )(xbc, xbc, xbc, dt_t, cum_t)


def _ssm_finish_kernel(yf_ref, yb_ref, x_ref, z_ref, d_ref, g_ref, o_ref, *, n_groups):
    z = z_ref[...].astype(F32)
    y = (yf_ref[...] + yb_ref[...] + d_ref[...] * x_ref[...].astype(F32)) * _silu(z)
    gw = y.shape[1] // n_groups
    for g in range(n_groups):
        yg = y[:, g * gw:(g + 1) * gw]
        yn = yg * lax.rsqrt(jnp.mean(yg * yg, axis=-1, keepdims=True) + NORM_EPS)
        o_ref[:, g * gw:(g + 1) * gw] = (yn * g_ref[:, g * gw:(g + 1) * gw]).astype(o_ref.dtype)


def ssm_finish(yf, yb, xbc, z, d_exp, norm_g):
    t, di = yf.shape
    tm = _pick(t, (128,))
    row = pl.BlockSpec((tm, di), lambda i: (i, 0))
    vec = pl.BlockSpec((1, di), lambda i: (0, 0))
    return pl.pallas_call(
        functools.partial(_ssm_finish_kernel, n_groups=SSM_GROUPS), grid=(t // tm,),
        in_specs=[row, row, row, row, vec, vec], out_specs=row,
        out_shape=jax.ShapeDtypeStruct((t, di), BF16),
        compiler_params=_cparams("parallel"),
    )(yf, yb, xbc, z, d_exp, norm_g.reshape(1, di))


def mamba_layer(x, hn, p, gate2, n_ctx):
    di = p["d_inner"]
    z = matmul(hn, p["w_z"], BF16)
    xbc_raw = matmul(hn, p["w_xbc"], F32)
    dt_raw = matmul(hn, p["w_dt"], F32)
    xbc = conv_silu(xbc_raw, p["conv_w"], p["conv_b"], n_ctx)
    dtf, cf, dtb, cb = dt_prep(dt_raw, p["dt_bias"], p["a_log"])
    yf = ssd_scan(xbc, dtf, cf, di, n_ctx, reverse=False)
    yb = ssd_scan(xbc, dtb, cb, di, n_ctx, reverse=True)
    yn = ssm_finish(yf, yb, xbc, z, p["d_exp"], p["norm_g"])
    return matmul_residual(yn, p["w_out"], x, gate2, n_ctx)


def prep_mamba_weights(w_in, conv_w, conv_b, dt_bias, a_log, d_skip, norm_g, w_out):
    di = norm_g.shape[0]
    conv_dim = conv_w.shape[1]
    nh = d_skip.shape[0]
    w_dt = w_in[:, di + conv_dim:]
    pad = (-w_dt.shape[1]) % LANE
    if pad:
        w_dt = jnp.concatenate([w_dt, jnp.zeros((w_in.shape[0], pad), F32)], axis=1)
    return dict(d_inner=di, w_z=w_in[:, :di].astype(BF16), w_xbc=w_in[:, di:di + conv_dim].astype(BF16),
                w_dt=w_dt.astype(BF16), conv_w=conv_w, conv_b=conv_b, dt_bias=dt_bias, a_log=a_log,
                d_exp=jnp.repeat(d_skip.astype(F32), SSM_HEAD_DIM).reshape(1, di), norm_g=norm_g,
                w_out=w_out.astype(BF16), n_heads=nh)


def _gating_kernel(lt_ref, b_ref, g_ref, *, n_experts):
    epg = n_experts // N_GROUPS
    s = [1.0 / (1.0 + jnp.exp(-lt_ref[e])) for e in range(n_experts)]
    sel = [s[e] + b_ref[e] for e in range(n_experts)]
    best_val, best_idx = None, None
    for g in range(N_GROUPS):
        mem = sel[g * epg:(g + 1) * epg]
        gs = None
        for a in range(epg):
            for b in range(a + 1, epg):
                pair = mem[a] + mem[b]
                gs = pair if gs is None else jnp.maximum(gs, pair)
        if g == 0:
            best_val, best_idx = gs, jnp.zeros(gs.shape, jnp.int32)
        else:
            better = gs > best_val
            best_val = jnp.where(better, gs, best_val)
            best_idx = jnp.where(better, g, best_idx)
    chosen, denom = [], None
    for e in range(n_experts):
        g = e // epg
        rank = jnp.zeros(best_idx.shape, jnp.int32)
        for m in range(g * epg, (g + 1) * epg):
            if m == e:
                continue
            ahead = (sel[m] > sel[e]) if m > e else (sel[m] >= sel[e])
            rank = rank + ahead.astype(jnp.int32)
        pick = jnp.logical_and(best_idx == g, rank < TOP_K)
        w = jnp.where(pick, s[e], 0.0)
        chosen.append(w)
        denom = w if denom is None else denom + w
    inv = 1.0 / denom
    for e in range(n_experts):
        g_ref[e] = chosen[e] * inv


def gating(logits_t, router_b):
    e, t = logits_t.shape
    r = t // LANE
    lt3 = logits_t.reshape(e, r, LANE)
    out = pl.pallas_call(
        functools.partial(_gating_kernel, n_experts=e), grid=(1,),
        in_specs=[pl.BlockSpec((e, r, LANE), lambda i: (0, 0, 0)),
                  pl.BlockSpec(memory_space=pltpu.SMEM)],
        out_specs=pl.BlockSpec((e, r, LANE), lambda i: (0, 0, 0)),
        out_shape=jax.ShapeDtypeStruct((e, r, LANE), F32),
        compiler_params=_cparams("arbitrary"),
    )(lt3, router_b.astype(F32))
    return out.reshape(e, t)


def _ffn_kernel(x_ref, wg_ref, wu_ref, wd_ref, gates_ref, g2_ref, res_hbm, o_ref, sem, *, tm, n_ctx):
    i = pl.program_id(0)
    e = pl.program_id(1)
    f = pl.program_id(2)

    @pl.when(jnp.logical_and(e == 0, f == 0))
    def _():
        cp = pltpu.make_async_copy(res_hbm.at[pl.ds(i * tm, tm), :], o_ref, sem)
        cp.start()
        cp.wait()

    x = x_ref[...]
    hgate = jnp.dot(x, wg_ref[...], preferred_element_type=F32)
    hup = jnp.dot(x, wu_ref[...], preferred_element_type=F32)
    gts = gates_ref[...]
    lane = lax.broadcasted_iota(jnp.int32, gts.shape, 1)
    gcol = jnp.sum(jnp.where(lane == e, gts, 0.0), axis=1, keepdims=True)
    act = (_silu(hgate) * hup * gcol).astype(BF16)
    g2 = _row_select(i * tm, tm, n_ctx, g2_ref)
    d = o_ref.shape[1]
    tc = min(d, 1024)
    for c0 in range(0, d, tc):
        y = jnp.dot(act, wd_ref[:, c0:c0 + tc], preferred_element_type=F32)
        o_ref[:, c0:c0 + tc] += g2[:, c0:c0 + tc] * y


def gated_ffn(hn, w_gate, w_up, w_down, gates, gate2, resid, n_ctx):
    t, d = hn.shape
    n_e, _, df = w_gate.shape
    tm = _pick(t, (640, 512, 256, 128))
    tf = _pick(df, (256, 128))
    return pl.pallas_call(
        functools.partial(_ffn_kernel, tm=tm, n_ctx=n_ctx),
        grid=(t // tm, n_e, df // tf),
        in_specs=[pl.BlockSpec((tm, d), lambda i, e, f: (i, 0)),
                  pl.BlockSpec((None, d, tf), lambda i, e, f: (e, 0, f)),
                  pl.BlockSpec((None, d, tf), lambda i, e, f: (e, 0, f)),
                  pl.BlockSpec((None, tf, d), lambda i, e, f: (e, f, 0)),
                  pl.BlockSpec((tm, n_e), lambda i, e, f: (i, 0)),
                  pl.BlockSpec((2, d), lambda i, e, f: (0, 0)),
                  pl.BlockSpec(memory_space=pl.ANY)],
        out_specs=pl.BlockSpec((tm, d), lambda i, e, f: (i, 0)),
        out_shape=jax.ShapeDtypeStruct((t, d), F32),
        scratch_shapes=[pltpu.SemaphoreType.DMA(())],
        compiler_params=_cparams("parallel", "arbitrary", "arbitrary"),
    )(hn, w_gate, w_up, w_down, gates, gate2, resid)


def moe_layer(x, hn, logits_t, router_b, p, gate2, n_ctx):
    gates = gating(logits_t, router_b).T
    ones = jnp.ones((x.shape[0], 1), F32)
    x = gated_ffn(hn, p["ws_gate"], p["ws_up"], p["ws_down"], ones, gate2, x, n_ctx)
    return gated_ffn(hn, p["w_gate"], p["w_up"], p["w_down"], gates, gate2, x, n_ctx)


def kernel(x, c, ctx, c_ctx, ada_down, ada_up, ada_bias, norm1_g, norm2_g, final_norm_g, mla_w_in, mla_q_norm, mla_kv_norm, mla_w_uq, mla_w_ukv, mla_w_o, ssm_w_in, ssm_conv_w, ssm_conv_b, ssm_dt_bias, ssm_a_log, ssm_d, ssm_norm_g, ssm_w_out, router_w, router_b, moe_w_gate, moe_w_up, moe_w_down, shared_w_gate, shared_w_up, shared_w_down):
    assert x.shape[0] == 1 and ctx.shape[0] == 1
    depth = ada_down.shape[0]
    n_lat, d = x.shape[1], x.shape[2]
    n_ctx = ctx.shape[1]
    xs = jnp.concatenate([ctx[0], x[0]], axis=0).astype(F32)

    cvecs = jnp.zeros((8, d), F32).at[0].set(c_ctx).at[1].set(c[0])
    mods = ada_modulation(cvecs, ada_down, ada_up, ada_bias)[:, :2]
    mods = mods.reshape(depth, 2, N_MOD, d)
    cc, ss = _rope_tables(n_ctx, n_lat)
    router_wt = router_w.T

    for i in range(depth):
        sh1, sc1, g1, sh2, sc2, g2 = (mods[i, :, k] for k in range(N_MOD))
        j = i // 2
        hn = norm_modulate(xs, norm1_g[i], sh1, sc1, n_ctx)
        if i % 2 == 0:
            p = prep_mla_weights(mla_w_in[j], mla_q_norm[j], mla_kv_norm[j], mla_w_uq[j], mla_w_ukv[j], mla_w_o[j])
            xs = mla_layer(xs, hn, p, g1, n_ctx, cc, ss)
        else:
            p = prep_mamba_weights(ssm_w_in[j], ssm_conv_w[j], ssm_conv_b[j], ssm_dt_bias[j], ssm_a_log[j],
                                   ssm_d[j], ssm_norm_g[j], ssm_w_out[j])
            xs = mamba_layer(xs, hn, p, g1, n_ctx)
        hn2, logits_t = norm_modulate(xs, norm2_g[i], sh2, sc2, n_ctx, router_wt=router_wt)
        pm = dict(w_gate=moe_w_gate[i].astype(BF16), w_up=moe_w_up[i].astype(BF16),
                  w_down=moe_w_down[i].astype(BF16), ws_gate=shared_w_gate[i].astype(BF16)[None],
                  ws_up=shared_w_up[i].astype(BF16)[None], ws_down=shared_w_down[i].astype(BF16)[None])
        xs = moe_layer(xs, hn2, logits_t, router_b, pm, g2, n_ctx)
    return final_norm(xs, final_norm_g, n_ctx)[None]
```

```python
import functools
import math

import jax
import jax.numpy as jnp
from jax import lax
from jax.experimental import pallas as pl
from jax.experimental.pallas import tpu as pltpu

F32 = jnp.float32
BF16 = jnp.bfloat16

GRID_W = 64
ROPE_THETA = 10000.0
NORM_EPS = 1e-6
N_MOD = 6
QK_NOPE = 128
QK_ROPE = 64
V_HEAD = 128
SSM_HEAD_DIM = 64
SSM_GROUPS = 8
D_STATE = 128
SSM_CHUNK = 128
N_GROUPS = 4
TOP_K = 2

LANE = 128
VMEM_LIMIT_BYTES = 56 * 1024 * 1024


def _cparams(*sem):
    return pltpu.CompilerParams(dimension_semantics=sem, vmem_limit_bytes=VMEM_LIMIT_BYTES)


def _pick(n, prefs):
    for p in prefs:
        if p <= n and n % p == 0:
            return p
    return n


def _row_select(row0, tm, n_ctx, ref):
    row = row0 + lax.broadcasted_iota(jnp.int32, (tm, 1), 0)
    return jnp.where(row < n_ctx, ref[0:1, :], ref[1:2, :])


def _silu(v):
    return v * (1.0 / (1.0 + jnp.exp(-v)))


def _ada_kernel(c_ref, wd_ref, wu_ref, b_ref, o_ref):
    cv = c_ref[...]
    h = jnp.dot(_silu(cv), wd_ref[...], preferred_element_type=F32, precision=lax.Precision.HIGHEST)
    o_ref[...] = jnp.dot(h, wu_ref[...], preferred_element_type=F32,
                         precision=lax.Precision.HIGHEST) + b_ref[...]


def ada_modulation(cvecs, w_down, w_up, bias):
    depth, d, r = w_down.shape
    n = w_up.shape[-1]
    tn = _pick(n, (2048, 1024, 512, 256, 128))
    return pl.pallas_call(
        _ada_kernel,
        grid=(depth, n // tn),
        in_specs=[pl.BlockSpec((8, d), lambda l, j: (0, 0)),
                  pl.BlockSpec((None, d, r), lambda l, j: (l, 0, 0)),
                  pl.BlockSpec((None, r, tn), lambda l, j: (l, 0, j)),
                  pl.BlockSpec((None, 1, tn), lambda l, j: (l, 0, j))],
        out_specs=pl.BlockSpec((None, 8, tn), lambda l, j: (l, 0, j)),
        out_shape=jax.ShapeDtypeStruct((depth, 8, n), F32),
        compiler_params=_cparams("parallel", "parallel"),
    )(cvecs, w_down, w_up, bias.reshape(depth, 1, n))


def _norm_mod_kernel(x_ref, g_ref, sh_ref, sc_ref, o_ref, *, tm, n_ctx):
    x = x_ref[...]
    y = x * lax.rsqrt(jnp.mean(x * x, axis=-1, keepdims=True) + NORM_EPS) * g_ref[...]
    row0 = pl.program_id(0) * tm
    sh = _row_select(row0, tm, n_ctx, sh_ref)
    sc = _row_select(row0, tm, n_ctx, sc_ref)
    o_ref[...] = (y * (1.0 + sc) + sh).astype(o_ref.dtype)


def _norm_mod_router_kernel(x_ref, g_ref, sh_ref, sc_ref, rw_ref, o_ref, lt_ref, *, tm, n_ctx):
    x = x_ref[...]
    y = x * lax.rsqrt(jnp.mean(x * x, axis=-1, keepdims=True) + NORM_EPS) * g_ref[...]
    row0 = pl.program_id(0) * tm
    sh = _row_select(row0, tm, n_ctx, sh_ref)
    sc = _row_select(row0, tm, n_ctx, sc_ref)
    t = y * (1.0 + sc) + sh
    o_ref[...] = t.astype(o_ref.dtype)
    lt_ref[...] = lax.dot_general(rw_ref[...], t, (((1,), (1,)), ((), ())),
                                  preferred_element_type=F32, precision=lax.Precision.HIGHEST)


def norm_modulate(x, g, shift2, scale2, n_ctx, router_wt=None):
    t, d = x.shape
    tm = _pick(t, (256, 128))
    vec = pl.BlockSpec((1, d), lambda i: (0, 0))
    two = pl.BlockSpec((2, d), lambda i: (0, 0))
    xs = pl.BlockSpec((tm, d), lambda i: (i, 0))
    if router_wt is None:
        return pl.pallas_call(
            functools.partial(_norm_mod_kernel, tm=tm, n_ctx=n_ctx),
            grid=(t // tm,), in_specs=[xs, vec, two, two], out_specs=xs,
            out_shape=jax.ShapeDtypeStruct((t, d), BF16),
            compiler_params=_cparams("parallel"),
        )(x, g.reshape(1, d), shift2, scale2)
    e = router_wt.shape[0]
    return pl.pallas_call(
        functools.partial(_norm_mod_router_kernel, tm=tm, n_ctx=n_ctx),
        grid=(t // tm,),
        in_specs=[xs, vec, two, two, pl.BlockSpec((e, d), lambda i: (0, 0))],
        out_specs=[xs, pl.BlockSpec((e, tm), lambda i: (0, i))],
        out_shape=[jax.ShapeDtypeStruct((t, d), BF16), jax.ShapeDtypeStruct((e, t), F32)],
        compiler_params=_cparams("parallel"),
    )(x, g.reshape(1, d), shift2, scale2, router_wt)


def _final_norm_kernel(x_ref, g_ref, o_ref):
    x = x_ref[...]
    o_ref[...] = x * lax.rsqrt(jnp.mean(x * x, axis=-1, keepdims=True) + NORM_EPS) * g_ref[...]


def final_norm(x, g, row_start):
    t, d = x.shape
    tm = _pick(t, (256, 128))
    assert row_start % tm == 0
    off = row_start // tm
    return pl.pallas_call(
        _final_norm_kernel, grid=((t - row_start) // tm,),
        in_specs=[pl.BlockSpec((tm, d), lambda i: (i + off, 0)), pl.BlockSpec((1, d), lambda i: (0, 0))],
        out_specs=pl.BlockSpec((tm, d), lambda i: (i, 0)),
        out_shape=jax.ShapeDtypeStruct((t - row_start, d), F32),
        compiler_params=_cparams("parallel"),
    )(x, g.reshape(1, d))


def _mm_kernel(a_ref, b_ref, o_ref):
    o_ref[...] = jnp.dot(a_ref[...], b_ref[...], preferred_element_type=F32).astype(o_ref.dtype)


def _mm_tiles(m, k, n):
    tm = _pick(m, (1280, 1024, 640, 512, 256, 128))
    if k > 4096:
        tm = _pick(m, (640, 512, 256, 128))
    tn = _pick(n, (512, 256, 128))
    return tm, tn


def matmul(a, b, out_dtype):
    m, k = a.shape
    n = b.shape[1]
    tm, tn = _mm_tiles(m, k, n)
    return pl.pallas_call(
        _mm_kernel, grid=(m // tm, n // tn),
        in_specs=[pl.BlockSpec((tm, k), lambda i, j: (i, 0)), pl.BlockSpec((k, tn), lambda i, j: (0, j))],
        out_specs=pl.BlockSpec((tm, tn), lambda i, j: (i, j)),
        out_shape=jax.ShapeDtypeStruct((m, n), out_dtype),
        compiler_params=_cparams("parallel", "parallel"),
    )(a, b)


def _mm_res_kernel(a_ref, b_ref, r_ref, g_ref, o_ref, *, tm, n_ctx):
    acc = jnp.dot(a_ref[...], b_ref[...], preferred_element_type=F32)
    gate = _row_select(pl.program_id(0) * tm, tm, n_ctx, g_ref)
    o_ref[...] = r_ref[...] + gate * acc


def matmul_residual(a, b, resid, gate2, n_ctx):
    m, k = a.shape
    n = b.shape[1]
    tm, tn = _mm_tiles(m, k, n)
    return pl.pallas_call(
        functools.partial(_mm_res_kernel, tm=tm, n_ctx=n_ctx), grid=(m // tm, n // tn),
        in_specs=[pl.BlockSpec((tm, k), lambda i, j: (i, 0)), pl.BlockSpec((k, tn), lambda i, j: (0, j)),
                  pl.BlockSpec((tm, tn), lambda i, j: (i, j)), pl.BlockSpec((2, tn), lambda i, j: (0, j))],
        out_specs=pl.BlockSpec((tm, tn), lambda i, j: (i, j)),
        out_shape=jax.ShapeDtypeStruct((m, n), F32),
        compiler_params=_cparams("parallel", "parallel"),
    )(a, b, resid, gate2)


def _rope_tables(n_ctx, n_lat):
    rows = n_lat // GRID_W
    row = jnp.broadcast_to(jnp.arange(rows, dtype=F32)[:, None], (rows, GRID_W)).reshape(-1)
    col = jnp.broadcast_to(jnp.arange(GRID_W, dtype=F32)[None, :], (rows, GRID_W)).reshape(-1)
    n_freq = QK_ROPE // 4
    inv = ROPE_THETA ** (-jnp.arange(n_freq, dtype=F32) / n_freq)
    ang = jnp.concatenate([row[:, None] * inv, col[:, None] * inv], axis=-1)
    ang = jnp.concatenate([jnp.zeros((n_ctx, QK_ROPE // 2), F32), ang], axis=0)
    return jnp.tile(jnp.cos(ang), (1, 4)), jnp.tile(jnp.sin(ang), (1, 4))


def _rot_cols(w):
    h = QK_ROPE // 2
    return jnp.concatenate([-w[..., h:], w[..., :h]], axis=-1)


def _mla_lat_post_kernel(lat_ref, qg_ref, kg_ref, cc_ref, ss_ref, cq_ref, ckv_ref, kr_ref, *, q_rank, kv_rank):
    cq = lat_ref[:, :q_rank]
    cq_ref[...] = (cq * lax.rsqrt(jnp.mean(cq * cq, axis=-1, keepdims=True) + NORM_EPS)
                   * qg_ref[...]).astype(cq_ref.dtype)
    ckv = lat_ref[:, q_rank:q_rank + kv_rank]
    ckv_ref[...] = (ckv * lax.rsqrt(jnp.mean(ckv * ckv, axis=-1, keepdims=True) + NORM_EPS)
                    * kg_ref[...]).astype(ckv_ref.dtype)
    base = q_rank + kv_rank
    cc = cc_ref[...]
    ss = ss_ref[...]
    lo = lat_ref[:, base:base + LANE] * cc + lat_ref[:, base + LANE:base + 2 * LANE] * ss
    hi = lat_ref[:, base + 2 * LANE:base + 3 * LANE] * cc + lat_ref[:, base + 3 * LANE:base + 4 * LANE] * ss
    kr_ref[:, :LANE] = lo.astype(kr_ref.dtype)
    kr_ref[:, LANE:] = hi.astype(kr_ref.dtype)


def mla_lat_post(lat, q_norm, kv_norm, cc, ss):
    t = lat.shape[0]
    q_rank, kv_rank = q_norm.shape[0], kv_norm.shape[0]
    tm = _pick(t, (256, 128))
    row = lambda w: pl.BlockSpec((tm, w), lambda i: (i, 0))
    return pl.pallas_call(
        functools.partial(_mla_lat_post_kernel, q_rank=q_rank, kv_rank=kv_rank), grid=(t // tm,),
        in_specs=[row(lat.shape[1]), pl.BlockSpec((1, q_rank), lambda i: (0, 0)),
                  pl.BlockSpec((1, kv_rank), lambda i: (0, 0)), row(LANE), row(LANE)],
        out_specs=[row(q_rank), row(kv_rank), row(2 * LANE)],
        out_shape=[jax.ShapeDtypeStruct((t, q_rank), BF16), jax.ShapeDtypeStruct((t, kv_rank), BF16),
                   jax.ShapeDtypeStruct((t, 2 * LANE), BF16)],
        compiler_params=_cparams("parallel"),
    )(lat, q_norm.reshape(1, -1), kv_norm.reshape(1, -1), cc, ss)


def _q_proj_kernel(a_ref, b_ref, cc_ref, ss_ref, qn_ref, qr_ref):
    acc = jnp.dot(a_ref[...], b_ref[...], preferred_element_type=F32)
    qn_ref[...] = acc[:, :2 * LANE].astype(qn_ref.dtype)
    qr_ref[...] = (acc[:, 2 * LANE:3 * LANE] * cc_ref[...]
                   + acc[:, 3 * LANE:4 * LANE] * ss_ref[...]).astype(qr_ref.dtype)


def q_proj(cqn, w_pairs, cc, ss):
    t, k = cqn.shape
    n_pairs = w_pairs.shape[1] // (4 * LANE)
    tm = _pick(t, (1280, 1024, 640, 512, 256, 128))
    return pl.pallas_call(
        _q_proj_kernel, grid=(t // tm, n_pairs),
        in_specs=[pl.BlockSpec((tm, k), lambda i, j: (i, 0)), pl.BlockSpec((k, 4 * LANE), lambda i, j: (0, j)),
                  pl.BlockSpec((tm, LANE), lambda i, j: (i, 0)), pl.BlockSpec((tm, LANE), lambda i, j: (i, 0))],
        out_specs=[pl.BlockSpec((tm, 2 * LANE), lambda i, j: (i, j)), pl.BlockSpec((tm, LANE), lambda i, j: (i, j))],
        out_shape=[jax.ShapeDtypeStruct((t, n_pairs * 2 * LANE), BF16),
                   jax.ShapeDtypeStruct((t, n_pairs * LANE), BF16)],
        compiler_params=_cparams("parallel", "parallel"),
    )(cqn, w_pairs, cc, ss)


def _attn_kernel(qn_ref, qr_ref, kn_ref, kr_ref, vt_ref, o_ref, s_scr, acc_scr, *, tq, tk, unroll, n_ctx, n_all):
    q = jnp.concatenate([qn_ref[...], qr_ref[...]], axis=1)
    qt = q.astype(F32).T.astype(BF16)

    def scores(slot, off, size):
        k = jnp.concatenate([kn_ref[pl.ds(off, size), :], kr_ref[pl.ds(off, size), :]], axis=1)
        s = jnp.dot(k, qt, preferred_element_type=F32)
        s_scr[slot, 0:size, :] = s
        return jnp.max(s, axis=0, keepdims=True)

    def update(slot, mx, off, size, carry):
        m, l = carry
        m_new = jnp.maximum(m, mx)
        alpha = jnp.exp2(m - m_new)
        p = jnp.exp2(s_scr[slot, 0:size, :] - m_new)
        l = alpha * l + jnp.sum(p, axis=0, keepdims=True)
        acc_scr[...] = alpha * acc_scr[...] + jnp.dot(vt_ref[:, pl.ds(off, size)], p.astype(BF16),
                                                      preferred_element_type=F32)
        return m_new, l

    def finish(carry):
        o_ref[...] = (acc_scr[...] / carry[1]).T.astype(o_ref.dtype)

    acc_scr[...] = jnp.zeros(acc_scr.shape, F32)
    init = (jnp.full((1, tq), -jnp.inf, F32), jnp.zeros((1, tq), F32))
    is_ctx = pl.program_id(1) * tq < n_ctx

    @pl.when(is_ctx)
    def _():
        finish(update(0, scores(0, 0, n_ctx), 0, n_ctx, init))

    @pl.when(jnp.logical_not(is_ctx))
    def _():
        n_chunks = n_all // tk

        def run(base, count, carry, lookahead):
            mx, st = carry[0], carry[1:]
            for u in range(count):
                off = base + u * tk
                more = u + 1 < count or lookahead
                mx_next = scores((u + 1) % 2, off + tk, tk) if more else None
                st = update(u % 2, mx, off, tk, st)
                mx = mx_next
            return (mx,) + st

        def body(i, carry):
            return run(pl.multiple_of(i * (unroll * tk), tk), unroll, carry, True)

        n_trips = (n_chunks - 1) // unroll
        carry = lax.fori_loop(0, n_trips, body, (scores(0, 0, tk),) + init)
        finish(run(n_trips * unroll * tk, n_chunks - n_trips * unroll, carry, False)[1:])


def attention(qn, qr, kn, kr2, vt, n_ctx):
    t = qn.shape[0]
    heads = qn.shape[1] // QK_NOPE
    tq = _pick(n_ctx, (256, 128))
    tk = _pick(t, (1280, 512, 256, 128))
    assert n_ctx % tq == 0 and t % tq == 0 and n_ctx <= tk
    return pl.pallas_call(
        functools.partial(_attn_kernel, tq=tq, tk=tk, unroll=12, n_ctx=n_ctx, n_all=t),
        grid=(heads, t // tq),
        in_specs=[pl.BlockSpec((tq, LANE), lambda h, i: (i, h)),
                  pl.BlockSpec((tq, LANE), lambda h, i: (i, h // 2)),
                  pl.BlockSpec((t, LANE), lambda h, i: (0, h)),
                  pl.BlockSpec((t, LANE), lambda h, i: (0, h % 2)),
                  pl.BlockSpec((V_HEAD, t), lambda h, i: (h, 0))],
        out_specs=pl.BlockSpec((tq, LANE), lambda h, i: (i, h)),
        out_shape=jax.ShapeDtypeStruct((t, heads * V_HEAD), BF16),
        scratch_shapes=[pltpu.VMEM((2, tk, tq), F32), pltpu.VMEM((V_HEAD, tq), F32)],
        compiler_params=_cparams("parallel", "arbitrary"),
        name="mla_attention",
    )(qn, qr, kn, kr2, vt)


def _mm_nt_kernel(a_ref, b_ref, o_ref):
    o_ref[...] = lax.dot_general(a_ref[...], b_ref[...], (((1,), (1,)), ((), ())),
                                 preferred_element_type=F32).astype(o_ref.dtype)


def matmul_nt(a, b, out_dtype):
    m, k = a.shape
    n = b.shape[0]
    tm = _pick(m, (1024, 512, 256, 128))
    tn = _pick(n, (1280, 1024, 640, 512, 256, 128))
    return pl.pallas_call(
        _mm_nt_kernel, grid=(m // tm, n // tn),
        in_specs=[pl.BlockSpec((tm, k), lambda i, j: (i, 0)), pl.BlockSpec((tn, k), lambda i, j: (j, 0))],
        out_specs=pl.BlockSpec((tm, tn), lambda i, j: (i, j)),
        out_shape=jax.ShapeDtypeStruct((m, n), out_dtype),
        compiler_params=_cparams("parallel", "parallel"),
        name="matmul_nt",
    )(a, b)


def mla_layer(x, hn, p, gate2, n_ctx, cc, ss):
    lat = matmul(hn, p["w_in"], F32)
    cqn, ckvn, kr2 = mla_lat_post(lat, p["q_norm"], p["kv_norm"], cc, ss)
    qn, qr = q_proj(cqn, p["w_uq"], cc, ss)
    kn = matmul(ckvn, p["w_uk"], BF16)
    vt = matmul_nt(p["w_uv_t"], ckvn, BF16)
    o = attention(qn, qr, kn, kr2, vt, n_ctx)
    return matmul_residual(o, p["w_o"], x, gate2, n_ctx)


def prep_mla_weights(w_in, q_norm, kv_norm, w_uq, w_ukv, w_o):
    d = w_in.shape[0]
    heads = d // 128
    q_rank, kv_rank = q_norm.shape[0], kv_norm.shape[0]
    z64 = jnp.zeros((d, QK_ROPE), F32)
    wkr = w_in[:, q_rank + kv_rank:]
    wkr_rot = _rot_cols(wkr)
    w_in_p = jnp.concatenate([w_in[:, :q_rank + kv_rank], wkr, z64, wkr_rot, z64, z64, wkr, z64, wkr_rot], axis=1)
    scale = (QK_NOPE + QK_ROPE) ** -0.5 * math.log2(math.e)
    wq = (w_uq * scale).reshape(q_rank, heads, QK_NOPE + QK_ROPE)
    nope = wq[..., :QK_NOPE].reshape(q_rank, heads // 2, 2 * QK_NOPE)
    rope = wq[..., QK_NOPE:]
    rope_a = rope.reshape(q_rank, heads // 2, 2 * QK_ROPE)
    rope_b = _rot_cols(rope).reshape(q_rank, heads // 2, 2 * QK_ROPE)
    w_uq_p = jnp.concatenate([nope, rope_a, rope_b], axis=-1).reshape(q_rank, heads // 2 * 4 * LANE)
    wkv = w_ukv.reshape(kv_rank, heads, QK_NOPE + V_HEAD)
    w_uk = wkv[..., :QK_NOPE].reshape(kv_rank, heads * QK_NOPE)
    w_uv_t = wkv[..., QK_NOPE:].reshape(kv_rank, heads * V_HEAD).T
    return dict(w_in=w_in_p.astype(BF16), q_norm=q_norm, kv_norm=kv_norm, w_uq=w_uq_p.astype(BF16),
                w_uk=w_uk.astype(BF16), w_uv_t=w_uv_t.astype(BF16), w_o=w_o.astype(BF16))


def _conv_silu_kernel(prev_ref, cur_ref, next_ref, w_ref, b_ref, o_ref, win_ref, *, tm, n_ctx, n_tiles):
    i = pl.program_id(0)
    row0 = i * tm
    has_prev = jnp.logical_and(i > 0, row0 != n_ctx)
    has_next = jnp.logical_and(i < n_tiles - 1, row0 + tm != n_ctx)
    h = CONV_HALO
    win_ref[0:h, :] = jnp.where(has_prev, prev_ref[...].astype(F32), 0.0)
    win_ref[h:h + tm, :] = cur_ref[...].astype(F32)
    win_ref[h + tm:2 * h + tm, :] = jnp.where(has_next, next_ref[...].astype(F32), 0.0)
    pad = w_ref.shape[0] // 2
    acc = jnp.zeros(o_ref.shape, F32) + b_ref[...]
    for k in range(w_ref.shape[0]):
        acc = acc + w_ref[k:k + 1, :] * win_ref[pl.ds(h - pad + k, tm), :]
    o_ref[...] = _silu(acc).astype(o_ref.dtype)


CONV_HALO = 16


def conv_silu(u, w, b, n_ctx):
    t, c = u.shape
    tm = _pick(n_ctx, (256, 128))
    tn = _pick(c, (2048, 1024, 512, 256, 128))
    n_tiles = t // tm
    rh = tm // CONV_HALO
    nbh = t // CONV_HALO
    assert w.shape[0] // 2 <= CONV_HALO
    return pl.pallas_call(
        functools.partial(_conv_silu_kernel, tm=tm, n_ctx=n_ctx, n_tiles=n_tiles),
        grid=(n_tiles, c // tn),
        in_specs=[pl.BlockSpec((CONV_HALO, tn), lambda i, j: (jnp.maximum(i * rh - 1, 0), j)),
                  pl.BlockSpec((tm, tn), lambda i, j: (i, j)),
                  pl.BlockSpec((CONV_HALO, tn), lambda i, j: (jnp.minimum((i + 1) * rh, nbh - 1), j)),
                  pl.BlockSpec((w.shape[0], tn), lambda i, j: (0, j)),
                  pl.BlockSpec((1, tn), lambda i, j: (0, j))],
        out_specs=pl.BlockSpec((tm, tn), lambda i, j: (i, j)),
        out_shape=jax.ShapeDtypeStruct((t, c), BF16),
        scratch_shapes=[pltpu.VMEM((tm + 2 * CONV_HALO, tn), F32)],
        compiler_params=_cparams("parallel", "parallel"),
        name="conv_silu",
    )(u, u, u, w, b.reshape(1, c))


def _dt_prep_kernel(raw_ref, bias_ref, a_ref, dtf_ref, cf_ref, dtb_ref, cb_ref, *, nh, q):
    raw = raw_ref[...]
    r = lax.broadcasted_iota(jnp.int32, (q, q), 0)
    c = lax.broadcasted_iota(jnp.int32, (q, q), 1)
    for d, (dt_ref, c_ref) in enumerate(((dtf_ref, cf_ref), (dtb_ref, cb_ref))):
        v = raw[:, d * nh:(d + 1) * nh] + bias_ref[d:d + 1, :]
        dt = jnp.maximum(v, 0.0) + jnp.log1p(jnp.exp(-jnp.abs(v)))
        da = dt * a_ref[d:d + 1, :]
        tri = jnp.where(r >= c, 1.0, 0.0) if d == 0 else jnp.where(r <= c, 1.0, 0.0)
        cum = jnp.dot(tri.astype(F32), da, preferred_element_type=F32, precision=lax.Precision.HIGHEST)
        dt_ref[...] = dt.T
        c_ref[...] = cum.T


def dt_prep(dt_raw, dt_bias, a_log):
    t = dt_raw.shape[0]
    nh = dt_raw.shape[1] // 2
    q = SSM_CHUNK
    a = -jnp.exp(a_log.astype(F32))
    out = jax.ShapeDtypeStruct((nh, t), F32)
    ob = pl.BlockSpec((nh, q), lambda i: (0, i))
    return pl.pallas_call(
        functools.partial(_dt_prep_kernel, nh=nh, q=q), grid=(t // q,),
        in_specs=[pl.BlockSpec((q, 2 * nh), lambda i: (i, 0)), pl.BlockSpec((2, nh), lambda i: (0, 0)),
                  pl.BlockSpec((2, nh), lambda i: (0, 0))],
        out_specs=[ob, ob, ob, ob], out_shape=[out, out, out, out],
        compiler_params=_cparams("parallel"),
    )(dt_raw, dt_bias.astype(F32), a)


def _ssd_kernel(x_ref, b_ref, c_ref, dt_ref, cum_ref, y_ref, s_ref, *, hg, reverse):
    q = SSM_CHUNK
    n_pairs = hg // 2

    @pl.when(pl.program_id(1) == 0)
    def _():
        s_ref[...] = jnp.zeros(s_ref.shape, s_ref.dtype)

    bmat = b_ref[...].astype(F32)
    cmat = c_ref[...].astype(F32)
    cb = lax.dot_general(c_ref[...], b_ref[...], (((1,), (1,)), ((), ())), preferred_element_type=F32)
    bt = bmat.T
    ii = lax.broadcasted_iota(jnp.int32, (q, q), 0)
    jj = lax.broadcasted_iota(jnp.int32, (q, q), 1)
    mask = (jj >= ii) if reverse else (ii >= jj)
    lane_lo = lax.broadcasted_iota(jnp.int32, (q, 2 * SSM_HEAD_DIM), 1) < SSM_HEAD_DIM
    last = 0 if reverse else q - 1

    for pr in range(n_pairs):
        xp = x_ref[:, pr * LANE:(pr + 1) * LANE]
        sp = s_ref[:, pr * LANE:(pr + 1) * LANE]
        rhs = jnp.concatenate([xp, sp.astype(BF16)], axis=0)
        ys, ss, decs = [], [], []
        for hh in range(2):
            h = pr * 2 + hh
            crow = jnp.broadcast_to(cum_ref[h:h + 1, :], (q, q))
            ccol = crow.T
            dtrow = dt_ref[h:h + 1, :]
            seg = jnp.where(mask, ccol - crow, -jnp.inf)
            mm = cb * jnp.exp(seg) * dtrow
            coff = cmat * jnp.exp(ccol)
            lhs = jnp.concatenate([mm.astype(BF16), coff.astype(BF16)], axis=1)
            ys.append(jnp.dot(lhs, rhs, preferred_element_type=F32))
            tot = cum_ref[h:h + 1, last:last + 1]
            wrow = jnp.exp(tot - cum_ref[h:h + 1, :]) * dtrow
            ss.append(jnp.dot((bt * wrow).astype(BF16), xp, preferred_element_type=F32))
            decs.append(jnp.exp(tot))
        y_ref[:, pr * LANE:(pr + 1) * LANE] = jnp.where(lane_lo, ys[0], ys[1]).astype(y_ref.dtype)
        dec = jnp.where(lane_lo[0:1, :], decs[0], decs[1])
        s_ref[:, pr * LANE:(pr + 1) * LANE] = sp * dec + jnp.where(lane_lo, ss[0], ss[1])


def ssd_scan(xbc, dt_t, cum_t, d_inner, n_ctx, reverse):
    t = xbc.shape[0]
    q = SSM_CHUNK
    nh = d_inner // SSM_HEAD_DIM
    hg = nh // SSM_GROUPS
    gw = hg * SSM_HEAD_DIM
    assert gw % LANE == 0 and hg % 8 == 0
    nc = t // q
    ncc = n_ctx // q
    xoff = d_inner // LANE
    if reverse:
        cidx = lambda s: jnp.where(s < ncc, ncc - 1 - s, nc - 1 + ncc - s)
    else:
        cidx = lambda s: s
    return pl.pallas_call(
        functools.partial(_ssd_kernel, hg=hg, reverse=reverse),
        grid=(SSM_GROUPS, nc),
        in_specs=[pl.BlockSpec((q, gw), lambda g, s: (cidx(s), g)),
                  pl.BlockSpec((q, D_STATE), lambda g, s: (cidx(s), xoff + g)),
                  pl.BlockSpec((q, D_STATE), lambda g, s: (cidx(s), xoff + SSM_GROUPS + g)),
                  pl.BlockSpec((hg, q), lambda g, s: (g, cidx(s))),
                  pl.BlockSpec((hg, q), lambda g, s: (g, cidx(s)))],
        out_specs=pl.BlockSpec((q, gw), lambda g, s: (cidx(s), g)),
        out_shape=jax.ShapeDtypeStruct((t, d_inner), BF16),
        scratch_shapes=[pltpu.VMEM((D_STATE, gw), F32)],
        compiler_params=_cparams("parallel", "arbitrary"),
    )(xbc, xbc, xbc, dt_t, cum_t)


def _ssm_finish_kernel(yf_ref, yb_ref, x_ref, z_ref, d_ref, g_ref, o_ref, *, n_groups):
    z = z_ref[...].astype(F32)
    y = (yf_ref[...].astype(F32) + yb_ref[...].astype(F32) + d_ref[...] * x_ref[...].astype(F32)) * _silu(z)
    gw = y.shape[1] // n_groups
    for g in range(n_groups):
        yg = y[:, g * gw:(g + 1) * gw]
        yn = yg * lax.rsqrt(jnp.mean(yg * yg, axis=-1, keepdims=True) + NORM_EPS)
        o_ref[:, g * gw:(g + 1) * gw] = (yn * g_ref[:, g * gw:(g + 1) * gw]).astype(o_ref.dtype)


def ssm_finish(yf, yb, xbc, z, d_exp, norm_g):
    t, di = yf.shape
    tm = _pick(t, (256, 128))
    row = pl.BlockSpec((tm, di), lambda i: (i, 0))
    vec = pl.BlockSpec((1, di), lambda i: (0, 0))
    return pl.pallas_call(
        functools.partial(_ssm_finish_kernel, n_groups=SSM_GROUPS), grid=(t // tm,),
        in_specs=[row, row, row, row, vec, vec], out_specs=row,
        out_shape=jax.ShapeDtypeStruct((t, di), BF16),
        compiler_params=_cparams("parallel"),
    )(yf, yb, xbc, z, d_exp, norm_g.reshape(1, di))


def mamba_layer(x, hn, p, gate2, n_ctx):
    di = p["d_inner"]
    z = matmul(hn, p["w_z"], BF16)
    xbc_raw = matmul(hn, p["w_xbc"], BF16)
    dt_raw = matmul(hn, p["w_dt"], F32)
    xbc = conv_silu(xbc_raw, p["conv_w"], p["conv_b"], n_ctx)
    dtf, cf, dtb, cb = dt_prep(dt_raw, p["dt_bias"], p["a_log"])
    yf = ssd_scan(xbc, dtf, cf, di, n_ctx, reverse=False)
    yb = ssd_scan(xbc, dtb, cb, di, n_ctx, reverse=True)
    yn = ssm_finish(yf, yb, xbc, z, p["d_exp"], p["norm_g"])
    return matmul_residual(yn, p["w_out"], x, gate2, n_ctx)


def prep_mamba_weights(w_in, conv_w, conv_b, dt_bias, a_log, d_skip, norm_g, w_out):
    di = norm_g.shape[0]
    conv_dim = conv_w.shape[1]
    nh = d_skip.shape[0]
    w_dt = w_in[:, di + conv_dim:]
    pad = (-w_dt.shape[1]) % LANE
    if pad:
        w_dt = jnp.concatenate([w_dt, jnp.zeros((w_in.shape[0], pad), F32)], axis=1)
    return dict(d_inner=di, w_z=w_in[:, :di].astype(BF16), w_xbc=w_in[:, di:di + conv_dim].astype(BF16),
                w_dt=w_dt.astype(BF16), conv_w=conv_w, conv_b=conv_b, dt_bias=dt_bias, a_log=a_log,
                d_exp=jnp.repeat(d_skip.astype(F32), SSM_HEAD_DIM).reshape(1, di), norm_g=norm_g,
                w_out=w_out.astype(BF16), n_heads=nh)


def _gating_kernel(lt_ref, b_ref, g_ref, grp_ref, *, n_experts):
    epg = n_experts // N_GROUPS
    s = [1.0 / (1.0 + jnp.exp(-lt_ref[e])) for e in range(n_experts)]
    sel = [s[e] + b_ref[e] for e in range(n_experts)]
    best_val, best_idx = None, None
    for g in range(N_GROUPS):
        mem = sel[g * epg:(g + 1) * epg]
        gs = None
        for a in range(epg):
            for b in range(a + 1, epg):
                pair = mem[a] + mem[b]
                gs = pair if gs is None else jnp.maximum(gs, pair)
        if g == 0:
            best_val, best_idx = gs, jnp.zeros(gs.shape, jnp.int32)
        else:
            better = gs > best_val
            best_val = jnp.where(better, gs, best_val)
            best_idx = jnp.where(better, g, best_idx)
    chosen, denom = [], None
    for e in range(n_experts):
        g = e // epg
        rank = jnp.zeros(best_idx.shape, jnp.int32)
        for m in range(g * epg, (g + 1) * epg):
            if m == e:
                continue
            ahead = (sel[m] > sel[e]) if m > e else (sel[m] >= sel[e])
            rank = rank + ahead.astype(jnp.int32)
        pick = jnp.logical_and(best_idx == g, rank < TOP_K)
        w = jnp.where(pick, s[e], 0.0)
        chosen.append(w)
        denom = w if denom is None else denom + w
    inv = 1.0 / denom
    for e in range(n_experts):
        g_ref[e] = chosen[e] * inv
    grp_ref[...] = best_idx


def gating(logits_t, router_b):
    e, t = logits_t.shape
    r = t // LANE
    lt3 = logits_t.reshape(e, r, LANE)
    gates, grp = pl.pallas_call(
        functools.partial(_gating_kernel, n_experts=e), grid=(1,),
        in_specs=[pl.BlockSpec((e, r, LANE), lambda i: (0, 0, 0)),
                  pl.BlockSpec(memory_space=pltpu.SMEM)],
        out_specs=[pl.BlockSpec((e, r, LANE), lambda i: (0, 0, 0)), pl.BlockSpec((r, LANE), lambda i: (0, 0))],
        out_shape=[jax.ShapeDtypeStruct((e, r, LANE), F32), jax.ShapeDtypeStruct((r, LANE), jnp.int32)],
        compiler_params=_cparams("arbitrary"),
        name="moe_gating",
    )(lt3, router_b.astype(F32))
    return gates.reshape(e, t), grp.reshape(t)


def _row_gather_kernel(idx_ref, src_hbm, o_hbm, sem, *, blk):
    base = pl.program_id(0) * blk

    def row_copy(q, src_row):
        return pltpu.make_async_copy(src_hbm.at[src_row], o_hbm.at[base + q], sem)

    def issue(q, carry):
        row_copy(q, idx_ref[0, 0, q]).start()
        return carry

    def drain(q, carry):
        row_copy(q, 0).wait()
        return carry

    lax.fori_loop(0, blk, issue, 0)
    lax.fori_loop(0, blk, drain, 0)


def row_gather(src, idx, blk):
    n, d = src.shape
    m = idx.shape[0]
    per = 4 // src.dtype.itemsize
    words = d // per
    assert words % LANE == 0 and m % blk == 0
    row_words = (words // LANE, LANE)
    packed = src.reshape(n, words, per) if per > 1 else src
    src3 = lax.bitcast_convert_type(packed, jnp.uint32).reshape((n,) + row_words)
    out = _row_gather_call(src3, idx, blk)
    out = out.reshape(m, words)
    if per > 1:
        return lax.bitcast_convert_type(out, src.dtype).reshape(m, d)
    return lax.bitcast_convert_type(out, src.dtype)


def _row_gather_call(src3, idx, blk):
    m = idx.shape[0]
    row_words = src3.shape[1:]
    return pl.pallas_call(
        functools.partial(_row_gather_kernel, blk=blk), grid=(m // blk,),
        in_specs=[pl.BlockSpec((1, 1, blk), lambda b: (b, 0, 0), memory_space=pltpu.SMEM),
                  pl.BlockSpec(memory_space=pl.ANY)],
        out_specs=pl.BlockSpec(memory_space=pl.ANY),
        out_shape=jax.ShapeDtypeStruct((m,) + row_words, jnp.uint32),
        scratch_shapes=[pltpu.SemaphoreType.DMA(())],
        compiler_params=_cparams("arbitrary"),
        name="row_gather",
    )(idx.reshape(m // blk, 1, blk), src3)


def _group_ffn_kernel(tg_ref, tv_ref, x_ref, wg_ref, wu_ref, wd_ref, gs_ref, o_ref, acc_ref):
    r = pl.program_id(0)
    j = pl.program_id(1)
    f = pl.program_id(2)
    first = jnp.logical_and(j == 0, f == 0)
    last = jnp.logical_and(j == pl.num_programs(1) - 1, f == pl.num_programs(2) - 1)

    @pl.when(first)
    def _():
        acc_ref[...] = jnp.zeros(acc_ref.shape, F32)

    @pl.when(tv_ref[r] > 0)
    def _():
        x = x_ref[...]
        hgate = jnp.dot(x, wg_ref[...], preferred_element_type=F32)
        hup = jnp.dot(x, wu_ref[...], preferred_element_type=F32)
        gts = gs_ref[...]
        lane = lax.broadcasted_iota(jnp.int32, gts.shape, 1)
        gcol = jnp.sum(jnp.where(lane == j, gts, 0.0), axis=1, keepdims=True)
        act = (_silu(hgate) * hup * gcol).astype(BF16)
        d = acc_ref.shape[1]
        tc = min(d, 1024)
        for c0 in range(0, d, tc):
            acc_ref[:, c0:c0 + tc] += jnp.dot(act, wd_ref[:, c0:c0 + tc], preferred_element_type=F32)

    @pl.when(last)
    def _():
        o_ref[...] = acc_ref[...].astype(o_ref.dtype)


def group_ffn(xs, w_gate, w_up, w_down, gates_sorted, tile_group, tile_valid, tm):
    n_rows, d = xs.shape
    n_e, _, df = w_gate.shape
    epg = n_e // N_GROUPS
    tf = _pick(df, (256, 128))
    grid_spec = pltpu.PrefetchScalarGridSpec(
        num_scalar_prefetch=2, grid=(n_rows // tm, epg, df // tf),
        in_specs=[pl.BlockSpec((tm, d), lambda r, j, f, tg, tv: (r, 0)),
                  pl.BlockSpec((None, d, tf), lambda r, j, f, tg, tv: (tg[r] * epg + j, 0, f)),
                  pl.BlockSpec((None, d, tf), lambda r, j, f, tg, tv: (tg[r] * epg + j, 0, f)),
                  pl.BlockSpec((None, tf, d), lambda r, j, f, tg, tv: (tg[r] * epg + j, f, 0)),
                  pl.BlockSpec((tm, epg), lambda r, j, f, tg, tv: (r, 0))],
        out_specs=pl.BlockSpec((tm, d), lambda r, j, f, tg, tv: (r, 0)),
        scratch_shapes=[pltpu.VMEM((tm, d), F32)])
    return pl.pallas_call(
        _group_ffn_kernel, grid_spec=grid_spec,
        out_shape=jax.ShapeDtypeStruct((n_rows, d), BF16),
        compiler_params=_cparams("parallel", "arbitrary", "arbitrary"),
        name="group_ffn",
    )(tile_group, tile_valid, xs, w_gate, w_up, w_down, gates_sorted)


def _shared_ffn_kernel(x_ref, wg_ref, wu_ref, wd_ref, y_ref, res_ref, g2_ref, o_ref, *, tm, n_ctx):
    g2 = _row_select(pl.program_id(0) * tm, tm, n_ctx, g2_ref)

    @pl.when(pl.program_id(1) == 0)
    def _():
        o_ref[...] = res_ref[...] + g2 * y_ref[...].astype(F32)

    x = x_ref[...]
    act = (_silu(jnp.dot(x, wg_ref[...], preferred_element_type=F32))
           * jnp.dot(x, wu_ref[...], preferred_element_type=F32)).astype(BF16)
    o_ref[...] += g2 * jnp.dot(act, wd_ref[...], preferred_element_type=F32)


def shared_ffn_combine(hn, w_gate, w_up, w_down, y_routed, resid, gate2, n_ctx):
    t, d = hn.shape
    df = w_gate.shape[1]
    tm = _pick(t, (320, 256, 128))
    tf = _pick(df, (256, 128))
    row = lambda: pl.BlockSpec((tm, d), lambda i, f: (i, 0))
    return pl.pallas_call(
        functools.partial(_shared_ffn_kernel, tm=tm, n_ctx=n_ctx), grid=(t // tm, df // tf),
        in_specs=[row(), pl.BlockSpec((d, tf), lambda i, f: (0, f)), pl.BlockSpec((d, tf), lambda i, f: (0, f)),
                  pl.BlockSpec((tf, d), lambda i, f: (f, 0)), row(), row(),
                  pl.BlockSpec((2, d), lambda i, f: (0, 0))],
        out_specs=row(),
        out_shape=jax.ShapeDtypeStruct((t, d), F32),
        compiler_params=_cparams("parallel", "arbitrary"),
        name="shared_ffn_combine",
    )(hn, w_gate, w_up, w_down, y_routed, resid, gate2)


def _group_sort_plan(grp, tm):
    t = grp.shape[0]
    n_tiles = (t + N_GROUPS * (tm - 1)) // tm
    n_rows = n_tiles * tm
    onehot = (grp[:, None] == jnp.arange(N_GROUPS, dtype=jnp.int32)[None, :]).astype(jnp.int32)
    rank = jnp.sum((jnp.cumsum(onehot, axis=0) - onehot) * onehot, axis=1)
    counts = jnp.sum(onehot, axis=0)
    padded = (counts + tm - 1) // tm * tm
    ends = jnp.cumsum(padded)
    dest = (ends - padded)[grp] + rank
    src = jnp.zeros((n_rows,), jnp.int32).at[dest].set(jnp.arange(t, dtype=jnp.int32))
    row_valid = jnp.zeros((n_rows,), jnp.int32).at[dest].set(1)
    tile_start = jnp.arange(n_tiles, dtype=jnp.int32) * tm
    tile_group = jnp.minimum(jnp.sum((tile_start[:, None] >= ends[None, :]).astype(jnp.int32), axis=1),
                             N_GROUPS - 1)
    tile_valid = (tile_start < ends[-1]).astype(jnp.int32)
    return dest, src, row_valid, tile_group, tile_valid


def moe_layer(x, hn, logits_t, router_b, p, gate2, n_ctx):
    t = hn.shape[0]
    n_e = logits_t.shape[0]
    epg = n_e // N_GROUPS
    tm = _pick(t, (640, 512, 256, 128))
    gates_t, grp = gating(logits_t, router_b)
    dest, src, row_valid, tile_group, tile_valid = _group_sort_plan(grp, tm)
    gate_cols = grp[:, None] * epg + jnp.arange(epg, dtype=jnp.int32)[None, :]
    gates_tok = jnp.take_along_axis(gates_t.T, gate_cols, axis=1)
    gates_sorted = jnp.where(row_valid[:, None] > 0, gates_tok[src], 0.0)
    xs = row_gather(hn, src, tm)
    ys = group_ffn(xs, p["w_gate"], p["w_up"], p["w_down"], gates_sorted, tile_group, tile_valid, tm)
    y_tok = row_gather(ys, dest, tm)
    return shared_ffn_combine(hn, p["ws_gate"], p["ws_up"], p["ws_down"], y_tok, x, gate2, n_ctx)


def kernel(x, c, ctx, c_ctx, ada_down, ada_up, ada_bias, norm1_g, norm2_g, final_norm_g, mla_w_in, mla_q_norm, mla_kv_norm, mla_w_uq, mla_w_ukv, mla_w_o, ssm_w_in, ssm_conv_w, ssm_conv_b, ssm_dt_bias, ssm_a_log, ssm_d, ssm_norm_g, ssm_w_out, router_w, router_b, moe_w_gate, moe_w_up, moe_w_down, shared_w_gate, shared_w_up, shared_w_down):
    assert x.shape[0] == 1 and ctx.shape[0] == 1
    depth = ada_down.shape[0]
    n_lat, d = x.shape[1], x.shape[2]
    n_ctx = ctx.shape[1]
    xs = jnp.concatenate([ctx[0], x[0]], axis=0).astype(F32)

    cvecs = jnp.zeros((8, d), F32).at[0].set(c_ctx).at[1].set(c[0])
    mods = ada_modulation(cvecs, ada_down, ada_up, ada_bias)[:, :2]
    mods = mods.reshape(depth, 2, N_MOD, d)
    cc, ss = _rope_tables(n_ctx, n_lat)
    router_wt = router_w.T

    for i in range(depth):
        sh1, sc1, g1, sh2, sc2, g2 = (mods[i, :, k] for k in range(N_MOD))
        j = i // 2
        hn = norm_modulate(xs, norm1_g[i], sh1, sc1, n_ctx)
        if i % 2 == 0:
            p = prep_mla_weights(mla_w_in[j], mla_q_norm[j], mla_kv_norm[j], mla_w_uq[j], mla_w_ukv[j], mla_w_o[j])
            xs = mla_layer(xs, hn, p, g1, n_ctx, cc, ss)
        else:
            p = prep_mamba_weights(ssm_w_in[j], ssm_conv_w[j], ssm_conv_b[j], ssm_dt_bias[j], ssm_a_log[j],
                                   ssm_d[j], ssm_norm_g[j], ssm_w_out[j])
            xs = mamba_layer(xs, hn, p, g1, n_ctx)
        hn2, logits_t = norm_modulate(xs, norm2_g[i], sh2, sc2, n_ctx, router_wt=router_wt)
        pm = dict(w_gate=moe_w_gate[i].astype(BF16), w_up=moe_w_up[i].astype(BF16),
                  w_down=moe_w_down[i].astype(BF16), ws_gate=shared_w_gate[i].astype(BF16),
                  ws_up=shared_w_up[i].astype(BF16), ws_down=shared_w_down[i].astype(BF16))
        xs = moe_layer(xs, hn2, logits_t, router_b, pm, g2, n_ctx)
    return final_norm(xs, final_norm_g, n_ctx)[None]
```

```python
import functools
import math

import jax
import jax.numpy as jnp
from jax import lax
from jax.experimental import pallas as pl
from jax.experimental.pallas import tpu as pltpu

F32 = jnp.float32
BF16 = jnp.bfloat16

GRID_W = 64
ROPE_THETA = 10000.0
NORM_EPS = 1e-6
N_MOD = 6
QK_NOPE = 128
QK_ROPE = 64
V_HEAD = 128
SSM_HEAD_DIM = 64
SSM_GROUPS = 8
D_STATE = 128
SSM_CHUNK = 128
N_GROUPS = 4
TOP_K = 2

LANE = 128
VMEM_LIMIT_BYTES = 56 * 1024 * 1024


def _cparams(*sem):
    return pltpu.CompilerParams(dimension_semantics=sem, vmem_limit_bytes=VMEM_LIMIT_BYTES)


def _pick(n, prefs):
    for p in prefs:
        if p <= n and n % p == 0:
            return p
    return n


def _row_select(row0, tm, n_ctx, ref):
    row = row0 + lax.broadcasted_iota(jnp.int32, (tm, 1), 0)
    return jnp.where(row < n_ctx, ref[0:1, :], ref[1:2, :])


def _silu(v):
    return v * (1.0 / (1.0 + jnp.exp(-v)))


def _ada_kernel(c_ref, wd_ref, wu_ref, b_ref, o_ref):
    cv = c_ref[...]
    h = jnp.dot(_silu(cv), wd_ref[...], preferred_element_type=F32, precision=lax.Precision.HIGHEST)
    o_ref[...] = jnp.dot(h, wu_ref[...], preferred_element_type=F32,
                         precision=lax.Precision.HIGHEST) + b_ref[...]


def ada_modulation(cvecs, w_down, w_up, bias):
    depth, d, r = w_down.shape
    n = w_up.shape[-1]
    tn = _pick(n, (2048, 1024, 512, 256, 128))
    return pl.pallas_call(
        _ada_kernel,
        grid=(depth, n // tn),
        in_specs=[pl.BlockSpec((8, d), lambda l, j: (0, 0)),
                  pl.BlockSpec((None, d, r), lambda l, j: (l, 0, 0)),
                  pl.BlockSpec((None, r, tn), lambda l, j: (l, 0, j)),
                  pl.BlockSpec((None, 1, tn), lambda l, j: (l, 0, j))],
        out_specs=pl.BlockSpec((None, 8, tn), lambda l, j: (l, 0, j)),
        out_shape=jax.ShapeDtypeStruct((depth, 8, n), F32),
        compiler_params=_cparams("parallel", "parallel"),
    )(cvecs, w_down, w_up, bias.reshape(depth, 1, n))


def _norm_mod_kernel(x_ref, g_ref, sh_ref, sc_ref, o_ref, *, tm, n_ctx):
    x = x_ref[...]
    y = x * lax.rsqrt(jnp.mean(x * x, axis=-1, keepdims=True) + NORM_EPS) * g_ref[...]
    row0 = pl.program_id(0) * tm
    sh = _row_select(row0, tm, n_ctx, sh_ref)
    sc = _row_select(row0, tm, n_ctx, sc_ref)
    o_ref[...] = (y * (1.0 + sc) + sh).astype(o_ref.dtype)


def _norm_mod_router_kernel(x_ref, g_ref, sh_ref, sc_ref, rw_ref, o_ref, ow_ref, lt_ref, *, tm, n_ctx):
    x = x_ref[...]
    y = x * lax.rsqrt(jnp.mean(x * x, axis=-1, keepdims=True) + NORM_EPS) * g_ref[...]
    row0 = pl.program_id(0) * tm
    sh = _row_select(row0, tm, n_ctx, sh_ref)
    sc = _row_select(row0, tm, n_ctx, sc_ref)
    t = y * (1.0 + sc) + sh
    o_ref[...] = t.astype(o_ref.dtype)
    _store_word_rows(ow_ref, _pack_words(t))
    lt_ref[...] = lax.dot_general(rw_ref[...], t, (((1,), (1,)), ((), ())),
                                  preferred_element_type=F32, precision=lax.Precision.HIGHEST)


def norm_modulate(x, g, shift2, scale2, n_ctx, router_wt=None):
    t, d = x.shape
    tm = _pick(t, (256, 128))
    vec = pl.BlockSpec((1, d), lambda i: (0, 0))
    two = pl.BlockSpec((2, d), lambda i: (0, 0))
    xs = pl.BlockSpec((tm, d), lambda i: (i, 0))
    if router_wt is None:
        return pl.pallas_call(
            functools.partial(_norm_mod_kernel, tm=tm, n_ctx=n_ctx),
            grid=(t // tm,), in_specs=[xs, vec, two, two], out_specs=xs,
            out_shape=jax.ShapeDtypeStruct((t, d), BF16),
            compiler_params=_cparams("parallel"),
        )(x, g.reshape(1, d), shift2, scale2)
    e = router_wt.shape[0]
    row_words = (d // 2 // LANE, LANE)
    return pl.pallas_call(
        functools.partial(_norm_mod_router_kernel, tm=tm, n_ctx=n_ctx),
        grid=(t // tm,),
        in_specs=[xs, vec, two, two, pl.BlockSpec((e, d), lambda i: (0, 0))],
        out_specs=[xs, pl.BlockSpec((tm,) + row_words, lambda i: (i, 0, 0)), pl.BlockSpec((e, tm), lambda i: (0, i))],
        out_shape=[jax.ShapeDtypeStruct((t, d), BF16), jax.ShapeDtypeStruct((t,) + row_words, jnp.uint32),
                   jax.ShapeDtypeStruct((e, t), F32)],
        compiler_params=_cparams("parallel"),
        name="norm_modulate_router",
    )(x, g.reshape(1, d), shift2, scale2, router_wt)


def _final_norm_kernel(x_ref, g_ref, o_ref):
    x = x_ref[...]
    o_ref[...] = x * lax.rsqrt(jnp.mean(x * x, axis=-1, keepdims=True) + NORM_EPS) * g_ref[...]


def final_norm(x, g, row_start):
    t, d = x.shape
    tm = _pick(t, (256, 128))
    assert row_start % tm == 0
    off = row_start // tm
    return pl.pallas_call(
        _final_norm_kernel, grid=((t - row_start) // tm,),
        in_specs=[pl.BlockSpec((tm, d), lambda i: (i + off, 0)), pl.BlockSpec((1, d), lambda i: (0, 0))],
        out_specs=pl.BlockSpec((tm, d), lambda i: (i, 0)),
        out_shape=jax.ShapeDtypeStruct((t - row_start, d), F32),
        compiler_params=_cparams("parallel"),
    )(x, g.reshape(1, d))


def _mm_kernel(a_ref, b_ref, o_ref):
    o_ref[...] = jnp.dot(a_ref[...], b_ref[...], preferred_element_type=F32).astype(o_ref.dtype)


def _mm_tiles(m, k, n):
    tm = _pick(m, (1280, 1024, 640, 512, 256, 128))
    if k > 4096:
        tm = _pick(m, (640, 512, 256, 128))
    tn = _pick(n, (512, 256, 128))
    return tm, tn


def matmul(a, b, out_dtype):
    m, k = a.shape
    n = b.shape[1]
    tm, tn = _mm_tiles(m, k, n)
    return pl.pallas_call(
        _mm_kernel, grid=(m // tm, n // tn),
        in_specs=[pl.BlockSpec((tm, k), lambda i, j: (i, 0)), pl.BlockSpec((k, tn), lambda i, j: (0, j))],
        out_specs=pl.BlockSpec((tm, tn), lambda i, j: (i, j)),
        out_shape=jax.ShapeDtypeStruct((m, n), out_dtype),
        compiler_params=_cparams("parallel", "parallel"),
    )(a, b)


def _mm_res_kernel(a_ref, b_ref, r_ref, g_ref, o_ref, *, tm, n_ctx):
    acc = jnp.dot(a_ref[...], b_ref[...], preferred_element_type=F32)
    gate = _row_select(pl.program_id(0) * tm, tm, n_ctx, g_ref)
    o_ref[...] = r_ref[...] + gate * acc


def matmul_residual(a, b, resid, gate2, n_ctx):
    m, k = a.shape
    n = b.shape[1]
    tm, tn = _mm_tiles(m, k, n)
    return pl.pallas_call(
        functools.partial(_mm_res_kernel, tm=tm, n_ctx=n_ctx), grid=(m // tm, n // tn),
        in_specs=[pl.BlockSpec((tm, k), lambda i, j: (i, 0)), pl.BlockSpec((k, tn), lambda i, j: (0, j)),
                  pl.BlockSpec((tm, tn), lambda i, j: (i, j)), pl.BlockSpec((2, tn), lambda i, j: (0, j))],
        out_specs=pl.BlockSpec((tm, tn), lambda i, j: (i, j)),
        out_shape=jax.ShapeDtypeStruct((m, n), F32),
        compiler_params=_cparams("parallel", "parallel"),
    )(a, b, resid, gate2)


def _rope_tables(n_ctx, n_lat):
    rows = n_lat // GRID_W
    row = jnp.broadcast_to(jnp.arange(rows, dtype=F32)[:, None], (rows, GRID_W)).reshape(-1)
    col = jnp.broadcast_to(jnp.arange(GRID_W, dtype=F32)[None, :], (rows, GRID_W)).reshape(-1)
    n_freq = QK_ROPE // 4
    inv = ROPE_THETA ** (-jnp.arange(n_freq, dtype=F32) / n_freq)
    ang = jnp.concatenate([row[:, None] * inv, col[:, None] * inv], axis=-1)
    ang = jnp.concatenate([jnp.zeros((n_ctx, QK_ROPE // 2), F32), ang], axis=0)
    return jnp.tile(jnp.cos(ang), (1, 4)), jnp.tile(jnp.sin(ang), (1, 4))


def _rot_cols(w):
    h = QK_ROPE // 2
    return jnp.concatenate([-w[..., h:], w[..., :h]], axis=-1)


def _mla_lat_post_kernel(lat_ref, qg_ref, kg_ref, cc_ref, ss_ref, cq_ref, ckv_ref, kr_ref, *, q_rank, kv_rank):
    cq = lat_ref[:, :q_rank]
    cq_ref[...] = (cq * lax.rsqrt(jnp.mean(cq * cq, axis=-1, keepdims=True) + NORM_EPS)
                   * qg_ref[...]).astype(cq_ref.dtype)
    ckv = lat_ref[:, q_rank:q_rank + kv_rank]
    ckv_ref[...] = (ckv * lax.rsqrt(jnp.mean(ckv * ckv, axis=-1, keepdims=True) + NORM_EPS)
                    * kg_ref[...]).astype(ckv_ref.dtype)
    base = q_rank + kv_rank
    cc = cc_ref[...]
    ss = ss_ref[...]
    lo = lat_ref[:, base:base + LANE] * cc + lat_ref[:, base + LANE:base + 2 * LANE] * ss
    hi = lat_ref[:, base + 2 * LANE:base + 3 * LANE] * cc + lat_ref[:, base + 3 * LANE:base + 4 * LANE] * ss
    kr_ref[:, :LANE] = lo.astype(kr_ref.dtype)
    kr_ref[:, LANE:] = hi.astype(kr_ref.dtype)


def mla_lat_post(lat, q_norm, kv_norm, cc, ss):
    t = lat.shape[0]
    q_rank, kv_rank = q_norm.shape[0], kv_norm.shape[0]
    tm = _pick(t, (256, 128))
    row = lambda w: pl.BlockSpec((tm, w), lambda i: (i, 0))
    return pl.pallas_call(
        functools.partial(_mla_lat_post_kernel, q_rank=q_rank, kv_rank=kv_rank), grid=(t // tm,),
        in_specs=[row(lat.shape[1]), pl.BlockSpec((1, q_rank), lambda i: (0, 0)),
                  pl.BlockSpec((1, kv_rank), lambda i: (0, 0)), row(LANE), row(LANE)],
        out_specs=[row(q_rank), row(kv_rank), row(2 * LANE)],
        out_shape=[jax.ShapeDtypeStruct((t, q_rank), BF16), jax.ShapeDtypeStruct((t, kv_rank), BF16),
                   jax.ShapeDtypeStruct((t, 2 * LANE), BF16)],
        compiler_params=_cparams("parallel"),
    )(lat, q_norm.reshape(1, -1), kv_norm.reshape(1, -1), cc, ss)


def _q_proj_kernel(a_ref, b_ref, cc_ref, ss_ref, qn_ref, qr_ref):
    acc = jnp.dot(a_ref[...], b_ref[...], preferred_element_type=F32)
    qn_ref[...] = acc[:, :2 * LANE].astype(qn_ref.dtype)
    qr_ref[...] = (acc[:, 2 * LANE:3 * LANE] * cc_ref[...]
                   + acc[:, 3 * LANE:4 * LANE] * ss_ref[...]).astype(qr_ref.dtype)


def q_proj(cqn, w_pairs, cc, ss):
    t, k = cqn.shape
    n_pairs = w_pairs.shape[1] // (4 * LANE)
    tm = _pick(t, (1280, 1024, 640, 512, 256, 128))
    return pl.pallas_call(
        _q_proj_kernel, grid=(t // tm, n_pairs),
        in_specs=[pl.BlockSpec((tm, k), lambda i, j: (i, 0)), pl.BlockSpec((k, 4 * LANE), lambda i, j: (0, j)),
                  pl.BlockSpec((tm, LANE), lambda i, j: (i, 0)), pl.BlockSpec((tm, LANE), lambda i, j: (i, 0))],
        out_specs=[pl.BlockSpec((tm, 2 * LANE), lambda i, j: (i, j)), pl.BlockSpec((tm, LANE), lambda i, j: (i, j))],
        out_shape=[jax.ShapeDtypeStruct((t, n_pairs * 2 * LANE), BF16),
                   jax.ShapeDtypeStruct((t, n_pairs * LANE), BF16)],
        compiler_params=_cparams("parallel", "parallel"),
    )(cqn, w_pairs, cc, ss)


def _attn_kernel(qn_ref, qr_ref, kn_ref, kr_ref, vt_ref, o_ref, s_scr, acc_scr, *, tq, tk, unroll, n_ctx, n_all):
    q = jnp.concatenate([qn_ref[...], qr_ref[...]], axis=1)
    qt = q.astype(F32).T.astype(BF16)

    def scores(slot, off, size):
        k = jnp.concatenate([kn_ref[pl.ds(off, size), :], kr_ref[pl.ds(off, size), :]], axis=1)
        s = jnp.dot(k, qt, preferred_element_type=F32)
        s_scr[slot, 0:size, :] = s
        return jnp.max(s, axis=0, keepdims=True)

    def update(slot, mx, off, size, carry):
        m, l = carry
        m_new = jnp.maximum(m, mx)
        alpha = jnp.exp2(m - m_new)
        p = jnp.exp2(s_scr[slot, 0:size, :] - m_new)
        l = alpha * l + jnp.sum(p, axis=0, keepdims=True)
        acc_scr[...] = alpha * acc_scr[...] + jnp.dot(vt_ref[:, pl.ds(off, size)], p.astype(BF16),
                                                      preferred_element_type=F32)
        return m_new, l

    def finish(carry):
        o_ref[...] = (acc_scr[...] / carry[1]).T.astype(o_ref.dtype)

    acc_scr[...] = jnp.zeros(acc_scr.shape, F32)
    init = (jnp.full((1, tq), -jnp.inf, F32), jnp.zeros((1, tq), F32))
    is_ctx = pl.program_id(1) * tq < n_ctx

    @pl.when(is_ctx)
    def _():
        finish(update(0, scores(0, 0, n_ctx), 0, n_ctx, init))

    @pl.when(jnp.logical_not(is_ctx))
    def _():
        n_chunks = n_all // tk

        def run(base, count, carry, lookahead):
            mx, st = carry[0], carry[1:]
            for u in range(count):
                off = base + u * tk
                more = u + 1 < count or lookahead
                mx_next = scores((u + 1) % 2, off + tk, tk) if more else None
                st = update(u % 2, mx, off, tk, st)
                mx = mx_next
            return (mx,) + st

        def body(i, carry):
            return run(pl.multiple_of(i * (unroll * tk), tk), unroll, carry, True)

        n_trips = (n_chunks - 1) // unroll
        carry = lax.fori_loop(0, n_trips, body, (scores(0, 0, tk),) + init)
        finish(run(n_trips * unroll * tk, n_chunks - n_trips * unroll, carry, False)[1:])


def attention(qn, qr, kn, kr2, vt, n_ctx):
    t = qn.shape[0]
    heads = qn.shape[1] // QK_NOPE
    tq = _pick(n_ctx, (256, 128))
    tk = _pick(t, (1280, 512, 256, 128))
    assert n_ctx % tq == 0 and t % tq == 0 and n_ctx <= tk
    return pl.pallas_call(
        functools.partial(_attn_kernel, tq=tq, tk=tk, unroll=12, n_ctx=n_ctx, n_all=t),
        grid=(heads, t // tq),
        in_specs=[pl.BlockSpec((tq, LANE), lambda h, i: (i, h)),
                  pl.BlockSpec((tq, LANE), lambda h, i: (i, h // 2)),
                  pl.BlockSpec((t, LANE), lambda h, i: (0, h)),
                  pl.BlockSpec((t, LANE), lambda h, i: (0, h % 2)),
                  pl.BlockSpec((V_HEAD, t), lambda h, i: (h, 0))],
        out_specs=pl.BlockSpec((tq, LANE), lambda h, i: (i, h)),
        out_shape=jax.ShapeDtypeStruct((t, heads * V_HEAD), BF16),
        scratch_shapes=[pltpu.VMEM((2, tk, tq), F32), pltpu.VMEM((V_HEAD, tq), F32)],
        compiler_params=_cparams("parallel", "arbitrary"),
        name="mla_attention",
    )(qn, qr, kn, kr2, vt)


def _mm_nt_kernel(a_ref, b_ref, o_ref):
    o_ref[...] = lax.dot_general(a_ref[...], b_ref[...], (((1,), (1,)), ((), ())),
                                 preferred_element_type=F32).astype(o_ref.dtype)


def matmul_nt(a, b, out_dtype):
    m, k = a.shape
    n = b.shape[0]
    tm = _pick(m, (1024, 512, 256, 128))
    tn = _pick(n, (1280, 1024, 640, 512, 256, 128))
    return pl.pallas_call(
        _mm_nt_kernel, grid=(m // tm, n // tn),
        in_specs=[pl.BlockSpec((tm, k), lambda i, j: (i, 0)), pl.BlockSpec((tn, k), lambda i, j: (j, 0))],
        out_specs=pl.BlockSpec((tm, tn), lambda i, j: (i, j)),
        out_shape=jax.ShapeDtypeStruct((m, n), out_dtype),
        compiler_params=_cparams("parallel", "parallel"),
        name="matmul_nt",
    )(a, b)


def mla_layer(x, hn, p, gate2, n_ctx, cc, ss):
    lat = matmul(hn, p["w_in"], F32)
    cqn, ckvn, kr2 = mla_lat_post(lat, p["q_norm"], p["kv_norm"], cc, ss)
    qn, qr = q_proj(cqn, p["w_uq"], cc, ss)
    kn = matmul(ckvn, p["w_uk"], BF16)
    vt = matmul_nt(p["w_uv_t"], ckvn, BF16)
    o = attention(qn, qr, kn, kr2, vt, n_ctx)
    return matmul_residual(o, p["w_o"], x, gate2, n_ctx)


def prep_mla_weights(w_in, q_norm, kv_norm, w_uq, w_ukv, w_o):
    d = w_in.shape[0]
    heads = d // 128
    q_rank, kv_rank = q_norm.shape[0], kv_norm.shape[0]
    z64 = jnp.zeros((d, QK_ROPE), F32)
    wkr = w_in[:, q_rank + kv_rank:]
    wkr_rot = _rot_cols(wkr)
    w_in_p = jnp.concatenate([w_in[:, :q_rank + kv_rank], wkr, z64, wkr_rot, z64, z64, wkr, z64, wkr_rot], axis=1)
    scale = (QK_NOPE + QK_ROPE) ** -0.5 * math.log2(math.e)
    wq = (w_uq * scale).reshape(q_rank, heads, QK_NOPE + QK_ROPE)
    nope = wq[..., :QK_NOPE].reshape(q_rank, heads // 2, 2 * QK_NOPE)
    rope = wq[..., QK_NOPE:]
    rope_a = rope.reshape(q_rank, heads // 2, 2 * QK_ROPE)
    rope_b = _rot_cols(rope).reshape(q_rank, heads // 2, 2 * QK_ROPE)
    w_uq_p = jnp.concatenate([nope, rope_a, rope_b], axis=-1).reshape(q_rank, heads // 2 * 4 * LANE)
    wkv = w_ukv.reshape(kv_rank, heads, QK_NOPE + V_HEAD)
    w_uk = wkv[..., :QK_NOPE].reshape(kv_rank, heads * QK_NOPE)
    w_uv_t = wkv[..., QK_NOPE:].reshape(kv_rank, heads * V_HEAD).T
    return dict(w_in=w_in_p.astype(BF16), q_norm=q_norm, kv_norm=kv_norm, w_uq=w_uq_p.astype(BF16),
                w_uk=w_uk.astype(BF16), w_uv_t=w_uv_t.astype(BF16), w_o=w_o.astype(BF16))


def _conv_silu_kernel(prev_ref, cur_ref, next_ref, w_ref, b_ref, o_ref, win_ref, *, tm, n_ctx, n_tiles):
    i = pl.program_id(0)
    row0 = i * tm
    has_prev = jnp.logical_and(i > 0, row0 != n_ctx)
    has_next = jnp.logical_and(i < n_tiles - 1, row0 + tm != n_ctx)
    h = CONV_HALO
    win_ref[0:h, :] = jnp.where(has_prev, prev_ref[...].astype(F32), 0.0)
    win_ref[h:h + tm, :] = cur_ref[...].astype(F32)
    win_ref[h + tm:2 * h + tm, :] = jnp.where(has_next, next_ref[...].astype(F32), 0.0)
    pad = w_ref.shape[0] // 2
    acc = jnp.zeros(o_ref.shape, F32) + b_ref[...]
    for k in range(w_ref.shape[0]):
        acc = acc + w_ref[k:k + 1, :] * win_ref[pl.ds(h - pad + k, tm), :]
    o_ref[...] = _silu(acc).astype(o_ref.dtype)


CONV_HALO = 16


def conv_silu(u, w, b, n_ctx):
    t, c = u.shape
    tm = _pick(n_ctx, (256, 128))
    tn = _pick(c, (2048, 1024, 512, 256, 128))
    n_tiles = t // tm
    rh = tm // CONV_HALO
    nbh = t // CONV_HALO
    assert w.shape[0] // 2 <= CONV_HALO
    return pl.pallas_call(
        functools.partial(_conv_silu_kernel, tm=tm, n_ctx=n_ctx, n_tiles=n_tiles),
        grid=(n_tiles, c // tn),
        in_specs=[pl.BlockSpec((CONV_HALO, tn), lambda i, j: (jnp.maximum(i * rh - 1, 0), j)),
                  pl.BlockSpec((tm, tn), lambda i, j: (i, j)),
                  pl.BlockSpec((CONV_HALO, tn), lambda i, j: (jnp.minimum((i + 1) * rh, nbh - 1), j)),
                  pl.BlockSpec((w.shape[0], tn), lambda i, j: (0, j)),
                  pl.BlockSpec((1, tn), lambda i, j: (0, j))],
        out_specs=pl.BlockSpec((tm, tn), lambda i, j: (i, j)),
        out_shape=jax.ShapeDtypeStruct((t, c), BF16),
        scratch_shapes=[pltpu.VMEM((tm + 2 * CONV_HALO, tn), F32)],
        compiler_params=_cparams("parallel", "parallel"),
        name="conv_silu",
    )(u, u, u, w, b.reshape(1, c))


def _dt_prep_kernel(raw_ref, bias_ref, a_ref, dtf_ref, cf_ref, dtb_ref, cb_ref, *, nh, q):
    raw = raw_ref[...]
    r = lax.broadcasted_iota(jnp.int32, (q, q), 0)
    c = lax.broadcasted_iota(jnp.int32, (q, q), 1)
    for d, (dt_ref, c_ref) in enumerate(((dtf_ref, cf_ref), (dtb_ref, cb_ref))):
        v = raw[:, d * nh:(d + 1) * nh] + bias_ref[d:d + 1, :]
        dt = jnp.maximum(v, 0.0) + jnp.log1p(jnp.exp(-jnp.abs(v)))
        da = dt * a_ref[d:d + 1, :]
        tri = jnp.where(r >= c, 1.0, 0.0) if d == 0 else jnp.where(r <= c, 1.0, 0.0)
        cum = jnp.dot(tri.astype(F32), da, preferred_element_type=F32, precision=lax.Precision.HIGHEST)
        dt_ref[...] = dt.T
        c_ref[...] = cum.T


def dt_prep(dt_raw, dt_bias, a_log):
    t = dt_raw.shape[0]
    nh = dt_raw.shape[1] // 2
    q = SSM_CHUNK
    a = -jnp.exp(a_log.astype(F32))
    out = jax.ShapeDtypeStruct((nh, t), F32)
    ob = pl.BlockSpec((nh, q), lambda i: (0, i))
    return pl.pallas_call(
        functools.partial(_dt_prep_kernel, nh=nh, q=q), grid=(t // q,),
        in_specs=[pl.BlockSpec((q, 2 * nh), lambda i: (i, 0)), pl.BlockSpec((2, nh), lambda i: (0, 0)),
                  pl.BlockSpec((2, nh), lambda i: (0, 0))],
        out_specs=[ob, ob, ob, ob], out_shape=[out, out, out, out],
        compiler_params=_cparams("parallel"),
    )(dt_raw, dt_bias.astype(F32), a)


def _ssd_kernel(x_ref, b_ref, c_ref, dt_ref, cum_ref, y_ref, s_ref, *, hg, reverse):
    q = SSM_CHUNK
    n_pairs = hg // 2

    @pl.when(pl.program_id(1) == 0)
    def _():
        s_ref[...] = jnp.zeros(s_ref.shape, s_ref.dtype)

    bmat = b_ref[...].astype(F32)
    cmat = c_ref[...].astype(F32)
    cb = lax.dot_general(c_ref[...], b_ref[...], (((1,), (1,)), ((), ())), preferred_element_type=F32)
    bt = bmat.T
    ii = lax.broadcasted_iota(jnp.int32, (q, q), 0)
    jj = lax.broadcasted_iota(jnp.int32, (q, q), 1)
    mask = (jj >= ii) if reverse else (ii >= jj)
    lane_lo = lax.broadcasted_iota(jnp.int32, (q, 2 * SSM_HEAD_DIM), 1) < SSM_HEAD_DIM
    last = 0 if reverse else q - 1

    for pr in range(n_pairs):
        xp = x_ref[:, pr * LANE:(pr + 1) * LANE]
        sp = s_ref[:, pr * LANE:(pr + 1) * LANE]
        rhs = jnp.concatenate([xp, sp.astype(BF16)], axis=0)
        ys, ss, decs = [], [], []
        for hh in range(2):
            h = pr * 2 + hh
            crow = jnp.broadcast_to(cum_ref[h:h + 1, :], (q, q))
            ccol = crow.T
            dtrow = dt_ref[h:h + 1, :]
            seg = jnp.where(mask, ccol - crow, -jnp.inf)
            mm = cb * jnp.exp(seg) * dtrow
            coff = cmat * jnp.exp(ccol)
            lhs = jnp.concatenate([mm.astype(BF16), coff.astype(BF16)], axis=1)
            ys.append(jnp.dot(lhs, rhs, preferred_element_type=F32))
            tot = cum_ref[h:h + 1, last:last + 1]
            wrow = jnp.exp(tot - cum_ref[h:h + 1, :]) * dtrow
            ss.append(jnp.dot((bt * wrow).astype(BF16), xp, preferred_element_type=F32))
            decs.append(jnp.exp(tot))
        y_ref[:, pr * LANE:(pr + 1) * LANE] = jnp.where(lane_lo, ys[0], ys[1]).astype(y_ref.dtype)
        dec = jnp.where(lane_lo[0:1, :], decs[0], decs[1])
        s_ref[:, pr * LANE:(pr + 1) * LANE] = sp * dec + jnp.where(lane_lo, ss[0], ss[1])


def ssd_scan(xbc, dt_t, cum_t, d_inner, n_ctx, reverse):
    t = xbc.shape[0]
    q = SSM_CHUNK
    nh = d_inner // SSM_HEAD_DIM
    hg = nh // SSM_GROUPS
    gw = hg * SSM_HEAD_DIM
    assert gw % LANE == 0 and hg % 8 == 0
    nc = t // q
    ncc = n_ctx // q
    xoff = d_inner // LANE
    if reverse:
        cidx = lambda s: jnp.where(s < ncc, ncc - 1 - s, nc - 1 + ncc - s)
    else:
        cidx = lambda s: s
    return pl.pallas_call(
        functools.partial(_ssd_kernel, hg=hg, reverse=reverse),
        grid=(SSM_GROUPS, nc),
        in_specs=[pl.BlockSpec((q, gw), lambda g, s: (cidx(s), g)),
                  pl.BlockSpec((q, D_STATE), lambda g, s: (cidx(s), xoff + g)),
                  pl.BlockSpec((q, D_STATE), lambda g, s: (cidx(s), xoff + SSM_GROUPS + g)),
                  pl.BlockSpec((hg, q), lambda g, s: (g, cidx(s))),
                  pl.BlockSpec((hg, q), lambda g, s: (g, cidx(s)))],
        out_specs=pl.BlockSpec((q, gw), lambda g, s: (cidx(s), g)),
        out_shape=jax.ShapeDtypeStruct((t, d_inner), BF16),
        scratch_shapes=[pltpu.VMEM((D_STATE, gw), F32)],
        compiler_params=_cparams("parallel", "arbitrary"),
    )(xbc, xbc, xbc, dt_t, cum_t)


def _ssm_finish_kernel(yf_ref, yb_ref, x_ref, z_ref, d_ref, g_ref, o_ref, *, n_groups):
    z = z_ref[...].astype(F32)
    y = (yf_ref[...].astype(F32) + yb_ref[...].astype(F32) + d_ref[...] * x_ref[...].astype(F32)) * _silu(z)
    gw = y.shape[1] // n_groups
    for g in range(n_groups):
        yg = y[:, g * gw:(g + 1) * gw]
        yn = yg * lax.rsqrt(jnp.mean(yg * yg, axis=-1, keepdims=True) + NORM_EPS)
        o_ref[:, g * gw:(g + 1) * gw] = (yn * g_ref[:, g * gw:(g + 1) * gw]).astype(o_ref.dtype)


def ssm_finish(yf, yb, xbc, z, d_exp, norm_g):
    t, di = yf.shape
    tm = _pick(t, (256, 128))
    row = pl.BlockSpec((tm, di), lambda i: (i, 0))
    vec = pl.BlockSpec((1, di), lambda i: (0, 0))
    return pl.pallas_call(
        functools.partial(_ssm_finish_kernel, n_groups=SSM_GROUPS), grid=(t // tm,),
        in_specs=[row, row, row, row, vec, vec], out_specs=row,
        out_shape=jax.ShapeDtypeStruct((t, di), BF16),
        compiler_params=_cparams("parallel"),
    )(yf, yb, xbc, z, d_exp, norm_g.reshape(1, di))


def mamba_layer(x, hn, p, gate2, n_ctx):
    di = p["d_inner"]
    z = matmul(hn, p["w_z"], BF16)
    xbc_raw = matmul(hn, p["w_xbc"], BF16)
    dt_raw = matmul(hn, p["w_dt"], F32)
    xbc = conv_silu(xbc_raw, p["conv_w"], p["conv_b"], n_ctx)
    dtf, cf, dtb, cb = dt_prep(dt_raw, p["dt_bias"], p["a_log"])
    yf = ssd_scan(xbc, dtf, cf, di, n_ctx, reverse=False)
    yb = ssd_scan(xbc, dtb, cb, di, n_ctx, reverse=True)
    yn = ssm_finish(yf, yb, xbc, z, p["d_exp"], p["norm_g"])
    return matmul_residual(yn, p["w_out"], x, gate2, n_ctx)


def prep_mamba_weights(w_in, conv_w, conv_b, dt_bias, a_log, d_skip, norm_g, w_out):
    di = norm_g.shape[0]
    conv_dim = conv_w.shape[1]
    nh = d_skip.shape[0]
    w_dt = w_in[:, di + conv_dim:]
    pad = (-w_dt.shape[1]) % LANE
    if pad:
        w_dt = jnp.concatenate([w_dt, jnp.zeros((w_in.shape[0], pad), F32)], axis=1)
    return dict(d_inner=di, w_z=w_in[:, :di].astype(BF16), w_xbc=w_in[:, di:di + conv_dim].astype(BF16),
                w_dt=w_dt.astype(BF16), conv_w=conv_w, conv_b=conv_b, dt_bias=dt_bias, a_log=a_log,
                d_exp=jnp.repeat(d_skip.astype(F32), SSM_HEAD_DIM).reshape(1, di), norm_g=norm_g,
                w_out=w_out.astype(BF16), n_heads=nh)


def _gating_kernel(lt_ref, b_ref, g_ref, grp_ref, *, n_experts):
    epg = n_experts // N_GROUPS
    s = [1.0 / (1.0 + jnp.exp(-lt_ref[e])) for e in range(n_experts)]
    sel = [s[e] + b_ref[e] for e in range(n_experts)]
    best_val, best_idx = None, None
    for g in range(N_GROUPS):
        mem = sel[g * epg:(g + 1) * epg]
        gs = None
        for a in range(epg):
            for b in range(a + 1, epg):
                pair = mem[a] + mem[b]
                gs = pair if gs is None else jnp.maximum(gs, pair)
        if g == 0:
            best_val, best_idx = gs, jnp.zeros(gs.shape, jnp.int32)
        else:
            better = gs > best_val
            best_val = jnp.where(better, gs, best_val)
            best_idx = jnp.where(better, g, best_idx)
    chosen, denom = [], None
    for e in range(n_experts):
        g = e // epg
        rank = jnp.zeros(best_idx.shape, jnp.int32)
        for m in range(g * epg, (g + 1) * epg):
            if m == e:
                continue
            ahead = (sel[m] > sel[e]) if m > e else (sel[m] >= sel[e])
            rank = rank + ahead.astype(jnp.int32)
        pick = jnp.logical_and(best_idx == g, rank < TOP_K)
        w = jnp.where(pick, s[e], 0.0)
        chosen.append(w)
        denom = w if denom is None else denom + w
    inv = 1.0 / denom
    for e in range(n_experts):
        g_ref[e] = chosen[e] * inv
    grp_ref[...] = best_idx


def gating(logits_t, router_b):
    e, t = logits_t.shape
    r = t // LANE
    lt3 = logits_t.reshape(e, r, LANE)
    gates, grp = pl.pallas_call(
        functools.partial(_gating_kernel, n_experts=e), grid=(1,),
        in_specs=[pl.BlockSpec((e, r, LANE), lambda i: (0, 0, 0)),
                  pl.BlockSpec(memory_space=pltpu.SMEM)],
        out_specs=[pl.BlockSpec((e, r, LANE), lambda i: (0, 0, 0)), pl.BlockSpec((r, LANE), lambda i: (0, 0))],
        out_shape=[jax.ShapeDtypeStruct((e, r, LANE), F32), jax.ShapeDtypeStruct((r, LANE), jnp.int32)],
        compiler_params=_cparams("arbitrary"),
        name="moe_gating",
    )(lt3, router_b.astype(F32))
    return gates.reshape(e, t), grp.reshape(t)


HI_HALF = 0xFFFF0000


def _pack_words(v):
    w = v.shape[1] // 2
    lo = pltpu.bitcast(v[:, :w].astype(jnp.bfloat16).astype(F32), jnp.uint32) >> 16
    hi = pltpu.bitcast(v[:, w:].astype(jnp.bfloat16).astype(F32), jnp.uint32) & jnp.uint32(HI_HALF)
    return lo | hi


def _unpack_words(words):
    return (pltpu.bitcast(words << 16, F32), pltpu.bitcast(words & jnp.uint32(HI_HALF), F32))


def _store_word_rows(ref3, words):
    for k in range(ref3.shape[1]):
        ref3[:, k, :] = words[:, k * LANE:(k + 1) * LANE]


def _load_word_rows(ref3):
    return jnp.concatenate([ref3[:, k, :] for k in range(ref3.shape[1])], axis=1)


def _row_gather_kernel(idx_ref, src_hbm, o_ref, sem, *, blk):
    def row_copy(q, src_row):
        return pltpu.make_async_copy(src_hbm.at[src_row], o_ref.at[q], sem)

    def issue(q, carry):
        row_copy(q, idx_ref[0, 0, q]).start()
        return carry

    def drain(q, carry):
        row_copy(q, 0).wait()
        return carry

    lax.fori_loop(0, blk, issue, 0)
    lax.fori_loop(0, blk, drain, 0)


def row_gather(src3, idx, blk):
    m = idx.shape[0]
    row_words = src3.shape[1:]
    assert m % blk == 0
    return pl.pallas_call(
        functools.partial(_row_gather_kernel, blk=blk), grid=(m // blk,),
        in_specs=[pl.BlockSpec((1, 1, blk), lambda b: (b, 0, 0), memory_space=pltpu.SMEM),
                  pl.BlockSpec(memory_space=pl.ANY)],
        out_specs=pl.BlockSpec((blk,) + row_words, lambda b: (b, 0, 0)),
        out_shape=jax.ShapeDtypeStruct((m,) + row_words, jnp.uint32),
        scratch_shapes=[pltpu.SemaphoreType.DMA(())],
        compiler_params=_cparams("arbitrary"),
        name="row_gather",
    )(idx.reshape(m // blk, 1, blk), src3)


def _group_ffn_kernel(tg_ref, tv_ref, x_ref, wg_ref, wu_ref, wd_ref, gs_ref, o_ref, acc_ref, xb_ref):
    r = pl.program_id(0)
    j = pl.program_id(1)
    f = pl.program_id(2)
    first = jnp.logical_and(j == 0, f == 0)
    last = jnp.logical_and(j == pl.num_programs(1) - 1, f == pl.num_programs(2) - 1)

    @pl.when(first)
    def _():
        acc_ref[...] = jnp.zeros(acc_ref.shape, F32)
        lo, hi = _unpack_words(_load_word_rows(x_ref))
        w = lo.shape[1]
        xb_ref[:, :w] = lo.astype(BF16)
        xb_ref[:, w:] = hi.astype(BF16)

    @pl.when(tv_ref[r] > 0)
    def _():
        x = xb_ref[...]
        hgate = jnp.dot(x, wg_ref[...], preferred_element_type=F32)
        hup = jnp.dot(x, wu_ref[...], preferred_element_type=F32)
        gts = gs_ref[...]
        lane = lax.broadcasted_iota(jnp.int32, gts.shape, 1)
        gcol = jnp.sum(jnp.where(lane == j, gts, 0.0), axis=1, keepdims=True)
        act = (_silu(hgate) * hup * gcol).astype(BF16)
        d = acc_ref.shape[1]
        tc = min(d, 1024)
        for c0 in range(0, d, tc):
            acc_ref[:, c0:c0 + tc] += jnp.dot(act, wd_ref[:, c0:c0 + tc], preferred_element_type=F32)

    @pl.when(last)
    def _():
        _store_word_rows(o_ref, _pack_words(acc_ref[...]))


def group_ffn(xs3, w_gate, w_up, w_down, gates_sorted, tile_group, tile_valid, tm):
    n_rows = xs3.shape[0]
    row_words = xs3.shape[1:]
    n_e, d, df = w_gate.shape
    epg = n_e // N_GROUPS
    tf = _pick(df, (256, 128))
    words = lambda: pl.BlockSpec((tm,) + row_words, lambda r, j, f, tg, tv: (r, 0, 0))
    grid_spec = pltpu.PrefetchScalarGridSpec(
        num_scalar_prefetch=2, grid=(n_rows // tm, epg, df // tf),
        in_specs=[words(),
                  pl.BlockSpec((None, d, tf), lambda r, j, f, tg, tv: (tg[r] * epg + j, 0, f)),
                  pl.BlockSpec((None, d, tf), lambda r, j, f, tg, tv: (tg[r] * epg + j, 0, f)),
                  pl.BlockSpec((None, tf, d), lambda r, j, f, tg, tv: (tg[r] * epg + j, f, 0)),
                  pl.BlockSpec((tm, epg), lambda r, j, f, tg, tv: (r, 0))],
        out_specs=words(),
        scratch_shapes=[pltpu.VMEM((tm, d), F32), pltpu.VMEM((tm, d), BF16)])
    return pl.pallas_call(
        _group_ffn_kernel, grid_spec=grid_spec,
        out_shape=jax.ShapeDtypeStruct((n_rows,) + row_words, jnp.uint32),
        compiler_params=_cparams("parallel", "arbitrary", "arbitrary"),
        name="group_ffn",
    )(tile_group, tile_valid, xs3, w_gate, w_up, w_down, gates_sorted)


def _shared_ffn_kernel(x_ref, wg_ref, wu_ref, wd_ref, y_ref, res_ref, g2_ref, o_ref, *, tm, n_ctx):
    g2 = _row_select(pl.program_id(0) * tm, tm, n_ctx, g2_ref)

    @pl.when(pl.program_id(1) == 0)
    def _():
        y = jnp.concatenate(_unpack_words(_load_word_rows(y_ref)), axis=1)
        o_ref[...] = res_ref[...] + g2 * y

    x = x_ref[...]
    act = (_silu(jnp.dot(x, wg_ref[...], preferred_element_type=F32))
           * jnp.dot(x, wu_ref[...], preferred_element_type=F32)).astype(BF16)
    o_ref[...] += g2 * jnp.dot(act, wd_ref[...], preferred_element_type=F32)


def shared_ffn_combine(hn, w_gate, w_up, w_down, y_routed, resid, gate2, n_ctx):
    t, d = hn.shape
    df = w_gate.shape[1]
    tm = _pick(t, (320, 256, 128))
    tf = _pick(df, (256, 128))
    row = lambda: pl.BlockSpec((tm, d), lambda i, f: (i, 0))
    return pl.pallas_call(
        functools.partial(_shared_ffn_kernel, tm=tm, n_ctx=n_ctx), grid=(t // tm, df // tf),
        in_specs=[row(), pl.BlockSpec((d, tf), lambda i, f: (0, f)), pl.BlockSpec((d, tf), lambda i, f: (0, f)),
                  pl.BlockSpec((tf, d), lambda i, f: (f, 0)),
                  pl.BlockSpec((tm,) + y_routed.shape[1:], lambda i, f: (i, 0, 0)), row(),
                  pl.BlockSpec((2, d), lambda i, f: (0, 0))],
        out_specs=row(),
        out_shape=jax.ShapeDtypeStruct((t, d), F32),
        compiler_params=_cparams("parallel", "arbitrary"),
        name="shared_ffn_combine",
    )(hn, w_gate, w_up, w_down, y_routed, resid, gate2)


def _group_sort_plan(grp, tm):
    t = grp.shape[0]
    n_tiles = (t + N_GROUPS * (tm - 1)) // tm
    n_rows = n_tiles * tm
    onehot = (grp[:, None] == jnp.arange(N_GROUPS, dtype=jnp.int32)[None, :]).astype(jnp.int32)
    rank = jnp.sum((jnp.cumsum(onehot, axis=0) - onehot) * onehot, axis=1)
    counts = jnp.sum(onehot, axis=0)
    padded = (counts + tm - 1) // tm * tm
    ends = jnp.cumsum(padded)
    dest = (ends - padded)[grp] + rank
    src = jnp.zeros((n_rows,), jnp.int32).at[dest].set(jnp.arange(t, dtype=jnp.int32))
    row_valid = jnp.zeros((n_rows,), jnp.int32).at[dest].set(1)
    tile_start = jnp.arange(n_tiles, dtype=jnp.int32) * tm
    tile_group = jnp.minimum(jnp.sum((tile_start[:, None] >= ends[None, :]).astype(jnp.int32), axis=1),
                             N_GROUPS - 1)
    tile_valid = (tile_start < ends[-1]).astype(jnp.int32)
    return dest, src, row_valid, tile_group, tile_valid


def moe_layer(x, hn, hn_words, logits_t, router_b, p, gate2, n_ctx):
    t = hn.shape[0]
    n_e = logits_t.shape[0]
    epg = n_e // N_GROUPS
    tm = _pick(t, (640, 512, 256, 128))
    gates_t, grp = gating(logits_t, router_b)
    dest, src, row_valid, tile_group, tile_valid = _group_sort_plan(grp, tm)
    gate_cols = grp[:, None] * epg + jnp.arange(epg, dtype=jnp.int32)[None, :]
    gates_tok = jnp.take_along_axis(gates_t.T, gate_cols, axis=1)
    gates_sorted = jnp.where(row_valid[:, None] > 0, gates_tok[src], 0.0)
    xs = row_gather(hn_words, src, tm)
    ys = group_ffn(xs, p["w_gate"], p["w_up"], p["w_down"], gates_sorted, tile_group, tile_valid, tm)
    y_tok = row_gather(ys, dest, tm)
    return shared_ffn_combine(hn, p["ws_gate"], p["ws_up"], p["ws_down"], y_tok, x, gate2, n_ctx)


def kernel(x, c, ctx, c_ctx, ada_down, ada_up, ada_bias, norm1_g, norm2_g, final_norm_g, mla_w_in, mla_q_norm, mla_kv_norm, mla_w_uq, mla_w_ukv, mla_w_o, ssm_w_in, ssm_conv_w, ssm_conv_b, ssm_dt_bias, ssm_a_log, ssm_d, ssm_norm_g, ssm_w_out, router_w, router_b, moe_w_gate, moe_w_up, moe_w_down, shared_w_gate, shared_w_up, shared_w_down):
    assert x.shape[0] == 1 and ctx.shape[0] == 1
    depth = ada_down.shape[0]
    n_lat, d = x.shape[1], x.shape[2]
    n_ctx = ctx.shape[1]
    xs = jnp.concatenate([ctx[0], x[0]], axis=0).astype(F32)

    cvecs = jnp.zeros((8, d), F32).at[0].set(c_ctx).at[1].set(c[0])
    mods = ada_modulation(cvecs, ada_down, ada_up, ada_bias)[:, :2]
    mods = mods.reshape(depth, 2, N_MOD, d)
    cc, ss = _rope_tables(n_ctx, n_lat)
    router_wt = router_w.T

    for i in range(depth):
        sh1, sc1, g1, sh2, sc2, g2 = (mods[i, :, k] for k in range(N_MOD))
        j = i // 2
        hn = norm_modulate(xs, norm1_g[i], sh1, sc1, n_ctx)
        if i % 2 == 0:
            p = prep_mla_weights(mla_w_in[j], mla_q_norm[j], mla_kv_norm[j], mla_w_uq[j], mla_w_ukv[j], mla_w_o[j])
            xs = mla_layer(xs, hn, p, g1, n_ctx, cc, ss)
        else:
            p = prep_mamba_weights(ssm_w_in[j], ssm_conv_w[j], ssm_conv_b[j], ssm_dt_bias[j], ssm_a_log[j],
                                   ssm_d[j], ssm_norm_g[j], ssm_w_out[j])
            xs = mamba_layer(xs, hn, p, g1, n_ctx)
        hn2, hn2_words, logits_t = norm_modulate(xs, norm2_g[i], sh2, sc2, n_ctx, router_wt=router_wt)
        pm = dict(w_gate=moe_w_gate[i].astype(BF16), w_up=moe_w_up[i].astype(BF16),
                  w_down=moe_w_down[i].astype(BF16), ws_gate=shared_w_gate[i].astype(BF16),
                  ws_up=shared_w_up[i].astype(BF16), ws_down=shared_w_down[i].astype(BF16))
        xs = moe_layer(xs, hn2, hn2_words, logits_t, router_b, pm, g2, n_ctx)
    return final_norm(xs, final_norm_g, n_ctx)[None]
```

```python
import functools
import math

import jax
import jax.numpy as jnp
from jax import lax
from jax.experimental import pallas as pl
from jax.experimental.pallas import tpu as pltpu

F32 = jnp.float32
BF16 = jnp.bfloat16

GRID_W = 64
ROPE_THETA = 10000.0
NORM_EPS = 1e-6
N_MOD = 6
QK_NOPE = 128
QK_ROPE = 64
V_HEAD = 128
SSM_HEAD_DIM = 64
SSM_GROUPS = 8
D_STATE = 128
SSM_CHUNK = 128
N_GROUPS = 4
TOP_K = 2

LANE = 128
VMEM_LIMIT_BYTES = 56 * 1024 * 1024


def _cparams(*sem):
    return pltpu.CompilerParams(dimension_semantics=sem, vmem_limit_bytes=VMEM_LIMIT_BYTES)


def _pick(n, prefs):
    for p in prefs:
        if p <= n and n % p == 0:
            return p
    return n


def _row_select(row0, tm, n_ctx, ref):
    row = row0 + lax.broadcasted_iota(jnp.int32, (tm, 1), 0)
    return jnp.where(row < n_ctx, ref[0:1, :], ref[1:2, :])


def _silu(v):
    return v * (1.0 / (1.0 + jnp.exp(-v)))


def _ada_kernel(c_ref, wd_ref, wu_ref, b_ref, o_ref):
    cv = c_ref[...]
    h = jnp.dot(_silu(cv), wd_ref[...], preferred_element_type=F32, precision=lax.Precision.HIGHEST)
    o_ref[...] = jnp.dot(h, wu_ref[...], preferred_element_type=F32,
                         precision=lax.Precision.HIGHEST) + b_ref[...]


def ada_modulation(cvecs, w_down, w_up, bias):
    depth, d, r = w_down.shape
    n = w_up.shape[-1]
    tn = _pick(n, (2048, 1024, 512, 256, 128))
    return pl.pallas_call(
        _ada_kernel,
        grid=(depth, n // tn),
        in_specs=[pl.BlockSpec((8, d), lambda l, j: (0, 0)),
                  pl.BlockSpec((None, d, r), lambda l, j: (l, 0, 0)),
                  pl.BlockSpec((None, r, tn), lambda l, j: (l, 0, j)),
                  pl.BlockSpec((None, 1, tn), lambda l, j: (l, 0, j))],
        out_specs=pl.BlockSpec((None, 8, tn), lambda l, j: (l, 0, j)),
        out_shape=jax.ShapeDtypeStruct((depth, 8, n), F32),
        compiler_params=_cparams("parallel", "parallel"),
    )(cvecs, w_down, w_up, bias.reshape(depth, 1, n))


def _norm_mod_kernel(x_ref, g_ref, sh_ref, sc_ref, o_ref, *, tm, n_ctx):
    x = x_ref[...]
    y = x * lax.rsqrt(jnp.mean(x * x, axis=-1, keepdims=True) + NORM_EPS) * g_ref[...]
    row0 = pl.program_id(0) * tm
    sh = _row_select(row0, tm, n_ctx, sh_ref)
    sc = _row_select(row0, tm, n_ctx, sc_ref)
    o_ref[...] = (y * (1.0 + sc) + sh).astype(o_ref.dtype)


def _norm_mod_router_kernel(x_ref, g_ref, sh_ref, sc_ref, rw_ref, o_ref, ow_ref, lt_ref, *, tm, n_ctx):
    x = x_ref[...]
    y = x * lax.rsqrt(jnp.mean(x * x, axis=-1, keepdims=True) + NORM_EPS) * g_ref[...]
    row0 = pl.program_id(0) * tm
    sh = _row_select(row0, tm, n_ctx, sh_ref)
    sc = _row_select(row0, tm, n_ctx, sc_ref)
    t = y * (1.0 + sc) + sh
    o_ref[...] = t.astype(o_ref.dtype)
    _store_word_rows(ow_ref, _pack_words(t))
    lt_ref[...] = lax.dot_general(rw_ref[...], t, (((1,), (1,)), ((), ())),
                                  preferred_element_type=F32, precision=lax.Precision.HIGHEST)


def norm_modulate(x, g, shift2, scale2, n_ctx, router_wt=None):
    t, d = x.shape
    tm = _pick(t, (256, 128))
    vec = pl.BlockSpec((1, d), lambda i: (0, 0))
    two = pl.BlockSpec((2, d), lambda i: (0, 0))
    xs = pl.BlockSpec((tm, d), lambda i: (i, 0))
    if router_wt is None:
        return pl.pallas_call(
            functools.partial(_norm_mod_kernel, tm=tm, n_ctx=n_ctx),
            grid=(t // tm,), in_specs=[xs, vec, two, two], out_specs=xs,
            out_shape=jax.ShapeDtypeStruct((t, d), BF16),
            compiler_params=_cparams("parallel"),
        )(x, g.reshape(1, d), shift2, scale2)
    e = router_wt.shape[0]
    row_words = (d // 2 // LANE, LANE)
    return pl.pallas_call(
        functools.partial(_norm_mod_router_kernel, tm=tm, n_ctx=n_ctx),
        grid=(t // tm,),
        in_specs=[xs, vec, two, two, pl.BlockSpec((e, d), lambda i: (0, 0))],
        out_specs=[xs, pl.BlockSpec((tm,) + row_words, lambda i: (i, 0, 0)), pl.BlockSpec((e, tm), lambda i: (0, i))],
        out_shape=[jax.ShapeDtypeStruct((t, d), BF16), jax.ShapeDtypeStruct((t,) + row_words, jnp.uint32),
                   jax.ShapeDtypeStruct((e, t), F32)],
        compiler_params=_cparams("parallel"),
        name="norm_modulate_router",
    )(x, g.reshape(1, d), shift2, scale2, router_wt)


def _final_norm_kernel(x_ref, g_ref, o_ref):
    x = x_ref[...]
    o_ref[...] = x * lax.rsqrt(jnp.mean(x * x, axis=-1, keepdims=True) + NORM_EPS) * g_ref[...]


def final_norm(x, g, row_start):
    t, d = x.shape
    tm = _pick(t, (256, 128))
    assert row_start % tm == 0
    off = row_start // tm
    return pl.pallas_call(
        _final_norm_kernel, grid=((t - row_start) // tm,),
        in_specs=[pl.BlockSpec((tm, d), lambda i: (i + off, 0)), pl.BlockSpec((1, d), lambda i: (0, 0))],
        out_specs=pl.BlockSpec((tm, d), lambda i: (i, 0)),
        out_shape=jax.ShapeDtypeStruct((t - row_start, d), F32),
        compiler_params=_cparams("parallel"),
    )(x, g.reshape(1, d))


def _mm_kernel(a_ref, b_ref, o_ref):
    o_ref[...] = jnp.dot(a_ref[...], b_ref[...], preferred_element_type=F32).astype(o_ref.dtype)


def _mm_tiles(m, k, n):
    tm = _pick(m, (1280, 1024, 640, 512, 256, 128))
    if k > 4096:
        tm = _pick(m, (640, 512, 256, 128))
    tn = _pick(n, (512, 256, 128))
    return tm, tn


def matmul(a, b, out_dtype):
    m, k = a.shape
    n = b.shape[1]
    tm, tn = _mm_tiles(m, k, n)
    return pl.pallas_call(
        _mm_kernel, grid=(m // tm, n // tn),
        in_specs=[pl.BlockSpec((tm, k), lambda i, j: (i, 0)), pl.BlockSpec((k, tn), lambda i, j: (0, j))],
        out_specs=pl.BlockSpec((tm, tn), lambda i, j: (i, j)),
        out_shape=jax.ShapeDtypeStruct((m, n), out_dtype),
        compiler_params=_cparams("parallel", "parallel"),
    )(a, b)


def _mm_res_kernel(a_ref, b_ref, r_ref, g_ref, o_ref, *, tm, n_ctx):
    acc = jnp.dot(a_ref[...], b_ref[...], preferred_element_type=F32)
    gate = _row_select(pl.program_id(0) * tm, tm, n_ctx, g_ref)
    o_ref[...] = r_ref[...] + gate * acc


def matmul_residual(a, b, resid, gate2, n_ctx):
    m, k = a.shape
    n = b.shape[1]
    tm, tn = _mm_tiles(m, k, n)
    return pl.pallas_call(
        functools.partial(_mm_res_kernel, tm=tm, n_ctx=n_ctx), grid=(m // tm, n // tn),
        in_specs=[pl.BlockSpec((tm, k), lambda i, j: (i, 0)), pl.BlockSpec((k, tn), lambda i, j: (0, j)),
                  pl.BlockSpec((tm, tn), lambda i, j: (i, j)), pl.BlockSpec((2, tn), lambda i, j: (0, j))],
        out_specs=pl.BlockSpec((tm, tn), lambda i, j: (i, j)),
        out_shape=jax.ShapeDtypeStruct((m, n), F32),
        compiler_params=_cparams("parallel", "parallel"),
    )(a, b, resid, gate2)


def _rope_tables(n_ctx, n_lat):
    rows = n_lat // GRID_W
    row = jnp.broadcast_to(jnp.arange(rows, dtype=F32)[:, None], (rows, GRID_W)).reshape(-1)
    col = jnp.broadcast_to(jnp.arange(GRID_W, dtype=F32)[None, :], (rows, GRID_W)).reshape(-1)
    n_freq = QK_ROPE // 4
    inv = ROPE_THETA ** (-jnp.arange(n_freq, dtype=F32) / n_freq)
    ang = jnp.concatenate([row[:, None] * inv, col[:, None] * inv], axis=-1)
    ang = jnp.concatenate([jnp.zeros((n_ctx, QK_ROPE // 2), F32), ang], axis=0)
    return jnp.tile(jnp.cos(ang), (1, 4)), jnp.tile(jnp.sin(ang), (1, 4))


def _rot_cols(w):
    h = QK_ROPE // 2
    return jnp.concatenate([-w[..., h:], w[..., :h]], axis=-1)


def _mla_lat_post_kernel(lat_ref, qg_ref, kg_ref, cc_ref, ss_ref, cq_ref, ckv_ref, kr_ref, *, q_rank, kv_rank):
    cq = lat_ref[:, :q_rank]
    cq_ref[...] = (cq * lax.rsqrt(jnp.mean(cq * cq, axis=-1, keepdims=True) + NORM_EPS)
                   * qg_ref[...]).astype(cq_ref.dtype)
    ckv = lat_ref[:, q_rank:q_rank + kv_rank]
    ckv_ref[...] = (ckv * lax.rsqrt(jnp.mean(ckv * ckv, axis=-1, keepdims=True) + NORM_EPS)
                    * kg_ref[...]).astype(ckv_ref.dtype)
    base = q_rank + kv_rank
    cc = cc_ref[...]
    ss = ss_ref[...]
    lo = lat_ref[:, base:base + LANE] * cc + lat_ref[:, base + LANE:base + 2 * LANE] * ss
    hi = lat_ref[:, base + 2 * LANE:base + 3 * LANE] * cc + lat_ref[:, base + 3 * LANE:base + 4 * LANE] * ss
    kr_ref[:, :LANE] = lo.astype(kr_ref.dtype)
    kr_ref[:, LANE:] = hi.astype(kr_ref.dtype)


def mla_lat_post(lat, q_norm, kv_norm, cc, ss):
    t = lat.shape[0]
    q_rank, kv_rank = q_norm.shape[0], kv_norm.shape[0]
    tm = _pick(t, (256, 128))
    row = lambda w: pl.BlockSpec((tm, w), lambda i: (i, 0))
    return pl.pallas_call(
        functools.partial(_mla_lat_post_kernel, q_rank=q_rank, kv_rank=kv_rank), grid=(t // tm,),
        in_specs=[row(lat.shape[1]), pl.BlockSpec((1, q_rank), lambda i: (0, 0)),
                  pl.BlockSpec((1, kv_rank), lambda i: (0, 0)), row(LANE), row(LANE)],
        out_specs=[row(q_rank), row(kv_rank), row(2 * LANE)],
        out_shape=[jax.ShapeDtypeStruct((t, q_rank), BF16), jax.ShapeDtypeStruct((t, kv_rank), BF16),
                   jax.ShapeDtypeStruct((t, 2 * LANE), BF16)],
        compiler_params=_cparams("parallel"),
    )(lat, q_norm.reshape(1, -1), kv_norm.reshape(1, -1), cc, ss)


def _q_proj_kernel(a_ref, b_ref, cc_ref, ss_ref, qn_ref, qr_ref):
    acc = jnp.dot(a_ref[...], b_ref[...], preferred_element_type=F32)
    qn_ref[...] = acc[:, :2 * LANE].astype(qn_ref.dtype)
    qr_ref[...] = (acc[:, 2 * LANE:3 * LANE] * cc_ref[...]
                   + acc[:, 3 * LANE:4 * LANE] * ss_ref[...]).astype(qr_ref.dtype)


def q_proj(cqn, w_pairs, cc, ss):
    t, k = cqn.shape
    n_pairs = w_pairs.shape[1] // (4 * LANE)
    tm = _pick(t, (1280, 1024, 640, 512, 256, 128))
    return pl.pallas_call(
        _q_proj_kernel, grid=(t // tm, n_pairs),
        in_specs=[pl.BlockSpec((tm, k), lambda i, j: (i, 0)), pl.BlockSpec((k, 4 * LANE), lambda i, j: (0, j)),
                  pl.BlockSpec((tm, LANE), lambda i, j: (i, 0)), pl.BlockSpec((tm, LANE), lambda i, j: (i, 0))],
        out_specs=[pl.BlockSpec((tm, 2 * LANE), lambda i, j: (i, j)), pl.BlockSpec((tm, LANE), lambda i, j: (i, j))],
        out_shape=[jax.ShapeDtypeStruct((t, n_pairs * 2 * LANE), BF16),
                   jax.ShapeDtypeStruct((t, n_pairs * LANE), BF16)],
        compiler_params=_cparams("parallel", "parallel"),
    )(cqn, w_pairs, cc, ss)


def _attn_kernel(qn_ref, qr_ref, kn_ref, kr_ref, vt_ref, o_ref, s_scr, acc_scr, *, tq, tk, unroll, n_ctx, n_all):
    q = jnp.concatenate([qn_ref[...], qr_ref[...]], axis=1)
    qt = q.astype(F32).T.astype(BF16)

    def scores(slot, off, size):
        k = jnp.concatenate([kn_ref[pl.ds(off, size), :], kr_ref[pl.ds(off, size), :]], axis=1)
        s = jnp.dot(k, qt, preferred_element_type=F32)
        s_scr[slot, 0:size, :] = s
        return jnp.max(s, axis=0, keepdims=True)

    def update(slot, mx, off, size, carry):
        m, l = carry
        m_new = jnp.maximum(m, mx)
        alpha = jnp.exp2(m - m_new)
        p = jnp.exp2(s_scr[slot, 0:size, :] - m_new)
        l = alpha * l + jnp.sum(p, axis=0, keepdims=True)
        acc_scr[...] = alpha * acc_scr[...] + jnp.dot(vt_ref[:, pl.ds(off, size)], p.astype(BF16),
                                                      preferred_element_type=F32)
        return m_new, l

    def finish(carry):
        o_ref[...] = (acc_scr[...] / carry[1]).T.astype(o_ref.dtype)

    acc_scr[...] = jnp.zeros(acc_scr.shape, F32)
    init = (jnp.full((1, tq), -jnp.inf, F32), jnp.zeros((1, tq), F32))
    is_ctx = pl.program_id(1) * tq < n_ctx

    @pl.when(is_ctx)
    def _():
        finish(update(0, scores(0, 0, n_ctx), 0, n_ctx, init))

    @pl.when(jnp.logical_not(is_ctx))
    def _():
        n_chunks = n_all // tk

        def run(base, count, carry, lookahead):
            mx, st = carry[0], carry[1:]
            for u in range(count):
                off = base + u * tk
                more = u + 1 < count or lookahead
                mx_next = scores((u + 1) % 2, off + tk, tk) if more else None
                st = update(u % 2, mx, off, tk, st)
                mx = mx_next
            return (mx,) + st

        def body(i, carry):
            return run(pl.multiple_of(i * (unroll * tk), tk), unroll, carry, True)

        n_trips = (n_chunks - 1) // unroll
        carry = lax.fori_loop(0, n_trips, body, (scores(0, 0, tk),) + init)
        finish(run(n_trips * unroll * tk, n_chunks - n_trips * unroll, carry, False)[1:])


def attention(qn, qr, kn, kr2, vt, n_ctx):
    t = qn.shape[0]
    heads = qn.shape[1] // QK_NOPE
    tq = _pick(n_ctx, (256, 128))
    tk = _pick(t, (3328, 1280, 512, 256, 128))
    assert n_ctx % tq == 0 and t % tq == 0 and n_ctx <= tk
    return pl.pallas_call(
        functools.partial(_attn_kernel, tq=tq, tk=tk, unroll=12, n_ctx=n_ctx, n_all=t),
        grid=(heads, t // tq),
        in_specs=[pl.BlockSpec((tq, LANE), lambda h, i: (i, h)),
                  pl.BlockSpec((tq, LANE), lambda h, i: (i, h // 2)),
                  pl.BlockSpec((t, LANE), lambda h, i: (0, h)),
                  pl.BlockSpec((t, LANE), lambda h, i: (0, h % 2)),
                  pl.BlockSpec((V_HEAD, t), lambda h, i: (h, 0))],
        out_specs=pl.BlockSpec((tq, LANE), lambda h, i: (i, h)),
        out_shape=jax.ShapeDtypeStruct((t, heads * V_HEAD), BF16),
        scratch_shapes=[pltpu.VMEM((2, tk, tq), F32), pltpu.VMEM((V_HEAD, tq), F32)],
        compiler_params=_cparams("parallel", "arbitrary"),
        name="mla_attention",
    )(qn, qr, kn, kr2, vt)


def _mm_nt_kernel(a_ref, b_ref, o_ref):
    o_ref[...] = lax.dot_general(a_ref[...], b_ref[...], (((1,), (1,)), ((), ())),
                                 preferred_element_type=F32).astype(o_ref.dtype)


def matmul_nt(a, b, out_dtype):
    m, k = a.shape
    n = b.shape[0]
    tm = _pick(m, (1024, 512, 256, 128))
    tn = _pick(n, (1280, 1024, 640, 512, 256, 128))
    return pl.pallas_call(
        _mm_nt_kernel, grid=(m // tm, n // tn),
        in_specs=[pl.BlockSpec((tm, k), lambda i, j: (i, 0)), pl.BlockSpec((tn, k), lambda i, j: (j, 0))],
        out_specs=pl.BlockSpec((tm, tn), lambda i, j: (i, j)),
        out_shape=jax.ShapeDtypeStruct((m, n), out_dtype),
        compiler_params=_cparams("parallel", "parallel"),
        name="matmul_nt",
    )(a, b)


def mla_layer(x, hn, p, gate2, n_ctx, cc, ss):
    lat = matmul(hn, p["w_in"], F32)
    cqn, ckvn, kr2 = mla_lat_post(lat, p["q_norm"], p["kv_norm"], cc, ss)
    qn, qr = q_proj(cqn, p["w_uq"], cc, ss)
    kn = matmul(ckvn, p["w_uk"], BF16)
    vt = matmul_nt(p["w_uv_t"], ckvn, BF16)
    o = attention(qn, qr, kn, kr2, vt, n_ctx)
    return matmul_residual(o, p["w_o"], x, gate2, n_ctx)


def prep_mla_weights(w_in, q_norm, kv_norm, w_uq, w_ukv, w_o):
    d = w_in.shape[0]
    heads = d // 128
    q_rank, kv_rank = q_norm.shape[0], kv_norm.shape[0]
    z64 = jnp.zeros((d, QK_ROPE), F32)
    wkr = w_in[:, q_rank + kv_rank:]
    wkr_rot = _rot_cols(wkr)
    w_in_p = jnp.concatenate([w_in[:, :q_rank + kv_rank], wkr, z64, wkr_rot, z64, z64, wkr, z64, wkr_rot], axis=1)
    scale = (QK_NOPE + QK_ROPE) ** -0.5 * math.log2(math.e)
    wq = (w_uq * scale).reshape(q_rank, heads, QK_NOPE + QK_ROPE)
    nope = wq[..., :QK_NOPE].reshape(q_rank, heads // 2, 2 * QK_NOPE)
    rope = wq[..., QK_NOPE:]
    rope_a = rope.reshape(q_rank, heads // 2, 2 * QK_ROPE)
    rope_b = _rot_cols(rope).reshape(q_rank, heads // 2, 2 * QK_ROPE)
    w_uq_p = jnp.concatenate([nope, rope_a, rope_b], axis=-1).reshape(q_rank, heads // 2 * 4 * LANE)
    wkv = w_ukv.reshape(kv_rank, heads, QK_NOPE + V_HEAD)
    w_uk = wkv[..., :QK_NOPE].reshape(kv_rank, heads * QK_NOPE)
    w_uv_t = wkv[..., QK_NOPE:].reshape(kv_rank, heads * V_HEAD).T
    return dict(w_in=w_in_p.astype(BF16), q_norm=q_norm, kv_norm=kv_norm, w_uq=w_uq_p.astype(BF16),
                w_uk=w_uk.astype(BF16), w_uv_t=w_uv_t.astype(BF16), w_o=w_o.astype(BF16))


def _conv_silu_kernel(prev_ref, cur_ref, next_ref, w_ref, b_ref, o_ref, win_ref, *, tm, n_ctx, n_tiles):
    i = pl.program_id(0)
    row0 = i * tm
    has_prev = jnp.logical_and(i > 0, row0 != n_ctx)
    has_next = jnp.logical_and(i < n_tiles - 1, row0 + tm != n_ctx)
    h = CONV_HALO
    win_ref[0:h, :] = jnp.where(has_prev, prev_ref[...].astype(F32), 0.0)
    win_ref[h:h + tm, :] = cur_ref[...].astype(F32)
    win_ref[h + tm:2 * h + tm, :] = jnp.where(has_next, next_ref[...].astype(F32), 0.0)
    pad = w_ref.shape[0] // 2
    acc = jnp.zeros(o_ref.shape, F32) + b_ref[...]
    for k in range(w_ref.shape[0]):
        acc = acc + w_ref[k:k + 1, :] * win_ref[pl.ds(h - pad + k, tm), :]
    o_ref[...] = _silu(acc).astype(o_ref.dtype)


CONV_HALO = 16


def conv_silu(u, w, b, n_ctx):
    t, c = u.shape
    tm = _pick(n_ctx, (256, 128))
    tn = _pick(c, (2048, 1024, 512, 256, 128))
    n_tiles = t // tm
    rh = tm // CONV_HALO
    nbh = t // CONV_HALO
    assert w.shape[0] // 2 <= CONV_HALO
    return pl.pallas_call(
        functools.partial(_conv_silu_kernel, tm=tm, n_ctx=n_ctx, n_tiles=n_tiles),
        grid=(n_tiles, c // tn),
        in_specs=[pl.BlockSpec((CONV_HALO, tn), lambda i, j: (jnp.maximum(i * rh - 1, 0), j)),
                  pl.BlockSpec((tm, tn), lambda i, j: (i, j)),
                  pl.BlockSpec((CONV_HALO, tn), lambda i, j: (jnp.minimum((i + 1) * rh, nbh - 1), j)),
                  pl.BlockSpec((w.shape[0], tn), lambda i, j: (0, j)),
                  pl.BlockSpec((1, tn), lambda i, j: (0, j))],
        out_specs=pl.BlockSpec((tm, tn), lambda i, j: (i, j)),
        out_shape=jax.ShapeDtypeStruct((t, c), BF16),
        scratch_shapes=[pltpu.VMEM((tm + 2 * CONV_HALO, tn), F32)],
        compiler_params=_cparams("parallel", "parallel"),
        name="conv_silu",
    )(u, u, u, w, b.reshape(1, c))


def _dt_prep_kernel(raw_ref, bias_ref, a_ref, dtf_ref, cf_ref, dtb_ref, cb_ref, *, nh, q):
    raw = raw_ref[...]
    r = lax.broadcasted_iota(jnp.int32, (q, q), 0)
    c = lax.broadcasted_iota(jnp.int32, (q, q), 1)
    for d, (dt_ref, c_ref) in enumerate(((dtf_ref, cf_ref), (dtb_ref, cb_ref))):
        v = raw[:, d * nh:(d + 1) * nh] + bias_ref[d:d + 1, :]
        dt = jnp.maximum(v, 0.0) + jnp.log1p(jnp.exp(-jnp.abs(v)))
        da = dt * a_ref[d:d + 1, :]
        tri = jnp.where(r >= c, 1.0, 0.0) if d == 0 else jnp.where(r <= c, 1.0, 0.0)
        cum = jnp.dot(tri.astype(F32), da, preferred_element_type=F32, precision=lax.Precision.HIGHEST)
        dt_ref[...] = dt.T
        c_ref[...] = cum.T


def dt_prep(dt_raw, dt_bias, a_log):
    t = dt_raw.shape[0]
    nh = dt_raw.shape[1] // 2
    q = SSM_CHUNK
    a = -jnp.exp(a_log.astype(F32))
    out = jax.ShapeDtypeStruct((nh, t), F32)
    ob = pl.BlockSpec((nh, q), lambda i: (0, i))
    return pl.pallas_call(
        functools.partial(_dt_prep_kernel, nh=nh, q=q), grid=(t // q,),
        in_specs=[pl.BlockSpec((q, 2 * nh), lambda i: (i, 0)), pl.BlockSpec((2, nh), lambda i: (0, 0)),
                  pl.BlockSpec((2, nh), lambda i: (0, 0))],
        out_specs=[ob, ob, ob, ob], out_shape=[out, out, out, out],
        compiler_params=_cparams("parallel"),
    )(dt_raw, dt_bias.astype(F32), a)


def _ssd_kernel(x_ref, b_ref, c_ref, dt_ref, cum_ref, y_ref, s_ref, *, hg, reverse, n_sub):
    q = SSM_CHUNK
    n_pairs = hg // 2

    @pl.when(pl.program_id(1) == 0)
    def _():
        s_ref[...] = jnp.zeros(s_ref.shape, s_ref.dtype)

    ii = lax.broadcasted_iota(jnp.int32, (q, q), 0)
    jj = lax.broadcasted_iota(jnp.int32, (q, q), 1)
    mask = (jj >= ii) if reverse else (ii >= jj)
    lane_lo = lax.broadcasted_iota(jnp.int32, (q, 2 * SSM_HEAD_DIM), 1) < SSM_HEAD_DIM
    last = 0 if reverse else q - 1

    for sub in (range(n_sub - 1, -1, -1) if reverse else range(n_sub)):
        r0 = sub * q
        bsub = b_ref[r0:r0 + q, :]
        csub = c_ref[r0:r0 + q, :]
        cmat = csub.astype(F32)
        cb = lax.dot_general(csub, bsub, (((1,), (1,)), ((), ())), preferred_element_type=F32)
        bt = bsub.astype(F32).T
        for pr in range(n_pairs):
            xp = x_ref[r0:r0 + q, pr * LANE:(pr + 1) * LANE]
            sp = s_ref[:, pr * LANE:(pr + 1) * LANE]
            rhs = jnp.concatenate([xp, sp.astype(BF16)], axis=0)
            ys, ss, decs = [], [], []
            for hh in range(2):
                h = pr * 2 + hh
                cum_h = cum_ref[h:h + 1, r0:r0 + q]
                crow = jnp.broadcast_to(cum_h, (q, q))
                ccol = crow.T
                dtrow = dt_ref[h:h + 1, r0:r0 + q]
                seg = jnp.where(mask, ccol - crow, -jnp.inf)
                mm = cb * jnp.exp(seg) * dtrow
                coff = cmat * jnp.exp(ccol)
                lhs = jnp.concatenate([mm.astype(BF16), coff.astype(BF16)], axis=1)
                ys.append(jnp.dot(lhs, rhs, preferred_element_type=F32))
                tot = cum_h[:, last:last + 1]
                wrow = jnp.exp(tot - cum_h) * dtrow
                ss.append(jnp.dot((bt * wrow).astype(BF16), xp, preferred_element_type=F32))
                decs.append(jnp.exp(tot))
            y_ref[r0:r0 + q, pr * LANE:(pr + 1) * LANE] = jnp.where(lane_lo, ys[0], ys[1]).astype(y_ref.dtype)
            dec = jnp.where(lane_lo[0:1, :], decs[0], decs[1])
            s_ref[:, pr * LANE:(pr + 1) * LANE] = sp * dec + jnp.where(lane_lo, ss[0], ss[1])


def ssd_scan(xbc, dt_t, cum_t, d_inner, n_ctx, reverse):
    t = xbc.shape[0]
    q = SSM_CHUNK
    nh = d_inner // SSM_HEAD_DIM
    hg = nh // SSM_GROUPS
    gw = hg * SSM_HEAD_DIM
    assert gw % LANE == 0 and hg % 8 == 0
    n_sub = 2 if (t // q) % 2 == 0 and (n_ctx // q) % 2 == 0 else 1
    rows = n_sub * q
    nc = t // rows
    ncc = n_ctx // rows
    xoff = d_inner // LANE
    if reverse:
        cidx = lambda s: jnp.where(s < ncc, ncc - 1 - s, nc - 1 + ncc - s)
    else:
        cidx = lambda s: s
    return pl.pallas_call(
        functools.partial(_ssd_kernel, hg=hg, reverse=reverse, n_sub=n_sub),
        grid=(SSM_GROUPS, nc),
        in_specs=[pl.BlockSpec((rows, gw), lambda g, s: (cidx(s), g)),
                  pl.BlockSpec((rows, D_STATE), lambda g, s: (cidx(s), xoff + g)),
                  pl.BlockSpec((rows, D_STATE), lambda g, s: (cidx(s), xoff + SSM_GROUPS + g)),
                  pl.BlockSpec((hg, rows), lambda g, s: (g, cidx(s))),
                  pl.BlockSpec((hg, rows), lambda g, s: (g, cidx(s)))],
        out_specs=pl.BlockSpec((rows, gw), lambda g, s: (cidx(s), g)),
        out_shape=jax.ShapeDtypeStruct((t, d_inner), BF16),
        scratch_shapes=[pltpu.VMEM((D_STATE, gw), F32)],
        compiler_params=_cparams("parallel", "arbitrary"),
        name="ssd_scan",
    )(xbc, xbc, xbc, dt_t, cum_t)


def _ssm_finish_kernel(yf_ref, yb_ref, x_ref, z_ref, d_ref, g_ref, o_ref, *, n_groups):
    z = z_ref[...].astype(F32)
    y = (yf_ref[...].astype(F32) + yb_ref[...].astype(F32) + d_ref[...] * x_ref[...].astype(F32)) * _silu(z)
    gw = y.shape[1] // n_groups
    for g in range(n_groups):
        yg = y[:, g * gw:(g + 1) * gw]
        yn = yg * lax.rsqrt(jnp.mean(yg * yg, axis=-1, keepdims=True) + NORM_EPS)
        o_ref[:, g * gw:(g + 1) * gw] = (yn * g_ref[:, g * gw:(g + 1) * gw]).astype(o_ref.dtype)


def ssm_finish(yf, yb, xbc, z, d_exp, norm_g):
    t, di = yf.shape
    tm = _pick(t, (256, 128))
    row = pl.BlockSpec((tm, di), lambda i: (i, 0))
    vec = pl.BlockSpec((1, di), lambda i: (0, 0))
    return pl.pallas_call(
        functools.partial(_ssm_finish_kernel, n_groups=SSM_GROUPS), grid=(t // tm,),
        in_specs=[row, row, row, row, vec, vec], out_specs=row,
        out_shape=jax.ShapeDtypeStruct((t, di), BF16),
        compiler_params=_cparams("parallel"),
    )(yf, yb, xbc, z, d_exp, norm_g.reshape(1, di))


def mamba_layer(x, hn, p, gate2, n_ctx):
    di = p["d_inner"]
    z = matmul(hn, p["w_z"], BF16)
    xbc_raw = matmul(hn, p["w_xbc"], BF16)
    dt_raw = matmul(hn, p["w_dt"], F32)
    xbc = conv_silu(xbc_raw, p["conv_w"], p["conv_b"], n_ctx)
    dtf, cf, dtb, cb = dt_prep(dt_raw, p["dt_bias"], p["a_log"])
    yf = ssd_scan(xbc, dtf, cf, di, n_ctx, reverse=False)
    yb = ssd_scan(xbc, dtb, cb, di, n_ctx, reverse=True)
    yn = ssm_finish(yf, yb, xbc, z, p["d_exp"], p["norm_g"])
    return matmul_residual(yn, p["w_out"], x, gate2, n_ctx)


def prep_mamba_weights(w_in, conv_w, conv_b, dt_bias, a_log, d_skip, norm_g, w_out):
    di = norm_g.shape[0]
    conv_dim = conv_w.shape[1]
    nh = d_skip.shape[0]
    w_dt = w_in[:, di + conv_dim:]
    pad = (-w_dt.shape[1]) % LANE
    if pad:
        w_dt = jnp.concatenate([w_dt, jnp.zeros((w_in.shape[0], pad), F32)], axis=1)
    return dict(d_inner=di, w_z=w_in[:, :di].astype(BF16), w_xbc=w_in[:, di:di + conv_dim].astype(BF16),
                w_dt=w_dt.astype(BF16), conv_w=conv_w, conv_b=conv_b, dt_bias=dt_bias, a_log=a_log,
                d_exp=jnp.repeat(d_skip.astype(F32), SSM_HEAD_DIM).reshape(1, di), norm_g=norm_g,
                w_out=w_out.astype(BF16), n_heads=nh)


def _gating_kernel(lt_ref, b_ref, g_ref, grp_ref, *, n_experts):
    epg = n_experts // N_GROUPS
    s = [1.0 / (1.0 + jnp.exp(-lt_ref[e])) for e in range(n_experts)]
    sel = [s[e] + b_ref[e] for e in range(n_experts)]
    best_val, best_idx = None, None
    for g in range(N_GROUPS):
        mem = sel[g * epg:(g + 1) * epg]
        gs = None
        for a in range(epg):
            for b in range(a + 1, epg):
                pair = mem[a] + mem[b]
                gs = pair if gs is None else jnp.maximum(gs, pair)
        if g == 0:
            best_val, best_idx = gs, jnp.zeros(gs.shape, jnp.int32)
        else:
            better = gs > best_val
            best_val = jnp.where(better, gs, best_val)
            best_idx = jnp.where(better, g, best_idx)
    chosen, denom = [], None
    for e in range(n_experts):
        g = e // epg
        rank = jnp.zeros(best_idx.shape, jnp.int32)
        for m in range(g * epg, (g + 1) * epg):
            if m == e:
                continue
            ahead = (sel[m] > sel[e]) if m > e else (sel[m] >= sel[e])
            rank = rank + ahead.astype(jnp.int32)
        pick = jnp.logical_and(best_idx == g, rank < TOP_K)
        w = jnp.where(pick, s[e], 0.0)
        chosen.append(w)
        denom = w if denom is None else denom + w
    inv = 1.0 / denom
    for e in range(n_experts):
        g_ref[e] = chosen[e] * inv
    grp_ref[...] = best_idx


def gating(logits_t, router_b):
    e, t = logits_t.shape
    r = t // LANE
    lt3 = logits_t.reshape(e, r, LANE)
    gates, grp = pl.pallas_call(
        functools.partial(_gating_kernel, n_experts=e), grid=(1,),
        in_specs=[pl.BlockSpec((e, r, LANE), lambda i: (0, 0, 0)),
                  pl.BlockSpec(memory_space=pltpu.SMEM)],
        out_specs=[pl.BlockSpec((e, r, LANE), lambda i: (0, 0, 0)), pl.BlockSpec((r, LANE), lambda i: (0, 0))],
        out_shape=[jax.ShapeDtypeStruct((e, r, LANE), F32), jax.ShapeDtypeStruct((r, LANE), jnp.int32)],
        compiler_params=_cparams("arbitrary"),
        name="moe_gating",
    )(lt3, router_b.astype(F32))
    return gates.reshape(e, t), grp.reshape(t)


HI_HALF = 0xFFFF0000


def _pack_words(v):
    w = v.shape[1] // 2
    lo = pltpu.bitcast(v[:, :w].astype(jnp.bfloat16).astype(F32), jnp.uint32) >> 16
    hi = pltpu.bitcast(v[:, w:].astype(jnp.bfloat16).astype(F32), jnp.uint32) & jnp.uint32(HI_HALF)
    return lo | hi


def _unpack_words(words):
    return (pltpu.bitcast(words << 16, F32), pltpu.bitcast(words & jnp.uint32(HI_HALF), F32))


def _store_word_rows(ref3, words):
    for k in range(ref3.shape[1]):
        ref3[:, k, :] = words[:, k * LANE:(k + 1) * LANE]


def _load_word_rows(ref3):
    return jnp.concatenate([ref3[:, k, :] for k in range(ref3.shape[1])], axis=1)


def _row_gather_kernel(idx_ref, src_hbm, o_ref, sem, *, blk):
    def row_copy(q, src_row):
        return pltpu.make_async_copy(src_hbm.at[src_row], o_ref.at[q], sem)

    def issue(q, carry):
        row_copy(q, idx_ref[0, 0, q]).start()
        return carry

    def drain(q, carry):
        row_copy(q, 0).wait()
        return carry

    lax.fori_loop(0, blk, issue, 0)
    lax.fori_loop(0, blk, drain, 0)


def row_gather(src3, idx, blk):
    m = idx.shape[0]
    row_words = src3.shape[1:]
    assert m % blk == 0
    return pl.pallas_call(
        functools.partial(_row_gather_kernel, blk=blk), grid=(m // blk,),
        in_specs=[pl.BlockSpec((1, 1, blk), lambda b: (b, 0, 0), memory_space=pltpu.SMEM),
                  pl.BlockSpec(memory_space=pl.ANY)],
        out_specs=pl.BlockSpec((blk,) + row_words, lambda b: (b, 0, 0)),
        out_shape=jax.ShapeDtypeStruct((m,) + row_words, jnp.uint32),
        scratch_shapes=[pltpu.SemaphoreType.DMA(())],
        compiler_params=_cparams("arbitrary"),
        name="row_gather",
    )(idx.reshape(m // blk, 1, blk), src3)


def _group_ffn_kernel(tg_ref, tv_ref, x_ref, wg_ref, wu_ref, wd_ref, gs_ref, o_ref, acc_ref, xb_ref):
    r = pl.program_id(0)
    j = pl.program_id(1)
    f = pl.program_id(2)
    first = jnp.logical_and(j == 0, f == 0)
    last = jnp.logical_and(j == pl.num_programs(1) - 1, f == pl.num_programs(2) - 1)

    @pl.when(first)
    def _():
        acc_ref[...] = jnp.zeros(acc_ref.shape, F32)
        lo, hi = _unpack_words(_load_word_rows(x_ref))
        w = lo.shape[1]
        xb_ref[:, :w] = lo.astype(BF16)
        xb_ref[:, w:] = hi.astype(BF16)

    @pl.when(tv_ref[r] > 0)
    def _():
        x = xb_ref[...]
        hgate = jnp.dot(x, wg_ref[...], preferred_element_type=F32)
        hup = jnp.dot(x, wu_ref[...], preferred_element_type=F32)
        gts = gs_ref[...]
        lane = lax.broadcasted_iota(jnp.int32, gts.shape, 1)
        gcol = jnp.sum(jnp.where(lane == j, gts, 0.0), axis=1, keepdims=True)
        act = (_silu(hgate) * hup * gcol).astype(BF16)
        d = acc_ref.shape[1]
        tc = min(d, 1024)
        for c0 in range(0, d, tc):
            acc_ref[:, c0:c0 + tc] += jnp.dot(act, wd_ref[:, c0:c0 + tc], preferred_element_type=F32)

    @pl.when(last)
    def _():
        _store_word_rows(o_ref, _pack_words(acc_ref[...]))


def group_ffn(xs3, w_gate, w_up, w_down, gates_sorted, tile_group, tile_valid, tm):
    n_rows = xs3.shape[0]
    row_words = xs3.shape[1:]
    n_e, d, df = w_gate.shape
    epg = n_e // N_GROUPS
    tf = _pick(df, (256, 128))
    words = lambda: pl.BlockSpec((tm,) + row_words, lambda r, j, f, tg, tv: (r, 0, 0))
    grid_spec = pltpu.PrefetchScalarGridSpec(
        num_scalar_prefetch=2, grid=(n_rows // tm, epg, df // tf),
        in_specs=[words(),
                  pl.BlockSpec((None, d, tf), lambda r, j, f, tg, tv: (tg[r] * epg + j, 0, f)),
                  pl.BlockSpec((None, d, tf), lambda r, j, f, tg, tv: (tg[r] * epg + j, 0, f)),
                  pl.BlockSpec((None, tf, d), lambda r, j, f, tg, tv: (tg[r] * epg + j, f, 0)),
                  pl.BlockSpec((tm, epg), lambda r, j, f, tg, tv: (r, 0))],
        out_specs=words(),
        scratch_shapes=[pltpu.VMEM((tm, d), F32), pltpu.VMEM((tm, d), BF16)])
    return pl.pallas_call(
        _group_ffn_kernel, grid_spec=grid_spec,
        out_shape=jax.ShapeDtypeStruct((n_rows,) + row_words, jnp.uint32),
        compiler_params=_cparams("parallel", "arbitrary", "arbitrary"),
        name="group_ffn",
    )(tile_group, tile_valid, xs3, w_gate, w_up, w_down, gates_sorted)


def _shared_ffn_kernel(x_ref, wg_ref, wu_ref, wd_ref, y_ref, res_ref, g2_ref, o_ref, *, tm, n_ctx):
    g2 = _row_select(pl.program_id(0) * tm, tm, n_ctx, g2_ref)

    @pl.when(pl.program_id(1) == 0)
    def _():
        y = jnp.concatenate(_unpack_words(_load_word_rows(y_ref)), axis=1)
        o_ref[...] = res_ref[...] + g2 * y

    x = x_ref[...]
    act = (_silu(jnp.dot(x, wg_ref[...], preferred_element_type=F32))
           * jnp.dot(x, wu_ref[...], preferred_element_type=F32)).astype(BF16)
    o_ref[...] += g2 * jnp.dot(act, wd_ref[...], preferred_element_type=F32)


def shared_ffn_combine(hn, w_gate, w_up, w_down, y_routed, resid, gate2, n_ctx):
    t, d = hn.shape
    df = w_gate.shape[1]
    tm = _pick(t, (320, 256, 128))
    tf = _pick(df, (256, 128))
    row = lambda: pl.BlockSpec((tm, d), lambda i, f: (i, 0))
    return pl.pallas_call(
        functools.partial(_shared_ffn_kernel, tm=tm, n_ctx=n_ctx), grid=(t // tm, df // tf),
        in_specs=[row(), pl.BlockSpec((d, tf), lambda i, f: (0, f)), pl.BlockSpec((d, tf), lambda i, f: (0, f)),
                  pl.BlockSpec((tf, d), lambda i, f: (f, 0)),
                  pl.BlockSpec((tm,) + y_routed.shape[1:], lambda i, f: (i, 0, 0)), row(),
                  pl.BlockSpec((2, d), lambda i, f: (0, 0))],
        out_specs=row(),
        out_shape=jax.ShapeDtypeStruct((t, d), F32),
        compiler_params=_cparams("parallel", "arbitrary"),
        name="shared_ffn_combine",
    )(hn, w_gate, w_up, w_down, y_routed, resid, gate2)


def _group_sort_plan(grp, tm):
    t = grp.shape[0]
    n_tiles = (t + N_GROUPS * (tm - 1)) // tm
    n_rows = n_tiles * tm
    onehot = (grp[:, None] == jnp.arange(N_GROUPS, dtype=jnp.int32)[None, :]).astype(jnp.int32)
    rank = jnp.sum((jnp.cumsum(onehot, axis=0) - onehot) * onehot, axis=1)
    counts = jnp.sum(onehot, axis=0)
    padded = (counts + tm - 1) // tm * tm
    ends = jnp.cumsum(padded)
    dest = (ends - padded)[grp] + rank
    src = jnp.zeros((n_rows,), jnp.int32).at[dest].set(jnp.arange(t, dtype=jnp.int32))
    row_valid = jnp.zeros((n_rows,), jnp.int32).at[dest].set(1)
    tile_start = jnp.arange(n_tiles, dtype=jnp.int32) * tm
    tile_group = jnp.minimum(jnp.sum((tile_start[:, None] >= ends[None, :]).astype(jnp.int32), axis=1),
                             N_GROUPS - 1)
    tile_valid = (tile_start < ends[-1]).astype(jnp.int32)
    return dest, src, row_valid, tile_group, tile_valid


def moe_layer(x, hn, hn_words, logits_t, router_b, p, gate2, n_ctx):
    t = hn.shape[0]
    n_e = logits_t.shape[0]
    epg = n_e // N_GROUPS
    tm = _pick(t, (640, 512, 256, 128))
    gates_t, grp = gating(logits_t, router_b)
    dest, src, row_valid, tile_group, tile_valid = _group_sort_plan(grp, tm)
    gate_cols = grp[:, None] * epg + jnp.arange(epg, dtype=jnp.int32)[None, :]
    gates_tok = jnp.take_along_axis(gates_t.T, gate_cols, axis=1)
    gates_sorted = jnp.where(row_valid[:, None] > 0, gates_tok[src], 0.0)
    xs = row_gather(hn_words, src, tm)
    ys = group_ffn(xs, p["w_gate"], p["w_up"], p["w_down"], gates_sorted, tile_group, tile_valid, tm)
    y_tok = row_gather(ys, dest, tm)
    return shared_ffn_combine(hn, p["ws_gate"], p["ws_up"], p["ws_down"], y_tok, x, gate2, n_ctx)


def kernel(x, c, ctx, c_ctx, ada_down, ada_up, ada_bias, norm1_g, norm2_g, final_norm_g, mla_w_in, mla_q_norm, mla_kv_norm, mla_w_uq, mla_w_ukv, mla_w_o, ssm_w_in, ssm_conv_w, ssm_conv_b, ssm_dt_bias, ssm_a_log, ssm_d, ssm_norm_g, ssm_w_out, router_w, router_b, moe_w_gate, moe_w_up, moe_w_down, shared_w_gate, shared_w_up, shared_w_down):
    assert x.shape[0] == 1 and ctx.shape[0] == 1
    depth = ada_down.shape[0]
    n_lat, d = x.shape[1], x.shape[2]
    n_ctx = ctx.shape[1]
    xs = jnp.concatenate([ctx[0], x[0]], axis=0).astype(F32)

    cvecs = jnp.zeros((8, d), F32).at[0].set(c_ctx).at[1].set(c[0])
    mods = ada_modulation(cvecs, ada_down, ada_up, ada_bias)[:, :2]
    mods = mods.reshape(depth, 2, N_MOD, d)
    cc, ss = _rope_tables(n_ctx, n_lat)
    router_wt = router_w.T

    for i in range(depth):
        sh1, sc1, g1, sh2, sc2, g2 = (mods[i, :, k] for k in range(N_MOD))
        j = i // 2
        hn = norm_modulate(xs, norm1_g[i], sh1, sc1, n_ctx)
        if i % 2 == 0:
            p = prep_mla_weights(mla_w_in[j], mla_q_norm[j], mla_kv_norm[j], mla_w_uq[j], mla_w_ukv[j], mla_w_o[j])
            xs = mla_layer(xs, hn, p, g1, n_ctx, cc, ss)
        else:
            p = prep_mamba_weights(ssm_w_in[j], ssm_conv_w[j], ssm_conv_b[j], ssm_dt_bias[j], ssm_a_log[j],
                                   ssm_d[j], ssm_norm_g[j], ssm_w_out[j])
            xs = mamba_layer(xs, hn, p, g1, n_ctx)
        hn2, hn2_words, logits_t = norm_modulate(xs, norm2_g[i], sh2, sc2, n_ctx, router_wt=router_wt)
        pm = dict(w_gate=moe_w_gate[i].astype(BF16), w_up=moe_w_up[i].astype(BF16),
                  w_down=moe_w_down[i].astype(BF16), ws_gate=shared_w_gate[i].astype(BF16),
                  ws_up=shared_w_up[i].astype(BF16), ws_down=shared_w_down[i].astype(BF16))
        xs = moe_layer(xs, hn2, hn2_words, logits_t, router_b, pm, g2, n_ctx)
    return final_norm(xs, final_norm_g, n_ctx)[None]
```

```python
import functools
import math

import jax
import jax.numpy as jnp
from jax import lax
from jax.experimental import pallas as pl
from jax.experimental.pallas import tpu as pltpu

F32 = jnp.float32
BF16 = jnp.bfloat16

GRID_W = 64
ROPE_THETA = 10000.0
NORM_EPS = 1e-6
N_MOD = 6
QK_NOPE = 128
QK_ROPE = 64
V_HEAD = 128
SSM_HEAD_DIM = 64
SSM_GROUPS = 8
D_STATE = 128
SSM_CHUNK = 128
N_GROUPS = 4
TOP_K = 2

LANE = 128
VMEM_LIMIT_BYTES = 56 * 1024 * 1024


def _cparams(*sem):
    return pltpu.CompilerParams(dimension_semantics=sem, vmem_limit_bytes=VMEM_LIMIT_BYTES)


def _pick(n, prefs):
    for p in prefs:
        if p <= n and n % p == 0:
            return p
    return n


def _row_select(row0, tm, n_ctx, ref):
    row = row0 + lax.broadcasted_iota(jnp.int32, (tm, 1), 0)
    return jnp.where(row < n_ctx, ref[0:1, :], ref[1:2, :])


def _silu(v):
    return v * (1.0 / (1.0 + jnp.exp(-v)))


def _ada_kernel(c_ref, wd_ref, wu_ref, b_ref, o_ref):
    cv = c_ref[...]
    h = jnp.dot(_silu(cv), wd_ref[...], preferred_element_type=F32, precision=lax.Precision.HIGHEST)
    o_ref[...] = jnp.dot(h, wu_ref[...], preferred_element_type=F32,
                         precision=lax.Precision.HIGHEST) + b_ref[...]


def ada_modulation(cvecs, w_down, w_up, bias):
    depth, d, r = w_down.shape
    n = w_up.shape[-1]
    tn = _pick(n, (2048, 1024, 512, 256, 128))
    return pl.pallas_call(
        _ada_kernel,
        grid=(depth, n // tn),
        in_specs=[pl.BlockSpec((8, d), lambda l, j: (0, 0)),
                  pl.BlockSpec((None, d, r), lambda l, j: (l, 0, 0)),
                  pl.BlockSpec((None, r, tn), lambda l, j: (l, 0, j)),
                  pl.BlockSpec((None, 1, tn), lambda l, j: (l, 0, j))],
        out_specs=pl.BlockSpec((None, 8, tn), lambda l, j: (l, 0, j)),
        out_shape=jax.ShapeDtypeStruct((depth, 8, n), F32),
        compiler_params=_cparams("parallel", "parallel"),
    )(cvecs, w_down, w_up, bias.reshape(depth, 1, n))


def _norm_mod_kernel(x_ref, g_ref, sh_ref, sc_ref, o_ref, *, tm, n_ctx):
    x = x_ref[...]
    y = x * lax.rsqrt(jnp.mean(x * x, axis=-1, keepdims=True) + NORM_EPS) * g_ref[...]
    row0 = pl.program_id(0) * tm
    sh = _row_select(row0, tm, n_ctx, sh_ref)
    sc = _row_select(row0, tm, n_ctx, sc_ref)
    o_ref[...] = (y * (1.0 + sc) + sh).astype(o_ref.dtype)


def _norm_mod_router_kernel(x_ref, g_ref, sh_ref, sc_ref, rw_ref, o_ref, ow_ref, lt_ref, *, tm, n_ctx):
    x = x_ref[...]
    y = x * lax.rsqrt(jnp.mean(x * x, axis=-1, keepdims=True) + NORM_EPS) * g_ref[...]
    row0 = pl.program_id(0) * tm
    sh = _row_select(row0, tm, n_ctx, sh_ref)
    sc = _row_select(row0, tm, n_ctx, sc_ref)
    t = y * (1.0 + sc) + sh
    o_ref[...] = t.astype(o_ref.dtype)
    _store_word_rows(ow_ref, _pack_words(t))
    lt_ref[...] = lax.dot_general(rw_ref[...], t, (((1,), (1,)), ((), ())),
                                  preferred_element_type=F32, precision=lax.Precision.HIGHEST)


def norm_modulate(x, g, shift2, scale2, n_ctx, router_wt=None):
    t, d = x.shape
    tm = _pick(t, (256, 128))
    vec = pl.BlockSpec((1, d), lambda i: (0, 0))
    two = pl.BlockSpec((2, d), lambda i: (0, 0))
    xs = pl.BlockSpec((tm, d), lambda i: (i, 0))
    if router_wt is None:
        return pl.pallas_call(
            functools.partial(_norm_mod_kernel, tm=tm, n_ctx=n_ctx),
            grid=(t // tm,), in_specs=[xs, vec, two, two], out_specs=xs,
            out_shape=jax.ShapeDtypeStruct((t, d), BF16),
            compiler_params=_cparams("parallel"),
        )(x, g.reshape(1, d), shift2, scale2)
    e = router_wt.shape[0]
    row_words = (d // 2 // LANE, LANE)
    return pl.pallas_call(
        functools.partial(_norm_mod_router_kernel, tm=tm, n_ctx=n_ctx),
        grid=(t // tm,),
        in_specs=[xs, vec, two, two, pl.BlockSpec((e, d), lambda i: (0, 0))],
        out_specs=[xs, pl.BlockSpec((tm,) + row_words, lambda i: (i, 0, 0)), pl.BlockSpec((e, tm), lambda i: (0, i))],
        out_shape=[jax.ShapeDtypeStruct((t, d), BF16), jax.ShapeDtypeStruct((t,) + row_words, jnp.uint32),
                   jax.ShapeDtypeStruct((e, t), F32)],
        compiler_params=_cparams("parallel"),
        name="norm_modulate_router",
    )(x, g.reshape(1, d), shift2, scale2, router_wt)


def _final_norm_kernel(x_ref, g_ref, o_ref):
    x = x_ref[...]
    o_ref[...] = x * lax.rsqrt(jnp.mean(x * x, axis=-1, keepdims=True) + NORM_EPS) * g_ref[...]


def final_norm(x, g, row_start):
    t, d = x.shape
    tm = _pick(t, (256, 128))
    assert row_start % tm == 0
    off = row_start // tm
    return pl.pallas_call(
        _final_norm_kernel, grid=((t - row_start) // tm,),
        in_specs=[pl.BlockSpec((tm, d), lambda i: (i + off, 0)), pl.BlockSpec((1, d), lambda i: (0, 0))],
        out_specs=pl.BlockSpec((tm, d), lambda i: (i, 0)),
        out_shape=jax.ShapeDtypeStruct((t - row_start, d), F32),
        compiler_params=_cparams("parallel"),
    )(x, g.reshape(1, d))


def _mm_kernel(a_ref, b_ref, o_ref):
    o_ref[...] = jnp.dot(a_ref[...], b_ref[...], preferred_element_type=F32).astype(o_ref.dtype)


def _mm_tiles(m, k, n):
    tm = _pick(m, (1280, 1024, 640, 512, 256, 128))
    if k > 4096:
        tm = _pick(m, (640, 512, 256, 128))
    tn = _pick(n, (512, 256, 128))
    return tm, tn


def matmul(a, b, out_dtype):
    m, k = a.shape
    n = b.shape[1]
    tm, tn = _mm_tiles(m, k, n)
    return pl.pallas_call(
        _mm_kernel, grid=(m // tm, n // tn),
        in_specs=[pl.BlockSpec((tm, k), lambda i, j: (i, 0)), pl.BlockSpec((k, tn), lambda i, j: (0, j))],
        out_specs=pl.BlockSpec((tm, tn), lambda i, j: (i, j)),
        out_shape=jax.ShapeDtypeStruct((m, n), out_dtype),
        compiler_params=_cparams("parallel", "parallel"),
    )(a, b)


def _mm_res_kernel(a_ref, b_ref, r_ref, g_ref, o_ref, *, tm, n_ctx):
    acc = jnp.dot(a_ref[...], b_ref[...], preferred_element_type=F32)
    gate = _row_select(pl.program_id(0) * tm, tm, n_ctx, g_ref)
    o_ref[...] = r_ref[...] + gate * acc


def matmul_residual(a, b, resid, gate2, n_ctx):
    m, k = a.shape
    n = b.shape[1]
    tm, tn = _mm_tiles(m, k, n)
    return pl.pallas_call(
        functools.partial(_mm_res_kernel, tm=tm, n_ctx=n_ctx), grid=(m // tm, n // tn),
        in_specs=[pl.BlockSpec((tm, k), lambda i, j: (i, 0)), pl.BlockSpec((k, tn), lambda i, j: (0, j)),
                  pl.BlockSpec((tm, tn), lambda i, j: (i, j)), pl.BlockSpec((2, tn), lambda i, j: (0, j))],
        out_specs=pl.BlockSpec((tm, tn), lambda i, j: (i, j)),
        out_shape=jax.ShapeDtypeStruct((m, n), F32),
        compiler_params=_cparams("parallel", "parallel"),
    )(a, b, resid, gate2)


def _rope_tables(n_ctx, n_lat):
    rows = n_lat // GRID_W
    row = jnp.broadcast_to(jnp.arange(rows, dtype=F32)[:, None], (rows, GRID_W)).reshape(-1)
    col = jnp.broadcast_to(jnp.arange(GRID_W, dtype=F32)[None, :], (rows, GRID_W)).reshape(-1)
    n_freq = QK_ROPE // 4
    inv = ROPE_THETA ** (-jnp.arange(n_freq, dtype=F32) / n_freq)
    ang = jnp.concatenate([row[:, None] * inv, col[:, None] * inv], axis=-1)
    ang = jnp.concatenate([jnp.zeros((n_ctx, QK_ROPE // 2), F32), ang], axis=0)
    return jnp.tile(jnp.cos(ang), (1, 4)), jnp.tile(jnp.sin(ang), (1, 4))


def _rot_cols(w):
    h = QK_ROPE // 2
    return jnp.concatenate([-w[..., h:], w[..., :h]], axis=-1)


def _mla_lat_post_kernel(lat_ref, qg_ref, kg_ref, cc_ref, ss_ref, cq_ref, ckv_ref, kr_ref, *, q_rank, kv_rank):
    cq = lat_ref[:, :q_rank]
    cq_ref[...] = (cq * lax.rsqrt(jnp.mean(cq * cq, axis=-1, keepdims=True) + NORM_EPS)
                   * qg_ref[...]).astype(cq_ref.dtype)
    ckv = lat_ref[:, q_rank:q_rank + kv_rank]
    ckv_ref[...] = (ckv * lax.rsqrt(jnp.mean(ckv * ckv, axis=-1, keepdims=True) + NORM_EPS)
                    * kg_ref[...]).astype(ckv_ref.dtype)
    base = q_rank + kv_rank
    cc = cc_ref[...]
    ss = ss_ref[...]
    lo = lat_ref[:, base:base + LANE] * cc + lat_ref[:, base + LANE:base + 2 * LANE] * ss
    hi = lat_ref[:, base + 2 * LANE:base + 3 * LANE] * cc + lat_ref[:, base + 3 * LANE:base + 4 * LANE] * ss
    kr_ref[:, :LANE] = lo.astype(kr_ref.dtype)
    kr_ref[:, LANE:] = hi.astype(kr_ref.dtype)


def mla_lat_post(lat, q_norm, kv_norm, cc, ss):
    t = lat.shape[0]
    q_rank, kv_rank = q_norm.shape[0], kv_norm.shape[0]
    tm = _pick(t, (256, 128))
    row = lambda w: pl.BlockSpec((tm, w), lambda i: (i, 0))
    return pl.pallas_call(
        functools.partial(_mla_lat_post_kernel, q_rank=q_rank, kv_rank=kv_rank), grid=(t // tm,),
        in_specs=[row(lat.shape[1]), pl.BlockSpec((1, q_rank), lambda i: (0, 0)),
                  pl.BlockSpec((1, kv_rank), lambda i: (0, 0)), row(LANE), row(LANE)],
        out_specs=[row(q_rank), row(kv_rank), row(2 * LANE)],
        out_shape=[jax.ShapeDtypeStruct((t, q_rank), BF16), jax.ShapeDtypeStruct((t, kv_rank), BF16),
                   jax.ShapeDtypeStruct((t, 2 * LANE), BF16)],
        compiler_params=_cparams("parallel"),
    )(lat, q_norm.reshape(1, -1), kv_norm.reshape(1, -1), cc, ss)


def _q_proj_kernel(a_ref, b_ref, cc_ref, ss_ref, qn_ref, qr_ref):
    acc = jnp.dot(a_ref[...], b_ref[...], preferred_element_type=F32)
    qn_ref[...] = acc[:, :2 * LANE].astype(qn_ref.dtype)
    qr_ref[...] = (acc[:, 2 * LANE:3 * LANE] * cc_ref[...]
                   + acc[:, 3 * LANE:4 * LANE] * ss_ref[...]).astype(qr_ref.dtype)


def q_proj(cqn, w_pairs, cc, ss):
    t, k = cqn.shape
    n_pairs = w_pairs.shape[1] // (4 * LANE)
    tm = _pick(t, (1280, 1024, 640, 512, 256, 128))
    return pl.pallas_call(
        _q_proj_kernel, grid=(t // tm, n_pairs),
        in_specs=[pl.BlockSpec((tm, k), lambda i, j: (i, 0)), pl.BlockSpec((k, 4 * LANE), lambda i, j: (0, j)),
                  pl.BlockSpec((tm, LANE), lambda i, j: (i, 0)), pl.BlockSpec((tm, LANE), lambda i, j: (i, 0))],
        out_specs=[pl.BlockSpec((tm, 2 * LANE), lambda i, j: (i, j)), pl.BlockSpec((tm, LANE), lambda i, j: (i, j))],
        out_shape=[jax.ShapeDtypeStruct((t, n_pairs * 2 * LANE), BF16),
                   jax.ShapeDtypeStruct((t, n_pairs * LANE), BF16)],
        compiler_params=_cparams("parallel", "parallel"),
    )(cqn, w_pairs, cc, ss)


def _attn_kernel(qn_ref, qr_ref, kn_ref, kr_ref, vt_ref, o_ref, s_scr, acc_scr, *, tq, tk, unroll, n_ctx, n_all):
    q = jnp.concatenate([qn_ref[...], qr_ref[...]], axis=1)
    qt = q.astype(F32).T.astype(BF16)

    def scores(slot, off, size):
        k = jnp.concatenate([kn_ref[pl.ds(off, size), :], kr_ref[pl.ds(off, size), :]], axis=1)
        s = jnp.dot(k, qt, preferred_element_type=F32)
        s_scr[slot, 0:size, :] = s
        return jnp.max(s, axis=0, keepdims=True)

    def update(slot, mx, off, size, carry):
        m, l = carry
        m_new = jnp.maximum(m, mx)
        alpha = jnp.exp2(m - m_new)
        p = jnp.exp2(s_scr[slot, 0:size, :] - m_new)
        l = alpha * l + jnp.sum(p, axis=0, keepdims=True)
        acc_scr[...] = alpha * acc_scr[...] + jnp.dot(vt_ref[:, pl.ds(off, size)], p.astype(BF16),
                                                      preferred_element_type=F32)
        return m_new, l

    def finish(carry):
        o_ref[...] = (acc_scr[...] / carry[1]).T.astype(o_ref.dtype)

    acc_scr[...] = jnp.zeros(acc_scr.shape, F32)
    init = (jnp.full((1, tq), -jnp.inf, F32), jnp.zeros((1, tq), F32))
    is_ctx = pl.program_id(1) * tq < n_ctx

    @pl.when(is_ctx)
    def _():
        finish(update(0, scores(0, 0, n_ctx), 0, n_ctx, init))

    @pl.when(jnp.logical_not(is_ctx))
    def _():
        n_chunks = n_all // tk

        def run(base, count, carry, lookahead):
            mx, st = carry[0], carry[1:]
            for u in range(count):
                off = base + u * tk
                more = u + 1 < count or lookahead
                mx_next = scores((u + 1) % 2, off + tk, tk) if more else None
                st = update(u % 2, mx, off, tk, st)
                mx = mx_next
            return (mx,) + st

        def body(i, carry):
            return run(pl.multiple_of(i * (unroll * tk), tk), unroll, carry, True)

        n_trips = (n_chunks - 1) // unroll
        carry = lax.fori_loop(0, n_trips, body, (scores(0, 0, tk),) + init)
        finish(run(n_trips * unroll * tk, n_chunks - n_trips * unroll, carry, False)[1:])


def attention(qn, qr, kn, kr2, vt, n_ctx):
    t = qn.shape[0]
    heads = qn.shape[1] // QK_NOPE
    tq = _pick(n_ctx, (256, 128))
    tk = _pick(t, (3328, 1280, 512, 256, 128))
    assert n_ctx % tq == 0 and t % tq == 0 and n_ctx <= tk
    return pl.pallas_call(
        functools.partial(_attn_kernel, tq=tq, tk=tk, unroll=12, n_ctx=n_ctx, n_all=t),
        grid=(heads, t // tq),
        in_specs=[pl.BlockSpec((tq, LANE), lambda h, i: (i, h)),
                  pl.BlockSpec((tq, LANE), lambda h, i: (i, h // 2)),
                  pl.BlockSpec((t, LANE), lambda h, i: (0, h)),
                  pl.BlockSpec((t, LANE), lambda h, i: (0, h % 2)),
                  pl.BlockSpec((V_HEAD, t), lambda h, i: (h, 0))],
        out_specs=pl.BlockSpec((tq, LANE), lambda h, i: (i, h)),
        out_shape=jax.ShapeDtypeStruct((t, heads * V_HEAD), BF16),
        scratch_shapes=[pltpu.VMEM((2, tk, tq), F32), pltpu.VMEM((V_HEAD, tq), F32)],
        compiler_params=_cparams("parallel", "arbitrary"),
        name="mla_attention",
    )(qn, qr, kn, kr2, vt)


def _mm_nt_kernel(a_ref, b_ref, o_ref):
    o_ref[...] = lax.dot_general(a_ref[...], b_ref[...], (((1,), (1,)), ((), ())),
                                 preferred_element_type=F32).astype(o_ref.dtype)


def matmul_nt(a, b, out_dtype):
    m, k = a.shape
    n = b.shape[0]
    tm = _pick(m, (1024, 512, 256, 128))
    tn = _pick(n, (1280, 1024, 640, 512, 256, 128))
    return pl.pallas_call(
        _mm_nt_kernel, grid=(m // tm, n // tn),
        in_specs=[pl.BlockSpec((tm, k), lambda i, j: (i, 0)), pl.BlockSpec((tn, k), lambda i, j: (j, 0))],
        out_specs=pl.BlockSpec((tm, tn), lambda i, j: (i, j)),
        out_shape=jax.ShapeDtypeStruct((m, n), out_dtype),
        compiler_params=_cparams("parallel", "parallel"),
        name="matmul_nt",
    )(a, b)


def mla_layer(x, hn, p, gate2, n_ctx, cc, ss):
    lat = matmul(hn, p["w_in"], F32)
    cqn, ckvn, kr2 = mla_lat_post(lat, p["q_norm"], p["kv_norm"], cc, ss)
    qn, qr = q_proj(cqn, p["w_uq"], cc, ss)
    kn = matmul(ckvn, p["w_uk"], BF16)
    vt = matmul_nt(p["w_uv_t"], ckvn, BF16)
    o = attention(qn, qr, kn, kr2, vt, n_ctx)
    return matmul_residual(o, p["w_o"], x, gate2, n_ctx)


def prep_mla_weights(w_in, q_norm, kv_norm, w_uq, w_ukv, w_o):
    d = w_in.shape[0]
    heads = d // 128
    q_rank, kv_rank = q_norm.shape[0], kv_norm.shape[0]
    z64 = jnp.zeros((d, QK_ROPE), F32)
    wkr = w_in[:, q_rank + kv_rank:]
    wkr_rot = _rot_cols(wkr)
    w_in_p = jnp.concatenate([w_in[:, :q_rank + kv_rank], wkr, z64, wkr_rot, z64, z64, wkr, z64, wkr_rot], axis=1)
    scale = (QK_NOPE + QK_ROPE) ** -0.5 * math.log2(math.e)
    wq = (w_uq * scale).reshape(q_rank, heads, QK_NOPE + QK_ROPE)
    nope = wq[..., :QK_NOPE].reshape(q_rank, heads // 2, 2 * QK_NOPE)
    rope = wq[..., QK_NOPE:]
    rope_a = rope.reshape(q_rank, heads // 2, 2 * QK_ROPE)
    rope_b = _rot_cols(rope).reshape(q_rank, heads // 2, 2 * QK_ROPE)
    w_uq_p = jnp.concatenate([nope, rope_a, rope_b], axis=-1).reshape(q_rank, heads // 2 * 4 * LANE)
    wkv = w_ukv.reshape(kv_rank, heads, QK_NOPE + V_HEAD)
    w_uk = wkv[..., :QK_NOPE].reshape(kv_rank, heads * QK_NOPE)
    w_uv_t = wkv[..., QK_NOPE:].reshape(kv_rank, heads * V_HEAD).T
    return dict(w_in=w_in_p.astype(BF16), q_norm=q_norm, kv_norm=kv_norm, w_uq=w_uq_p.astype(BF16),
                w_uk=w_uk.astype(BF16), w_uv_t=w_uv_t.astype(BF16), w_o=w_o.astype(BF16))


def _conv_silu_kernel(prev_ref, cur_ref, next_ref, w_ref, b_ref, o_ref, win_ref, *, tm, n_ctx, n_tiles):
    i = pl.program_id(0)
    row0 = i * tm
    has_prev = jnp.logical_and(i > 0, row0 != n_ctx)
    has_next = jnp.logical_and(i < n_tiles - 1, row0 + tm != n_ctx)
    h = CONV_HALO
    win_ref[0:h, :] = jnp.where(has_prev, prev_ref[...].astype(F32), 0.0)
    win_ref[h:h + tm, :] = cur_ref[...].astype(F32)
    win_ref[h + tm:2 * h + tm, :] = jnp.where(has_next, next_ref[...].astype(F32), 0.0)
    pad = w_ref.shape[0] // 2
    acc = jnp.zeros(o_ref.shape, F32) + b_ref[...]
    for k in range(w_ref.shape[0]):
        acc = acc + w_ref[k:k + 1, :] * win_ref[pl.ds(h - pad + k, tm), :]
    o_ref[...] = _silu(acc).astype(o_ref.dtype)


CONV_HALO = 16


def conv_silu(u, w, b, n_ctx):
    t, c = u.shape
    tm = _pick(n_ctx, (256, 128))
    tn = _pick(c, (2048, 1024, 512, 256, 128))
    n_tiles = t // tm
    rh = tm // CONV_HALO
    nbh = t // CONV_HALO
    assert w.shape[0] // 2 <= CONV_HALO
    return pl.pallas_call(
        functools.partial(_conv_silu_kernel, tm=tm, n_ctx=n_ctx, n_tiles=n_tiles),
        grid=(n_tiles, c // tn),
        in_specs=[pl.BlockSpec((CONV_HALO, tn), lambda i, j: (jnp.maximum(i * rh - 1, 0), j)),
                  pl.BlockSpec((tm, tn), lambda i, j: (i, j)),
                  pl.BlockSpec((CONV_HALO, tn), lambda i, j: (jnp.minimum((i + 1) * rh, nbh - 1), j)),
                  pl.BlockSpec((w.shape[0], tn), lambda i, j: (0, j)),
                  pl.BlockSpec((1, tn), lambda i, j: (0, j))],
        out_specs=pl.BlockSpec((tm, tn), lambda i, j: (i, j)),
        out_shape=jax.ShapeDtypeStruct((t, c), BF16),
        scratch_shapes=[pltpu.VMEM((tm + 2 * CONV_HALO, tn), F32)],
        compiler_params=_cparams("parallel", "parallel"),
        name="conv_silu",
    )(u, u, u, w, b.reshape(1, c))


def _dt_prep_kernel(raw_ref, bias_ref, a_ref, dtf_ref, cf_ref, dtb_ref, cb_ref, *, nh, q):
    raw = raw_ref[...]
    r = lax.broadcasted_iota(jnp.int32, (q, q), 0)
    c = lax.broadcasted_iota(jnp.int32, (q, q), 1)
    for d, (dt_ref, c_ref) in enumerate(((dtf_ref, cf_ref), (dtb_ref, cb_ref))):
        v = raw[:, d * nh:(d + 1) * nh] + bias_ref[d:d + 1, :]
        dt = jnp.maximum(v, 0.0) + jnp.log1p(jnp.exp(-jnp.abs(v)))
        da = dt * a_ref[d:d + 1, :]
        tri = jnp.where(r >= c, 1.0, 0.0) if d == 0 else jnp.where(r <= c, 1.0, 0.0)
        cum = jnp.dot(tri.astype(F32), da, preferred_element_type=F32, precision=lax.Precision.HIGHEST)
        dt_ref[...] = dt.T
        c_ref[...] = cum.T


def dt_prep(dt_raw, dt_bias, a_log):
    t = dt_raw.shape[0]
    nh = dt_raw.shape[1] // 2
    q = SSM_CHUNK
    a = -jnp.exp(a_log.astype(F32))
    out = jax.ShapeDtypeStruct((nh, t), F32)
    ob = pl.BlockSpec((nh, q), lambda i: (0, i))
    return pl.pallas_call(
        functools.partial(_dt_prep_kernel, nh=nh, q=q), grid=(t // q,),
        in_specs=[pl.BlockSpec((q, 2 * nh), lambda i: (i, 0)), pl.BlockSpec((2, nh), lambda i: (0, 0)),
                  pl.BlockSpec((2, nh), lambda i: (0, 0))],
        out_specs=[ob, ob, ob, ob], out_shape=[out, out, out, out],
        compiler_params=_cparams("parallel"),
    )(dt_raw, dt_bias.astype(F32), a)


def _ssd_kernel(x_ref, b_ref, c_ref, dt_ref, cum_ref, y_ref, s_ref, *, hg, reverse, n_sub):
    q = SSM_CHUNK
    n_pairs = hg // 2

    @pl.when(pl.program_id(1) == 0)
    def _():
        s_ref[...] = jnp.zeros(s_ref.shape, s_ref.dtype)

    ii = lax.broadcasted_iota(jnp.int32, (q, q), 0)
    jj = lax.broadcasted_iota(jnp.int32, (q, q), 1)
    mask = (jj >= ii) if reverse else (ii >= jj)
    lane_lo = lax.broadcasted_iota(jnp.int32, (q, 2 * SSM_HEAD_DIM), 1) < SSM_HEAD_DIM
    last = 0 if reverse else q - 1

    for sub in (range(n_sub - 1, -1, -1) if reverse else range(n_sub)):
        r0 = sub * q
        bsub = b_ref[r0:r0 + q, :]
        csub = c_ref[r0:r0 + q, :]
        cmat = csub.astype(F32)
        cb = lax.dot_general(csub, bsub, (((1,), (1,)), ((), ())), preferred_element_type=F32)
        bt = bsub.astype(F32).T
        for pr in range(n_pairs):
            xp = x_ref[r0:r0 + q, pr * LANE:(pr + 1) * LANE]
            sp = s_ref[:, pr * LANE:(pr + 1) * LANE]
            rhs = jnp.concatenate([xp, sp.astype(BF16)], axis=0)
            ys, ss, decs = [], [], []
            for hh in range(2):
                h = pr * 2 + hh
                cum_h = cum_ref[h:h + 1, r0:r0 + q]
                crow = jnp.broadcast_to(cum_h, (q, q))
                ccol = crow.T
                dtrow = dt_ref[h:h + 1, r0:r0 + q]
                seg = jnp.where(mask, ccol - crow, -jnp.inf)
                mm = cb * jnp.exp(seg) * dtrow
                coff = cmat * jnp.exp(ccol)
                lhs = jnp.concatenate([mm.astype(BF16), coff.astype(BF16)], axis=1)
                ys.append(jnp.dot(lhs, rhs, preferred_element_type=F32))
                tot = cum_h[:, last:last + 1]
                wrow = jnp.exp(tot - cum_h) * dtrow
                ss.append(jnp.dot((bt * wrow).astype(BF16), xp, preferred_element_type=F32))
                decs.append(jnp.exp(tot))
            y_ref[r0:r0 + q, pr * LANE:(pr + 1) * LANE] = jnp.where(lane_lo, ys[0], ys[1]).astype(y_ref.dtype)
            dec = jnp.where(lane_lo[0:1, :], decs[0], decs[1])
            s_ref[:, pr * LANE:(pr + 1) * LANE] = sp * dec + jnp.where(lane_lo, ss[0], ss[1])


def ssd_scan(xbc, dt_t, cum_t, d_inner, n_ctx, reverse):
    t = xbc.shape[0]
    q = SSM_CHUNK
    nh = d_inner // SSM_HEAD_DIM
    hg = nh // SSM_GROUPS
    gw = hg * SSM_HEAD_DIM
    assert gw % LANE == 0 and hg % 8 == 0
    if reverse:
        n_sub = 2 if (t // q) % 2 == 0 and (n_ctx // q) % 2 == 0 else 1
    else:
        n_sub = _pick(t // q, (5, 2, 1))
    rows = n_sub * q
    nc = t // rows
    ncc = n_ctx // rows
    xoff = d_inner // LANE
    if reverse:
        cidx = lambda s: jnp.where(s < ncc, ncc - 1 - s, nc - 1 + ncc - s)
    else:
        cidx = lambda s: s
    return pl.pallas_call(
        functools.partial(_ssd_kernel, hg=hg, reverse=reverse, n_sub=n_sub),
        grid=(SSM_GROUPS, nc),
        in_specs=[pl.BlockSpec((rows, gw), lambda g, s: (cidx(s), g)),
                  pl.BlockSpec((rows, D_STATE), lambda g, s: (cidx(s), xoff + g)),
                  pl.BlockSpec((rows, D_STATE), lambda g, s: (cidx(s), xoff + SSM_GROUPS + g)),
                  pl.BlockSpec((hg, rows), lambda g, s: (g, cidx(s))),
                  pl.BlockSpec((hg, rows), lambda g, s: (g, cidx(s)))],
        out_specs=pl.BlockSpec((rows, gw), lambda g, s: (cidx(s), g)),
        out_shape=jax.ShapeDtypeStruct((t, d_inner), BF16),
        scratch_shapes=[pltpu.VMEM((D_STATE, gw), F32)],
        compiler_params=_cparams("parallel", "arbitrary"),
        name="ssd_scan",
    )(xbc, xbc, xbc, dt_t, cum_t)


def _ssm_finish_kernel(yf_ref, yb_ref, x_ref, z_ref, d_ref, g_ref, o_ref, *, n_groups):
    z = z_ref[...].astype(F32)
    y = (yf_ref[...].astype(F32) + yb_ref[...].astype(F32) + d_ref[...] * x_ref[...].astype(F32)) * _silu(z)
    gw = y.shape[1] // n_groups
    for g in range(n_groups):
        yg = y[:, g * gw:(g + 1) * gw]
        yn = yg * lax.rsqrt(jnp.mean(yg * yg, axis=-1, keepdims=True) + NORM_EPS)
        o_ref[:, g * gw:(g + 1) * gw] = (yn * g_ref[:, g * gw:(g + 1) * gw]).astype(o_ref.dtype)


def ssm_finish(yf, yb, xbc, z, d_exp, norm_g):
    t, di = yf.shape
    tm = _pick(t, (256, 128))
    row = pl.BlockSpec((tm, di), lambda i: (i, 0))
    vec = pl.BlockSpec((1, di), lambda i: (0, 0))
    return pl.pallas_call(
        functools.partial(_ssm_finish_kernel, n_groups=SSM_GROUPS), grid=(t // tm,),
        in_specs=[row, row, row, row, vec, vec], out_specs=row,
        out_shape=jax.ShapeDtypeStruct((t, di), BF16),
        compiler_params=_cparams("parallel"),
    )(yf, yb, xbc, z, d_exp, norm_g.reshape(1, di))


def mamba_layer(x, hn, p, gate2, n_ctx):
    di = p["d_inner"]
    z = matmul(hn, p["w_z"], BF16)
    xbc_raw = matmul(hn, p["w_xbc"], BF16)
    dt_raw = matmul(hn, p["w_dt"], F32)
    xbc = conv_silu(xbc_raw, p["conv_w"], p["conv_b"], n_ctx)
    dtf, cf, dtb, cb = dt_prep(dt_raw, p["dt_bias"], p["a_log"])
    yf = ssd_scan(xbc, dtf, cf, di, n_ctx, reverse=False)
    yb = ssd_scan(xbc, dtb, cb, di, n_ctx, reverse=True)
    yn = ssm_finish(yf, yb, xbc, z, p["d_exp"], p["norm_g"])
    return matmul_residual(yn, p["w_out"], x, gate2, n_ctx)


def prep_mamba_weights(w_in, conv_w, conv_b, dt_bias, a_log, d_skip, norm_g, w_out):
    di = norm_g.shape[0]
    conv_dim = conv_w.shape[1]
    nh = d_skip.shape[0]
    w_dt = w_in[:, di + conv_dim:]
    pad = (-w_dt.shape[1]) % LANE
    if pad:
        w_dt = jnp.concatenate([w_dt, jnp.zeros((w_in.shape[0], pad), F32)], axis=1)
    return dict(d_inner=di, w_z=w_in[:, :di].astype(BF16), w_xbc=w_in[:, di:di + conv_dim].astype(BF16),
                w_dt=w_dt.astype(BF16), conv_w=conv_w, conv_b=conv_b, dt_bias=dt_bias, a_log=a_log,
                d_exp=jnp.repeat(d_skip.astype(F32), SSM_HEAD_DIM).reshape(1, di), norm_g=norm_g,
                w_out=w_out.astype(BF16), n_heads=nh)


def _gating_kernel(lt_ref, b_ref, g_ref, grp_ref, *, n_experts):
    epg = n_experts // N_GROUPS
    s = [1.0 / (1.0 + jnp.exp(-lt_ref[e])) for e in range(n_experts)]
    sel = [s[e] + b_ref[e] for e in range(n_experts)]
    best_val, best_idx = None, None
    for g in range(N_GROUPS):
        mem = sel[g * epg:(g + 1) * epg]
        gs = None
        for a in range(epg):
            for b in range(a + 1, epg):
                pair = mem[a] + mem[b]
                gs = pair if gs is None else jnp.maximum(gs, pair)
        if g == 0:
            best_val, best_idx = gs, jnp.zeros(gs.shape, jnp.int32)
        else:
            better = gs > best_val
            best_val = jnp.where(better, gs, best_val)
            best_idx = jnp.where(better, g, best_idx)
    chosen, denom = [], None
    for e in range(n_experts):
        g = e // epg
        rank = jnp.zeros(best_idx.shape, jnp.int32)
        for m in range(g * epg, (g + 1) * epg):
            if m == e:
                continue
            ahead = (sel[m] > sel[e]) if m > e else (sel[m] >= sel[e])
            rank = rank + ahead.astype(jnp.int32)
        pick = jnp.logical_and(best_idx == g, rank < TOP_K)
        w = jnp.where(pick, s[e], 0.0)
        chosen.append(w)
        denom = w if denom is None else denom + w
    inv = 1.0 / denom
    for e in range(n_experts):
        g_ref[e] = chosen[e] * inv
    grp_ref[...] = best_idx


def gating(logits_t, router_b):
    e, t = logits_t.shape
    r = t // LANE
    lt3 = logits_t.reshape(e, r, LANE)
    gates, grp = pl.pallas_call(
        functools.partial(_gating_kernel, n_experts=e), grid=(1,),
        in_specs=[pl.BlockSpec((e, r, LANE), lambda i: (0, 0, 0)),
                  pl.BlockSpec(memory_space=pltpu.SMEM)],
        out_specs=[pl.BlockSpec((e, r, LANE), lambda i: (0, 0, 0)), pl.BlockSpec((r, LANE), lambda i: (0, 0))],
        out_shape=[jax.ShapeDtypeStruct((e, r, LANE), F32), jax.ShapeDtypeStruct((r, LANE), jnp.int32)],
        compiler_params=_cparams("arbitrary"),
        name="moe_gating",
    )(lt3, router_b.astype(F32))
    return gates.reshape(e, t), grp.reshape(t)


HI_HALF = 0xFFFF0000
GATHER_UNROLL = 8


def _pack_words(v):
    w = v.shape[1] // 2
    lo = pltpu.bitcast(v[:, :w].astype(jnp.bfloat16).astype(F32), jnp.uint32) >> 16
    hi = pltpu.bitcast(v[:, w:].astype(jnp.bfloat16).astype(F32), jnp.uint32) & jnp.uint32(HI_HALF)
    return lo | hi


def _unpack_words(words):
    return (pltpu.bitcast(words << 16, F32), pltpu.bitcast(words & jnp.uint32(HI_HALF), F32))


def _store_word_rows(ref3, words):
    for k in range(ref3.shape[1]):
        ref3[:, k, :] = words[:, k * LANE:(k + 1) * LANE]


def _load_word_rows(ref3):
    return jnp.concatenate([ref3[:, k, :] for k in range(ref3.shape[1])], axis=1)


def _row_gather_kernel(idx_ref, src_hbm, o_ref, sem, *, blk):
    def row_copy(q, src_row):
        return pltpu.make_async_copy(src_hbm.at[src_row], o_ref.at[q], sem)

    def issue(q, carry):
        row_copy(q, idx_ref[0, 0, q]).start()
        return carry

    def drain(q, carry):
        row_copy(q, 0).wait()
        return carry

    lax.fori_loop(0, blk, issue, 0, unroll=GATHER_UNROLL)
    lax.fori_loop(0, blk, drain, 0, unroll=GATHER_UNROLL)


def row_gather(src3, idx, blk):
    m = idx.shape[0]
    row_words = src3.shape[1:]
    assert m % blk == 0
    return pl.pallas_call(
        functools.partial(_row_gather_kernel, blk=blk), grid=(m // blk,),
        in_specs=[pl.BlockSpec((1, 1, blk), lambda b: (b, 0, 0), memory_space=pltpu.SMEM),
                  pl.BlockSpec(memory_space=pl.ANY)],
        out_specs=pl.BlockSpec((blk,) + row_words, lambda b: (b, 0, 0)),
        out_shape=jax.ShapeDtypeStruct((m,) + row_words, jnp.uint32),
        scratch_shapes=[pltpu.SemaphoreType.DMA(())],
        compiler_params=_cparams("arbitrary"),
        name="row_gather",
    )(idx.reshape(m // blk, 1, blk), src3)


def _group_ffn_kernel(tg_ref, tv_ref, x_ref, wg_ref, wu_ref, wd_ref, gs_ref, o_ref, acc_ref, xb_ref):
    r = pl.program_id(0)
    j = pl.program_id(1)
    f = pl.program_id(2)
    first = jnp.logical_and(j == 0, f == 0)
    last = jnp.logical_and(j == pl.num_programs(1) - 1, f == pl.num_programs(2) - 1)

    @pl.when(first)
    def _():
        acc_ref[...] = jnp.zeros(acc_ref.shape, F32)
        lo, hi = _unpack_words(_load_word_rows(x_ref))
        w = lo.shape[1]
        xb_ref[:, :w] = lo.astype(BF16)
        xb_ref[:, w:] = hi.astype(BF16)

    @pl.when(tv_ref[r] > 0)
    def _():
        x = xb_ref[...]
        hgate = jnp.dot(x, wg_ref[...], preferred_element_type=F32)
        hup = jnp.dot(x, wu_ref[...], preferred_element_type=F32)
        gts = gs_ref[...]
        lane = lax.broadcasted_iota(jnp.int32, gts.shape, 1)
        gcol = jnp.sum(jnp.where(lane == j, gts, 0.0), axis=1, keepdims=True)
        act = (_silu(hgate) * hup * gcol).astype(BF16)
        d = acc_ref.shape[1]
        tc = min(d, 1024)
        for c0 in range(0, d, tc):
            acc_ref[:, c0:c0 + tc] += jnp.dot(act, wd_ref[:, c0:c0 + tc], preferred_element_type=F32)

    @pl.when(last)
    def _():
        _store_word_rows(o_ref, _pack_words(acc_ref[...]))


def group_ffn(xs3, w_gate, w_up, w_down, gates_sorted, tile_group, tile_valid, tm):
    n_rows = xs3.shape[0]
    row_words = xs3.shape[1:]
    n_e, d, df = w_gate.shape
    epg = n_e // N_GROUPS
    tf = _pick(df, (256, 128))
    words = lambda: pl.BlockSpec((tm,) + row_words, lambda r, j, f, tg, tv: (r, 0, 0))
    grid_spec = pltpu.PrefetchScalarGridSpec(
        num_scalar_prefetch=2, grid=(n_rows // tm, epg, df // tf),
        in_specs=[words(),
                  pl.BlockSpec((None, d, tf), lambda r, j, f, tg, tv: (tg[r] * epg + j, 0, f)),
                  pl.BlockSpec((None, d, tf), lambda r, j, f, tg, tv: (tg[r] * epg + j, 0, f)),
                  pl.BlockSpec((None, tf, d), lambda r, j, f, tg, tv: (tg[r] * epg + j, f, 0)),
                  pl.BlockSpec((tm, epg), lambda r, j, f, tg, tv: (r, 0))],
        out_specs=words(),
        scratch_shapes=[pltpu.VMEM((tm, d), F32), pltpu.VMEM((tm, d), BF16)])
    return pl.pallas_call(
        _group_ffn_kernel, grid_spec=grid_spec,
        out_shape=jax.ShapeDtypeStruct((n_rows,) + row_words, jnp.uint32),
        compiler_params=_cparams("parallel", "arbitrary", "arbitrary"),
        name="group_ffn",
    )(tile_group, tile_valid, xs3, w_gate, w_up, w_down, gates_sorted)


def _shared_ffn_kernel(x_ref, wg_ref, wu_ref, wd_ref, y_ref, res_ref, g2_ref, o_ref, *, tm, n_ctx):
    g2 = _row_select(pl.program_id(0) * tm, tm, n_ctx, g2_ref)

    @pl.when(pl.program_id(1) == 0)
    def _():
        y = jnp.concatenate(_unpack_words(_load_word_rows(y_ref)), axis=1)
        o_ref[...] = res_ref[...] + g2 * y

    x = x_ref[...]
    act = (_silu(jnp.dot(x, wg_ref[...], preferred_element_type=F32))
           * jnp.dot(x, wu_ref[...], preferred_element_type=F32)).astype(BF16)
    o_ref[...] += g2 * jnp.dot(act, wd_ref[...], preferred_element_type=F32)


def shared_ffn_combine(hn, w_gate, w_up, w_down, y_routed, resid, gate2, n_ctx):
    t, d = hn.shape
    df = w_gate.shape[1]
    tm = _pick(t, (320, 256, 128))
    tf = _pick(df, (256, 128))
    row = lambda: pl.BlockSpec((tm, d), lambda i, f: (i, 0))
    return pl.pallas_call(
        functools.partial(_shared_ffn_kernel, tm=tm, n_ctx=n_ctx), grid=(t // tm, df // tf),
        in_specs=[row(), pl.BlockSpec((d, tf), lambda i, f: (0, f)), pl.BlockSpec((d, tf), lambda i, f: (0, f)),
                  pl.BlockSpec((tf, d), lambda i, f: (f, 0)),
                  pl.BlockSpec((tm,) + y_routed.shape[1:], lambda i, f: (i, 0, 0)), row(),
                  pl.BlockSpec((2, d), lambda i, f: (0, 0))],
        out_specs=row(),
        out_shape=jax.ShapeDtypeStruct((t, d), F32),
        compiler_params=_cparams("parallel", "arbitrary"),
        name="shared_ffn_combine",
    )(hn, w_gate, w_up, w_down, y_routed, resid, gate2)


def _group_sort_plan(grp, tm):
    t = grp.shape[0]
    n_tiles = (t + N_GROUPS * (tm - 1)) // tm
    n_rows = n_tiles * tm
    onehot = (grp[:, None] == jnp.arange(N_GROUPS, dtype=jnp.int32)[None, :]).astype(jnp.int32)
    rank = jnp.sum((jnp.cumsum(onehot, axis=0) - onehot) * onehot, axis=1)
    counts = jnp.sum(onehot, axis=0)
    padded = (counts + tm - 1) // tm * tm
    ends = jnp.cumsum(padded)
    dest = (ends - padded)[grp] + rank
    src = jnp.zeros((n_rows,), jnp.int32).at[dest].set(jnp.arange(t, dtype=jnp.int32))
    row_valid = jnp.zeros((n_rows,), jnp.int32).at[dest].set(1)
    tile_start = jnp.arange(n_tiles, dtype=jnp.int32) * tm
    tile_group = jnp.minimum(jnp.sum((tile_start[:, None] >= ends[None, :]).astype(jnp.int32), axis=1),
                             N_GROUPS - 1)
    tile_valid = (tile_start < ends[-1]).astype(jnp.int32)
    return dest, src, row_valid, tile_group, tile_valid


def moe_layer(x, hn, hn_words, logits_t, router_b, p, gate2, n_ctx):
    t = hn.shape[0]
    n_e = logits_t.shape[0]
    epg = n_e // N_GROUPS
    tm = _pick(t, (640, 512, 256, 128))
    gates_t, grp = gating(logits_t, router_b)
    dest, src, row_valid, tile_group, tile_valid = _group_sort_plan(grp, tm)
    gate_cols = grp[:, None] * epg + jnp.arange(epg, dtype=jnp.int32)[None, :]
    gates_tok = jnp.take_along_axis(gates_t.T, gate_cols, axis=1)
    gates_sorted = jnp.where(row_valid[:, None] > 0, gates_tok[src], 0.0)
    xs = row_gather(hn_words, src, tm)
    ys = group_ffn(xs, p["w_gate"], p["w_up"], p["w_down"], gates_sorted, tile_group, tile_valid, tm)
    y_tok = row_gather(ys, dest, tm)
    return shared_ffn_combine(hn, p["ws_gate"], p["ws_up"], p["ws_down"], y_tok, x, gate2, n_ctx)


def kernel(x, c, ctx, c_ctx, ada_down, ada_up, ada_bias, norm1_g, norm2_g, final_norm_g, mla_w_in, mla_q_norm, mla_kv_norm, mla_w_uq, mla_w_ukv, mla_w_o, ssm_w_in, ssm_conv_w, ssm_conv_b, ssm_dt_bias, ssm_a_log, ssm_d, ssm_norm_g, ssm_w_out, router_w, router_b, moe_w_gate, moe_w_up, moe_w_down, shared_w_gate, shared_w_up, shared_w_down):
    assert x.shape[0] == 1 and ctx.shape[0] == 1
    depth = ada_down.shape[0]
    n_lat, d = x.shape[1], x.shape[2]
    n_ctx = ctx.shape[1]
    xs = jnp.concatenate([ctx[0], x[0]], axis=0).astype(F32)

    cvecs = jnp.zeros((8, d), F32).at[0].set(c_ctx).at[1].set(c[0])
    mods = ada_modulation(cvecs, ada_down, ada_up, ada_bias)[:, :2]
    mods = mods.reshape(depth, 2, N_MOD, d)
    cc, ss = _rope_tables(n_ctx, n_lat)
    router_wt = router_w.T

    for i in range(depth):
        sh1, sc1, g1, sh2, sc2, g2 = (mods[i, :, k] for k in range(N_MOD))
        j = i // 2
        hn = norm_modulate(xs, norm1_g[i], sh1, sc1, n_ctx)
        if i % 2 == 0:
            p = prep_mla_weights(mla_w_in[j], mla_q_norm[j], mla_kv_norm[j], mla_w_uq[j], mla_w_ukv[j], mla_w_o[j])
            xs = mla_layer(xs, hn, p, g1, n_ctx, cc, ss)
        else:
            p = prep_mamba_weights(ssm_w_in[j], ssm_conv_w[j], ssm_conv_b[j], ssm_dt_bias[j], ssm_a_log[j],
                                   ssm_d[j], ssm_norm_g[j], ssm_w_out[j])
            xs = mamba_layer(xs, hn, p, g1, n_ctx)
        hn2, hn2_words, logits_t = norm_modulate(xs, norm2_g[i], sh2, sc2, n_ctx, router_wt=router_wt)
        pm = dict(w_gate=moe_w_gate[i].astype(BF16), w_up=moe_w_up[i].astype(BF16),
                  w_down=moe_w_down[i].astype(BF16), ws_gate=shared_w_gate[i].astype(BF16),
                  ws_up=shared_w_up[i].astype(BF16), ws_down=shared_w_down[i].astype(BF16))
        xs = moe_layer(xs, hn2, hn2_words, logits_t, router_b, pm, g2, n_ctx)
    return final_norm(xs, final_norm_g, n_ctx)[None]
```

```python
import functools
import math

import jax
import jax.numpy as jnp
from jax import lax
from jax.experimental import pallas as pl
from jax.experimental.pallas import tpu as pltpu

F32 = jnp.float32
BF16 = jnp.bfloat16

GRID_W = 64
ROPE_THETA = 10000.0
NORM_EPS = 1e-6
N_MOD = 6
QK_NOPE = 128
QK_ROPE = 64
V_HEAD = 128
SSM_HEAD_DIM = 64
SSM_GROUPS = 8
D_STATE = 128
SSM_CHUNK = 128
N_GROUPS = 4
TOP_K = 2

LANE = 128
VMEM_LIMIT_BYTES = 56 * 1024 * 1024


def _cparams(*sem):
    return pltpu.CompilerParams(dimension_semantics=sem, vmem_limit_bytes=VMEM_LIMIT_BYTES)


def _pick(n, prefs):
    for p in prefs:
        if p <= n and n % p == 0:
            return p
    return n


def _row_select(row0, tm, n_ctx, ref):
    row = row0 + lax.broadcasted_iota(jnp.int32, (tm, 1), 0)
    return jnp.where(row < n_ctx, ref[0:1, :], ref[1:2, :])


def _silu(v):
    return v * (1.0 / (1.0 + jnp.exp(-v)))


def _ada_kernel(c_ref, wd_ref, wu_ref, b_ref, o_ref):
    cv = c_ref[...]
    h = jnp.dot(_silu(cv), wd_ref[...], preferred_element_type=F32, precision=lax.Precision.HIGHEST)
    o_ref[...] = jnp.dot(h, wu_ref[...], preferred_element_type=F32,
                         precision=lax.Precision.HIGHEST) + b_ref[...]


def ada_modulation(cvecs, w_down, w_up, bias):
    depth, d, r = w_down.shape
    n = w_up.shape[-1]
    tn = _pick(n, (2048, 1024, 512, 256, 128))
    return pl.pallas_call(
        _ada_kernel,
        grid=(depth, n // tn),
        in_specs=[pl.BlockSpec((8, d), lambda l, j: (0, 0)),
                  pl.BlockSpec((None, d, r), lambda l, j: (l, 0, 0)),
                  pl.BlockSpec((None, r, tn), lambda l, j: (l, 0, j)),
                  pl.BlockSpec((None, 1, tn), lambda l, j: (l, 0, j))],
        out_specs=pl.BlockSpec((None, 8, tn), lambda l, j: (l, 0, j)),
        out_shape=jax.ShapeDtypeStruct((depth, 8, n), F32),
        compiler_params=_cparams("parallel", "parallel"),
    )(cvecs, w_down, w_up, bias.reshape(depth, 1, n))


def _norm_mod_kernel(x_ref, g_ref, sh_ref, sc_ref, o_ref, *, tm, n_ctx):
    x = x_ref[...]
    y = x * lax.rsqrt(jnp.mean(x * x, axis=-1, keepdims=True) + NORM_EPS) * g_ref[...]
    row0 = pl.program_id(0) * tm
    sh = _row_select(row0, tm, n_ctx, sh_ref)
    sc = _row_select(row0, tm, n_ctx, sc_ref)
    o_ref[...] = (y * (1.0 + sc) + sh).astype(o_ref.dtype)


def _norm_mod_router_kernel(x_ref, g_ref, sh_ref, sc_ref, rw_ref, o_ref, ow_ref, lt_ref, *, tm, n_ctx):
    x = x_ref[...]
    y = x * lax.rsqrt(jnp.mean(x * x, axis=-1, keepdims=True) + NORM_EPS) * g_ref[...]
    row0 = pl.program_id(0) * tm
    sh = _row_select(row0, tm, n_ctx, sh_ref)
    sc = _row_select(row0, tm, n_ctx, sc_ref)
    t = y * (1.0 + sc) + sh
    o_ref[...] = t.astype(o_ref.dtype)
    _store_word_rows(ow_ref, _pack_words(t))
    lt_ref[...] = lax.dot_general(rw_ref[...], t, (((1,), (1,)), ((), ())),
                                  preferred_element_type=F32, precision=lax.Precision.HIGHEST)


def norm_modulate(x, g, shift2, scale2, n_ctx, router_wt=None):
    t, d = x.shape
    tm = _pick(t, (256, 128))
    vec = pl.BlockSpec((1, d), lambda i: (0, 0))
    two = pl.BlockSpec((2, d), lambda i: (0, 0))
    xs = pl.BlockSpec((tm, d), lambda i: (i, 0))
    if router_wt is None:
        return pl.pallas_call(
            functools.partial(_norm_mod_kernel, tm=tm, n_ctx=n_ctx),
            grid=(t // tm,), in_specs=[xs, vec, two, two], out_specs=xs,
            out_shape=jax.ShapeDtypeStruct((t, d), BF16),
            compiler_params=_cparams("parallel"),
        )(x, g.reshape(1, d), shift2, scale2)
    e = router_wt.shape[0]
    row_words = (d // 2 // LANE, LANE)
    return pl.pallas_call(
        functools.partial(_norm_mod_router_kernel, tm=tm, n_ctx=n_ctx),
        grid=(t // tm,),
        in_specs=[xs, vec, two, two, pl.BlockSpec((e, d), lambda i: (0, 0))],
        out_specs=[xs, pl.BlockSpec((tm,) + row_words, lambda i: (i, 0, 0)), pl.BlockSpec((e, tm), lambda i: (0, i))],
        out_shape=[jax.ShapeDtypeStruct((t, d), BF16), jax.ShapeDtypeStruct((t,) + row_words, jnp.uint32),
                   jax.ShapeDtypeStruct((e, t), F32)],
        compiler_params=_cparams("parallel"),
        name="norm_modulate_router",
    )(x, g.reshape(1, d), shift2, scale2, router_wt)


def _final_norm_kernel(x_ref, g_ref, o_ref):
    x = x_ref[...]
    o_ref[...] = x * lax.rsqrt(jnp.mean(x * x, axis=-1, keepdims=True) + NORM_EPS) * g_ref[...]


def final_norm(x, g, row_start):
    t, d = x.shape
    tm = _pick(t, (256, 128))
    assert row_start % tm == 0
    off = row_start // tm
    return pl.pallas_call(
        _final_norm_kernel, grid=((t - row_start) // tm,),
        in_specs=[pl.BlockSpec((tm, d), lambda i: (i + off, 0)), pl.BlockSpec((1, d), lambda i: (0, 0))],
        out_specs=pl.BlockSpec((tm, d), lambda i: (i, 0)),
        out_shape=jax.ShapeDtypeStruct((t - row_start, d), F32),
        compiler_params=_cparams("parallel"),
    )(x, g.reshape(1, d))


def _mm_kernel(a_ref, b_ref, o_ref):
    o_ref[...] = jnp.dot(a_ref[...], b_ref[...], preferred_element_type=F32).astype(o_ref.dtype)


def _mm_tiles(m, k, n):
    tm = _pick(m, (1280, 1024, 640, 512, 256, 128))
    if k > 4096:
        tm = _pick(m, (640, 512, 256, 128))
    tn = _pick(n, (512, 256, 128))
    return tm, tn


def matmul(a, b, out_dtype):
    m, k = a.shape
    n = b.shape[1]
    tm, tn = _mm_tiles(m, k, n)
    return pl.pallas_call(
        _mm_kernel, grid=(m // tm, n // tn),
        in_specs=[pl.BlockSpec((tm, k), lambda i, j: (i, 0)), pl.BlockSpec((k, tn), lambda i, j: (0, j))],
        out_specs=pl.BlockSpec((tm, tn), lambda i, j: (i, j)),
        out_shape=jax.ShapeDtypeStruct((m, n), out_dtype),
        compiler_params=_cparams("parallel", "parallel"),
    )(a, b)


def _mm_res_kernel(a_ref, b_ref, r_ref, g_ref, o_ref, *, tm, n_ctx):
    acc = jnp.dot(a_ref[...], b_ref[...], preferred_element_type=F32)
    gate = _row_select(pl.program_id(0) * tm, tm, n_ctx, g_ref)
    o_ref[...] = r_ref[...] + gate * acc


def matmul_residual(a, b, resid, gate2, n_ctx):
    m, k = a.shape
    n = b.shape[1]
    tm, tn = _mm_tiles(m, k, n)
    return pl.pallas_call(
        functools.partial(_mm_res_kernel, tm=tm, n_ctx=n_ctx), grid=(m // tm, n // tn),
        in_specs=[pl.BlockSpec((tm, k), lambda i, j: (i, 0)), pl.BlockSpec((k, tn), lambda i, j: (0, j)),
                  pl.BlockSpec((tm, tn), lambda i, j: (i, j)), pl.BlockSpec((2, tn), lambda i, j: (0, j))],
        out_specs=pl.BlockSpec((tm, tn), lambda i, j: (i, j)),
        out_shape=jax.ShapeDtypeStruct((m, n), F32),
        compiler_params=_cparams("parallel", "parallel"),
    )(a, b, resid, gate2)


def _rope_tables(n_ctx, n_lat):
    rows = n_lat // GRID_W
    row = jnp.broadcast_to(jnp.arange(rows, dtype=F32)[:, None], (rows, GRID_W)).reshape(-1)
    col = jnp.broadcast_to(jnp.arange(GRID_W, dtype=F32)[None, :], (rows, GRID_W)).reshape(-1)
    n_freq = QK_ROPE // 4
    inv = ROPE_THETA ** (-jnp.arange(n_freq, dtype=F32) / n_freq)
    ang = jnp.concatenate([row[:, None] * inv, col[:, None] * inv], axis=-1)
    ang = jnp.concatenate([jnp.zeros((n_ctx, QK_ROPE // 2), F32), ang], axis=0)
    return jnp.tile(jnp.cos(ang), (1, 4)), jnp.tile(jnp.sin(ang), (1, 4))


def _rot_cols(w):
    h = QK_ROPE // 2
    return jnp.concatenate([-w[..., h:], w[..., :h]], axis=-1)


def _mla_lat_post_kernel(lat_ref, qg_ref, kg_ref, cc_ref, ss_ref, cq_ref, ckv_ref, kr_ref, *, q_rank, kv_rank):
    cq = lat_ref[:, :q_rank]
    cq_ref[...] = (cq * lax.rsqrt(jnp.mean(cq * cq, axis=-1, keepdims=True) + NORM_EPS)
                   * qg_ref[...]).astype(cq_ref.dtype)
    ckv = lat_ref[:, q_rank:q_rank + kv_rank]
    ckv_ref[...] = (ckv * lax.rsqrt(jnp.mean(ckv * ckv, axis=-1, keepdims=True) + NORM_EPS)
                    * kg_ref[...]).astype(ckv_ref.dtype)
    base = q_rank + kv_rank
    cc = cc_ref[...]
    ss = ss_ref[...]
    lo = lat_ref[:, base:base + LANE] * cc + lat_ref[:, base + LANE:base + 2 * LANE] * ss
    hi = lat_ref[:, base + 2 * LANE:base + 3 * LANE] * cc + lat_ref[:, base + 3 * LANE:base + 4 * LANE] * ss
    kr_ref[:, :LANE] = lo.astype(kr_ref.dtype)
    kr_ref[:, LANE:] = hi.astype(kr_ref.dtype)


def mla_lat_post(lat, q_norm, kv_norm, cc, ss):
    t = lat.shape[0]
    q_rank, kv_rank = q_norm.shape[0], kv_norm.shape[0]
    tm = _pick(t, (256, 128))
    row = lambda w: pl.BlockSpec((tm, w), lambda i: (i, 0))
    return pl.pallas_call(
        functools.partial(_mla_lat_post_kernel, q_rank=q_rank, kv_rank=kv_rank), grid=(t // tm,),
        in_specs=[row(lat.shape[1]), pl.BlockSpec((1, q_rank), lambda i: (0, 0)),
                  pl.BlockSpec((1, kv_rank), lambda i: (0, 0)), row(LANE), row(LANE)],
        out_specs=[row(q_rank), row(kv_rank), row(2 * LANE)],
        out_shape=[jax.ShapeDtypeStruct((t, q_rank), BF16), jax.ShapeDtypeStruct((t, kv_rank), BF16),
                   jax.ShapeDtypeStruct((t, 2 * LANE), BF16)],
        compiler_params=_cparams("parallel"),
    )(lat, q_norm.reshape(1, -1), kv_norm.reshape(1, -1), cc, ss)


def _q_proj_kernel(a_ref, b_ref, cc_ref, ss_ref, qn_ref, qr_ref):
    acc = jnp.dot(a_ref[...], b_ref[...], preferred_element_type=F32)
    qn_ref[...] = acc[:, :2 * LANE].astype(qn_ref.dtype)
    qr_ref[...] = (acc[:, 2 * LANE:3 * LANE] * cc_ref[...]
                   + acc[:, 3 * LANE:4 * LANE] * ss_ref[...]).astype(qr_ref.dtype)


def q_proj(cqn, w_pairs, cc, ss):
    t, k = cqn.shape
    n_pairs = w_pairs.shape[1] // (4 * LANE)
    tm = _pick(t, (1280, 1024, 640, 512, 256, 128))
    return pl.pallas_call(
        _q_proj_kernel, grid=(t // tm, n_pairs),
        in_specs=[pl.BlockSpec((tm, k), lambda i, j: (i, 0)), pl.BlockSpec((k, 4 * LANE), lambda i, j: (0, j)),
                  pl.BlockSpec((tm, LANE), lambda i, j: (i, 0)), pl.BlockSpec((tm, LANE), lambda i, j: (i, 0))],
        out_specs=[pl.BlockSpec((tm, 2 * LANE), lambda i, j: (i, j)), pl.BlockSpec((tm, LANE), lambda i, j: (i, j))],
        out_shape=[jax.ShapeDtypeStruct((t, n_pairs * 2 * LANE), BF16),
                   jax.ShapeDtypeStruct((t, n_pairs * LANE), BF16)],
        compiler_params=_cparams("parallel", "parallel"),
    )(cqn, w_pairs, cc, ss)


def _attn_kernel(qn_ref, qr_ref, kn_ref, kr_ref, vt_ref, o_ref, s_scr, acc_scr, *, tq, tk, unroll, n_ctx, n_all):
    q = jnp.concatenate([qn_ref[...], qr_ref[...]], axis=1)
    qt = q.astype(F32).T.astype(BF16)

    def scores(slot, off, size):
        k = jnp.concatenate([kn_ref[pl.ds(off, size), :], kr_ref[pl.ds(off, size), :]], axis=1)
        s = jnp.dot(k, qt, preferred_element_type=F32)
        s_scr[slot, 0:size, :] = s
        return jnp.max(s, axis=0, keepdims=True)

    def update(slot, mx, off, size, carry):
        m, l = carry
        m_new = jnp.maximum(m, mx)
        alpha = jnp.exp2(m - m_new)
        p = jnp.exp2(s_scr[slot, 0:size, :] - m_new)
        l = alpha * l + jnp.sum(p, axis=0, keepdims=True)
        acc_scr[...] = alpha * acc_scr[...] + jnp.dot(vt_ref[:, pl.ds(off, size)], p.astype(BF16),
                                                      preferred_element_type=F32)
        return m_new, l

    def finish(carry):
        o_ref[...] = (acc_scr[...] / carry[1]).T.astype(o_ref.dtype)

    acc_scr[...] = jnp.zeros(acc_scr.shape, F32)
    init = (jnp.full((1, tq), -jnp.inf, F32), jnp.zeros((1, tq), F32))
    is_ctx = pl.program_id(1) * tq < n_ctx

    @pl.when(is_ctx)
    def _():
        finish(update(0, scores(0, 0, n_ctx), 0, n_ctx, init))

    @pl.when(jnp.logical_not(is_ctx))
    def _():
        n_chunks = n_all // tk

        def run(base, count, carry, lookahead):
            mx, st = carry[0], carry[1:]
            for u in range(count):
                off = base + u * tk
                more = u + 1 < count or lookahead
                mx_next = scores((u + 1) % 2, off + tk, tk) if more else None
                st = update(u % 2, mx, off, tk, st)
                mx = mx_next
            return (mx,) + st

        def body(i, carry):
            return run(pl.multiple_of(i * (unroll * tk), tk), unroll, carry, True)

        n_trips = (n_chunks - 1) // unroll
        carry = lax.fori_loop(0, n_trips, body, (scores(0, 0, tk),) + init)
        finish(run(n_trips * unroll * tk, n_chunks - n_trips * unroll, carry, False)[1:])


def attention(qn, qr, kn, kr2, vt, n_ctx):
    t = qn.shape[0]
    heads = qn.shape[1] // QK_NOPE
    tq = _pick(n_ctx, (256, 128))
    tk = _pick(t, (3328, 1280, 512, 256, 128))
    assert n_ctx % tq == 0 and t % tq == 0 and n_ctx <= tk
    return pl.pallas_call(
        functools.partial(_attn_kernel, tq=tq, tk=tk, unroll=12, n_ctx=n_ctx, n_all=t),
        grid=(heads, t // tq),
        in_specs=[pl.BlockSpec((tq, LANE), lambda h, i: (i, h)),
                  pl.BlockSpec((tq, LANE), lambda h, i: (i, h // 2)),
                  pl.BlockSpec((t, LANE), lambda h, i: (0, h)),
                  pl.BlockSpec((t, LANE), lambda h, i: (0, h % 2)),
                  pl.BlockSpec((V_HEAD, t), lambda h, i: (h, 0))],
        out_specs=pl.BlockSpec((tq, LANE), lambda h, i: (i, h)),
        out_shape=jax.ShapeDtypeStruct((t, heads * V_HEAD), BF16),
        scratch_shapes=[pltpu.VMEM((2, tk, tq), F32), pltpu.VMEM((V_HEAD, tq), F32)],
        compiler_params=_cparams("parallel", "arbitrary"),
        name="mla_attention",
    )(qn, qr, kn, kr2, vt)


def _mm_nt_kernel(a_ref, b_ref, o_ref):
    o_ref[...] = lax.dot_general(a_ref[...], b_ref[...], (((1,), (1,)), ((), ())),
                                 preferred_element_type=F32).astype(o_ref.dtype)


def matmul_nt(a, b, out_dtype):
    m, k = a.shape
    n = b.shape[0]
    tm = _pick(m, (1024, 512, 256, 128))
    tn = _pick(n, (1280, 1024, 640, 512, 256, 128))
    return pl.pallas_call(
        _mm_nt_kernel, grid=(m // tm, n // tn),
        in_specs=[pl.BlockSpec((tm, k), lambda i, j: (i, 0)), pl.BlockSpec((tn, k), lambda i, j: (j, 0))],
        out_specs=pl.BlockSpec((tm, tn), lambda i, j: (i, j)),
        out_shape=jax.ShapeDtypeStruct((m, n), out_dtype),
        compiler_params=_cparams("parallel", "parallel"),
        name="matmul_nt",
    )(a, b)


def mla_layer(x, hn, p, gate2, n_ctx, cc, ss):
    lat = matmul(hn, p["w_in"], F32)
    cqn, ckvn, kr2 = mla_lat_post(lat, p["q_norm"], p["kv_norm"], cc, ss)
    qn, qr = q_proj(cqn, p["w_uq"], cc, ss)
    kn = matmul(ckvn, p["w_uk"], BF16)
    vt = matmul_nt(p["w_uv_t"], ckvn, BF16)
    o = attention(qn, qr, kn, kr2, vt, n_ctx)
    return matmul_residual(o, p["w_o"], x, gate2, n_ctx)


def prep_mla_weights(w_in, q_norm, kv_norm, w_uq, w_ukv, w_o):
    d = w_in.shape[0]
    heads = d // 128
    q_rank, kv_rank = q_norm.shape[0], kv_norm.shape[0]
    z64 = jnp.zeros((d, QK_ROPE), F32)
    wkr = w_in[:, q_rank + kv_rank:]
    wkr_rot = _rot_cols(wkr)
    w_in_p = jnp.concatenate([w_in[:, :q_rank + kv_rank], wkr, z64, wkr_rot, z64, z64, wkr, z64, wkr_rot], axis=1)
    scale = (QK_NOPE + QK_ROPE) ** -0.5 * math.log2(math.e)
    wq = (w_uq * scale).reshape(q_rank, heads, QK_NOPE + QK_ROPE)
    nope = wq[..., :QK_NOPE].reshape(q_rank, heads // 2, 2 * QK_NOPE)
    rope = wq[..., QK_NOPE:]
    rope_a = rope.reshape(q_rank, heads // 2, 2 * QK_ROPE)
    rope_b = _rot_cols(rope).reshape(q_rank, heads // 2, 2 * QK_ROPE)
    w_uq_p = jnp.concatenate([nope, rope_a, rope_b], axis=-1).reshape(q_rank, heads // 2 * 4 * LANE)
    wkv = w_ukv.reshape(kv_rank, heads, QK_NOPE + V_HEAD)
    w_uk = wkv[..., :QK_NOPE].reshape(kv_rank, heads * QK_NOPE)
    w_uv_t = wkv[..., QK_NOPE:].reshape(kv_rank, heads * V_HEAD).T
    return dict(w_in=w_in_p.astype(BF16), q_norm=q_norm, kv_norm=kv_norm, w_uq=w_uq_p.astype(BF16),
                w_uk=w_uk.astype(BF16), w_uv_t=w_uv_t.astype(BF16), w_o=w_o.astype(BF16))


def _conv_silu_kernel(prev_ref, cur_ref, next_ref, w_ref, b_ref, o_ref, win_ref, *, tm, n_ctx, n_tiles):
    i = pl.program_id(0)
    row0 = i * tm
    has_prev = jnp.logical_and(i > 0, row0 != n_ctx)
    has_next = jnp.logical_and(i < n_tiles - 1, row0 + tm != n_ctx)
    h = CONV_HALO
    win_ref[0:h, :] = jnp.where(has_prev, prev_ref[...].astype(F32), 0.0)
    win_ref[h:h + tm, :] = cur_ref[...].astype(F32)
    win_ref[h + tm:2 * h + tm, :] = jnp.where(has_next, next_ref[...].astype(F32), 0.0)
    pad = w_ref.shape[0] // 2
    acc = jnp.zeros(o_ref.shape, F32) + b_ref[...]
    for k in range(w_ref.shape[0]):
        acc = acc + w_ref[k:k + 1, :] * win_ref[pl.ds(h - pad + k, tm), :]
    o_ref[...] = _silu(acc).astype(o_ref.dtype)


CONV_HALO = 16


def conv_silu(u, w, b, n_ctx):
    t, c = u.shape
    tm = _pick(n_ctx, (256, 128))
    tn = _pick(c, (2048, 1024, 512, 256, 128))
    n_tiles = t // tm
    rh = tm // CONV_HALO
    nbh = t // CONV_HALO
    assert w.shape[0] // 2 <= CONV_HALO
    return pl.pallas_call(
        functools.partial(_conv_silu_kernel, tm=tm, n_ctx=n_ctx, n_tiles=n_tiles),
        grid=(n_tiles, c // tn),
        in_specs=[pl.BlockSpec((CONV_HALO, tn), lambda i, j: (jnp.maximum(i * rh - 1, 0), j)),
                  pl.BlockSpec((tm, tn), lambda i, j: (i, j)),
                  pl.BlockSpec((CONV_HALO, tn), lambda i, j: (jnp.minimum((i + 1) * rh, nbh - 1), j)),
                  pl.BlockSpec((w.shape[0], tn), lambda i, j: (0, j)),
                  pl.BlockSpec((1, tn), lambda i, j: (0, j))],
        out_specs=pl.BlockSpec((tm, tn), lambda i, j: (i, j)),
        out_shape=jax.ShapeDtypeStruct((t, c), BF16),
        scratch_shapes=[pltpu.VMEM((tm + 2 * CONV_HALO, tn), F32)],
        compiler_params=_cparams("parallel", "parallel"),
        name="conv_silu",
    )(u, u, u, w, b.reshape(1, c))


def _dt_prep_kernel(raw_ref, bias_ref, a_ref, dtf_ref, cf_ref, dtb_ref, cb_ref, *, nh, q):
    raw = raw_ref[...]
    r = lax.broadcasted_iota(jnp.int32, (q, q), 0)
    c = lax.broadcasted_iota(jnp.int32, (q, q), 1)
    for d, (dt_ref, c_ref) in enumerate(((dtf_ref, cf_ref), (dtb_ref, cb_ref))):
        v = raw[:, d * nh:(d + 1) * nh] + bias_ref[d:d + 1, :]
        dt = jnp.maximum(v, 0.0) + jnp.log1p(jnp.exp(-jnp.abs(v)))
        da = dt * a_ref[d:d + 1, :]
        tri = jnp.where(r >= c, 1.0, 0.0) if d == 0 else jnp.where(r <= c, 1.0, 0.0)
        cum = jnp.dot(tri.astype(F32), da, preferred_element_type=F32, precision=lax.Precision.HIGHEST)
        dt_ref[...] = dt.T
        c_ref[...] = cum.T


def dt_prep(dt_raw, dt_bias, a_log):
    t = dt_raw.shape[0]
    nh = dt_raw.shape[1] // 2
    q = SSM_CHUNK
    a = -jnp.exp(a_log.astype(F32))
    out = jax.ShapeDtypeStruct((nh, t), F32)
    ob = pl.BlockSpec((nh, q), lambda i: (0, i))
    return pl.pallas_call(
        functools.partial(_dt_prep_kernel, nh=nh, q=q), grid=(t // q,),
        in_specs=[pl.BlockSpec((q, 2 * nh), lambda i: (i, 0)), pl.BlockSpec((2, nh), lambda i: (0, 0)),
                  pl.BlockSpec((2, nh), lambda i: (0, 0))],
        out_specs=[ob, ob, ob, ob], out_shape=[out, out, out, out],
        compiler_params=_cparams("parallel"),
    )(dt_raw, dt_bias.astype(F32), a)


def _ssd_kernel(x_ref, b_ref, c_ref, dt_ref, cum_ref, y_ref, s_ref, *, hg, reverse, n_sub):
    q = SSM_CHUNK
    n_pairs = hg // 2

    @pl.when(pl.program_id(1) == 0)
    def _():
        s_ref[...] = jnp.zeros(s_ref.shape, s_ref.dtype)

    ii = lax.broadcasted_iota(jnp.int32, (q, q), 0)
    jj = lax.broadcasted_iota(jnp.int32, (q, q), 1)
    mask = (jj >= ii) if reverse else (ii >= jj)
    lane_lo = lax.broadcasted_iota(jnp.int32, (q, 2 * SSM_HEAD_DIM), 1) < SSM_HEAD_DIM
    last = 0 if reverse else q - 1

    for sub in (range(n_sub - 1, -1, -1) if reverse else range(n_sub)):
        r0 = sub * q
        bsub = b_ref[r0:r0 + q, :]
        csub = c_ref[r0:r0 + q, :]
        cmat = csub.astype(F32)
        cb = lax.dot_general(csub, bsub, (((1,), (1,)), ((), ())), preferred_element_type=F32)
        bt = bsub.astype(F32).T
        for pr in range(n_pairs):
            xp = x_ref[r0:r0 + q, pr * LANE:(pr + 1) * LANE]
            sp = s_ref[:, pr * LANE:(pr + 1) * LANE]
            rhs = jnp.concatenate([xp, sp.astype(BF16)], axis=0)
            ys, ss, decs = [], [], []
            for hh in range(2):
                h = pr * 2 + hh
                cum_h = cum_ref[h:h + 1, r0:r0 + q]
                crow = jnp.broadcast_to(cum_h, (q, q))
                ccol = crow.T
                dtrow = dt_ref[h:h + 1, r0:r0 + q]
                seg = jnp.where(mask, ccol - crow, -jnp.inf)
                mm = cb * jnp.exp(seg) * dtrow
                coff = cmat * jnp.exp(ccol)
                lhs = jnp.concatenate([mm.astype(BF16), coff.astype(BF16)], axis=1)
                ys.append(jnp.dot(lhs, rhs, preferred_element_type=F32))
                tot = cum_h[:, last:last + 1]
                wrow = jnp.exp(tot - cum_h) * dtrow
                ss.append(jnp.dot((bt * wrow).astype(BF16), xp, preferred_element_type=F32))
                decs.append(jnp.exp(tot))
            y_ref[r0:r0 + q, pr * LANE:(pr + 1) * LANE] = jnp.where(lane_lo, ys[0], ys[1]).astype(y_ref.dtype)
            dec = jnp.where(lane_lo[0:1, :], decs[0], decs[1])
            s_ref[:, pr * LANE:(pr + 1) * LANE] = sp * dec + jnp.where(lane_lo, ss[0], ss[1])


def ssd_scan(xbc, dt_t, cum_t, d_inner, n_ctx, reverse):
    t = xbc.shape[0]
    q = SSM_CHUNK
    nh = d_inner // SSM_HEAD_DIM
    hg = nh // SSM_GROUPS
    gw = hg * SSM_HEAD_DIM
    assert gw % LANE == 0 and hg % 8 == 0
    if reverse:
        n_sub = 2 if (t // q) % 2 == 0 and (n_ctx // q) % 2 == 0 else 1
    else:
        n_sub = _pick(t // q, (5, 2, 1))
    rows = n_sub * q
    nc = t // rows
    ncc = n_ctx // rows
    xoff = d_inner // LANE
    if reverse:
        cidx = lambda s: jnp.where(s < ncc, ncc - 1 - s, nc - 1 + ncc - s)
    else:
        cidx = lambda s: s
    return pl.pallas_call(
        functools.partial(_ssd_kernel, hg=hg, reverse=reverse, n_sub=n_sub),
        grid=(SSM_GROUPS, nc),
        in_specs=[pl.BlockSpec((rows, gw), lambda g, s: (cidx(s), g)),
                  pl.BlockSpec((rows, D_STATE), lambda g, s: (cidx(s), xoff + g)),
                  pl.BlockSpec((rows, D_STATE), lambda g, s: (cidx(s), xoff + SSM_GROUPS + g)),
                  pl.BlockSpec((hg, rows), lambda g, s: (g, cidx(s))),
                  pl.BlockSpec((hg, rows), lambda g, s: (g, cidx(s)))],
        out_specs=pl.BlockSpec((rows, gw), lambda g, s: (cidx(s), g)),
        out_shape=jax.ShapeDtypeStruct((t, d_inner), BF16),
        scratch_shapes=[pltpu.VMEM((D_STATE, gw), F32)],
        compiler_params=_cparams("parallel", "arbitrary"),
        name="ssd_scan",
    )(xbc, xbc, xbc, dt_t, cum_t)


def _ssm_finish_kernel(yf_ref, yb_ref, x_ref, z_ref, d_ref, g_ref, o_ref, *, n_groups):
    z = z_ref[...].astype(F32)
    y = (yf_ref[...].astype(F32) + yb_ref[...].astype(F32) + d_ref[...] * x_ref[...].astype(F32)) * _silu(z)
    gw = y.shape[1] // n_groups
    for g in range(n_groups):
        yg = y[:, g * gw:(g + 1) * gw]
        yn = yg * lax.rsqrt(jnp.mean(yg * yg, axis=-1, keepdims=True) + NORM_EPS)
        o_ref[:, g * gw:(g + 1) * gw] = (yn * g_ref[:, g * gw:(g + 1) * gw]).astype(o_ref.dtype)


def ssm_finish(yf, yb, xbc, z, d_exp, norm_g):
    t, di = yf.shape
    tm = _pick(t, (256, 128))
    row = pl.BlockSpec((tm, di), lambda i: (i, 0))
    vec = pl.BlockSpec((1, di), lambda i: (0, 0))
    return pl.pallas_call(
        functools.partial(_ssm_finish_kernel, n_groups=SSM_GROUPS), grid=(t // tm,),
        in_specs=[row, row, row, row, vec, vec], out_specs=row,
        out_shape=jax.ShapeDtypeStruct((t, di), BF16),
        compiler_params=_cparams("parallel"),
    )(yf, yb, xbc, z, d_exp, norm_g.reshape(1, di))


def mamba_layer(x, hn, p, gate2, n_ctx):
    di = p["d_inner"]
    z = matmul(hn, p["w_z"], BF16)
    xbc_raw = matmul(hn, p["w_xbc"], BF16)
    dt_raw = matmul(hn, p["w_dt"], F32)
    xbc = conv_silu(xbc_raw, p["conv_w"], p["conv_b"], n_ctx)
    dtf, cf, dtb, cb = dt_prep(dt_raw, p["dt_bias"], p["a_log"])
    yf = ssd_scan(xbc, dtf, cf, di, n_ctx, reverse=False)
    yb = ssd_scan(xbc, dtb, cb, di, n_ctx, reverse=True)
    yn = ssm_finish(yf, yb, xbc, z, p["d_exp"], p["norm_g"])
    return matmul_residual(yn, p["w_out"], x, gate2, n_ctx)


def prep_mamba_weights(w_in, conv_w, conv_b, dt_bias, a_log, d_skip, norm_g, w_out):
    di = norm_g.shape[0]
    conv_dim = conv_w.shape[1]
    nh = d_skip.shape[0]
    w_dt = w_in[:, di + conv_dim:]
    pad = (-w_dt.shape[1]) % LANE
    if pad:
        w_dt = jnp.concatenate([w_dt, jnp.zeros((w_in.shape[0], pad), F32)], axis=1)
    return dict(d_inner=di, w_z=w_in[:, :di].astype(BF16), w_xbc=w_in[:, di:di + conv_dim].astype(BF16),
                w_dt=w_dt.astype(BF16), conv_w=conv_w, conv_b=conv_b, dt_bias=dt_bias, a_log=a_log,
                d_exp=jnp.repeat(d_skip.astype(F32), SSM_HEAD_DIM).reshape(1, di), norm_g=norm_g,
                w_out=w_out.astype(BF16), n_heads=nh)


def _gating_kernel(lt_ref, b_ref, g_ref, grp_ref, *, n_experts):
    epg = n_experts // N_GROUPS
    s = [1.0 / (1.0 + jnp.exp(-lt_ref[e])) for e in range(n_experts)]
    sel = [s[e] + b_ref[e] for e in range(n_experts)]
    best_val, best_idx = None, None
    for g in range(N_GROUPS):
        mem = sel[g * epg:(g + 1) * epg]
        gs = None
        for a in range(epg):
            for b in range(a + 1, epg):
                pair = mem[a] + mem[b]
                gs = pair if gs is None else jnp.maximum(gs, pair)
        if g == 0:
            best_val, best_idx = gs, jnp.zeros(gs.shape, jnp.int32)
        else:
            better = gs > best_val
            best_val = jnp.where(better, gs, best_val)
            best_idx = jnp.where(better, g, best_idx)
    chosen, denom = [], None
    for e in range(n_experts):
        g = e // epg
        rank = jnp.zeros(best_idx.shape, jnp.int32)
        for m in range(g * epg, (g + 1) * epg):
            if m == e:
                continue
            ahead = (sel[m] > sel[e]) if m > e else (sel[m] >= sel[e])
            rank = rank + ahead.astype(jnp.int32)
        pick = jnp.logical_and(best_idx == g, rank < TOP_K)
        w = jnp.where(pick, s[e], 0.0)
        chosen.append(w)
        denom = w if denom is None else denom + w
    inv = 1.0 / denom
    for e in range(n_experts):
        g_ref[e] = chosen[e] * inv
    grp_ref[...] = best_idx


def gating(logits_t, router_b):
    e, t = logits_t.shape
    r = t // LANE
    lt3 = logits_t.reshape(e, r, LANE)
    gates, grp = pl.pallas_call(
        functools.partial(_gating_kernel, n_experts=e), grid=(1,),
        in_specs=[pl.BlockSpec((e, r, LANE), lambda i: (0, 0, 0)),
                  pl.BlockSpec(memory_space=pltpu.SMEM)],
        out_specs=[pl.BlockSpec((e, r, LANE), lambda i: (0, 0, 0)), pl.BlockSpec((r, LANE), lambda i: (0, 0))],
        out_shape=[jax.ShapeDtypeStruct((e, r, LANE), F32), jax.ShapeDtypeStruct((r, LANE), jnp.int32)],
        compiler_params=_cparams("arbitrary"),
        name="moe_gating",
    )(lt3, router_b.astype(F32))
    return gates.reshape(e, t), grp.reshape(t)


HI_HALF = 0xFFFF0000
GATHER_UNROLL = 8


def _pack_words(v):
    w = v.shape[1] // 2
    lo = pltpu.bitcast(v[:, :w].astype(jnp.bfloat16).astype(F32), jnp.uint32) >> 16
    hi = pltpu.bitcast(v[:, w:].astype(jnp.bfloat16).astype(F32), jnp.uint32) & jnp.uint32(HI_HALF)
    return lo | hi


def _unpack_words(words):
    return (pltpu.bitcast(words << 16, F32), pltpu.bitcast(words & jnp.uint32(HI_HALF), F32))


def _store_word_rows(ref3, words):
    for k in range(ref3.shape[1]):
        ref3[:, k, :] = words[:, k * LANE:(k + 1) * LANE]


def _load_word_rows(ref3):
    return jnp.concatenate([ref3[:, k, :] for k in range(ref3.shape[1])], axis=1)


def _row_gather_kernel(idx_ref, src_hbm, o_ref, sem, *, blk):
    def row_copy(q, src_row):
        return pltpu.make_async_copy(src_hbm.at[src_row], o_ref.at[q], sem)

    def issue(q, carry):
        row_copy(q, idx_ref[0, 0, q]).start()
        return carry

    def drain(q, carry):
        row_copy(q, 0).wait()
        return carry

    lax.fori_loop(0, blk, issue, 0, unroll=GATHER_UNROLL)
    lax.fori_loop(0, blk, drain, 0, unroll=GATHER_UNROLL)


def row_gather(src3, idx, blk):
    m = idx.shape[0]
    row_words = src3.shape[1:]
    assert m % blk == 0
    return pl.pallas_call(
        functools.partial(_row_gather_kernel, blk=blk), grid=(m // blk,),
        in_specs=[pl.BlockSpec((1, 1, blk), lambda b: (b, 0, 0), memory_space=pltpu.SMEM),
                  pl.BlockSpec(memory_space=pl.ANY)],
        out_specs=pl.BlockSpec((blk,) + row_words, lambda b: (b, 0, 0)),
        out_shape=jax.ShapeDtypeStruct((m,) + row_words, jnp.uint32),
        scratch_shapes=[pltpu.SemaphoreType.DMA(())],
        compiler_params=_cparams("arbitrary"),
        name="row_gather",
    )(idx.reshape(m // blk, 1, blk), src3)


def _group_ffn_kernel(tg_ref, tv_ref, x_ref, wg_ref, wu_ref, wd_ref, gs_ref, o_ref, acc_ref, xb_ref):
    r = pl.program_id(0)
    j = pl.program_id(1)
    f = pl.program_id(2)
    first = jnp.logical_and(j == 0, f == 0)
    last = jnp.logical_and(j == pl.num_programs(1) - 1, f == pl.num_programs(2) - 1)

    @pl.when(first)
    def _():
        acc_ref[...] = jnp.zeros(acc_ref.shape, F32)
        lo, hi = _unpack_words(_load_word_rows(x_ref))
        w = lo.shape[1]
        xb_ref[:, :w] = lo.astype(BF16)
        xb_ref[:, w:] = hi.astype(BF16)

    @pl.when(tv_ref[r] > 0)
    def _():
        x = xb_ref[...]
        hgate = jnp.dot(x, wg_ref[...], preferred_element_type=F32)
        hup = jnp.dot(x, wu_ref[...], preferred_element_type=F32)
        gts = gs_ref[...]
        lane = lax.broadcasted_iota(jnp.int32, gts.shape, 1)
        gcol = jnp.sum(jnp.where(lane == j, gts, 0.0), axis=1, keepdims=True)
        act = (_silu(hgate) * hup * gcol).astype(BF16)
        d = acc_ref.shape[1]
        tc = min(d, 1024)
        for c0 in range(0, d, tc):
            acc_ref[:, c0:c0 + tc] += jnp.dot(act, wd_ref[:, c0:c0 + tc], preferred_element_type=F32)

    @pl.when(last)
    def _():
        _store_word_rows(o_ref, _pack_words(acc_ref[...]))


def group_ffn(xs3, w_gate, w_up, w_down, gates_sorted, tile_group, tile_valid, tm):
    n_rows = xs3.shape[0]
    row_words = xs3.shape[1:]
    n_e, d, df = w_gate.shape
    epg = n_e // N_GROUPS
    tf = _pick(df, (256, 128))
    words = lambda: pl.BlockSpec((tm,) + row_words, lambda r, j, f, tg, tv: (r, 0, 0))
    grid_spec = pltpu.PrefetchScalarGridSpec(
        num_scalar_prefetch=2, grid=(n_rows // tm, epg, df // tf),
        in_specs=[words(),
                  pl.BlockSpec((None, d, tf), lambda r, j, f, tg, tv: (tg[r] * epg + j, 0, f)),
                  pl.BlockSpec((None, d, tf), lambda r, j, f, tg, tv: (tg[r] * epg + j, 0, f)),
                  pl.BlockSpec((None, tf, d), lambda r, j, f, tg, tv: (tg[r] * epg + j, f, 0)),
                  pl.BlockSpec((tm, epg), lambda r, j, f, tg, tv: (r, 0))],
        out_specs=words(),
        scratch_shapes=[pltpu.VMEM((tm, d), F32), pltpu.VMEM((tm, d), BF16)])
    return pl.pallas_call(
        _group_ffn_kernel, grid_spec=grid_spec,
        out_shape=jax.ShapeDtypeStruct((n_rows,) + row_words, jnp.uint32),
        compiler_params=_cparams("parallel", "arbitrary", "arbitrary"),
        name="group_ffn",
    )(tile_group, tile_valid, xs3, w_gate, w_up, w_down, gates_sorted)


def _shared_ffn_kernel(x_ref, wg_ref, wu_ref, wd_ref, y_ref, res_ref, g2_ref, o_ref, *, tm, n_ctx):
    g2 = _row_select(pl.program_id(0) * tm, tm, n_ctx, g2_ref)

    @pl.when(pl.program_id(1) == 0)
    def _():
        y = jnp.concatenate(_unpack_words(_load_word_rows(y_ref)), axis=1)
        o_ref[...] = res_ref[...] + g2 * y

    x = x_ref[...]
    act = (_silu(jnp.dot(x, wg_ref[...], preferred_element_type=F32))
           * jnp.dot(x, wu_ref[...], preferred_element_type=F32)).astype(BF16)
    o_ref[...] += g2 * jnp.dot(act, wd_ref[...], preferred_element_type=F32)


def shared_ffn_combine(hn, w_gate, w_up, w_down, y_routed, resid, gate2, n_ctx):
    t, d = hn.shape
    df = w_gate.shape[1]
    tm = _pick(t, (256, 128))
    tf = df
    row = lambda: pl.BlockSpec((tm, d), lambda i, f: (i, 0))
    return pl.pallas_call(
        functools.partial(_shared_ffn_kernel, tm=tm, n_ctx=n_ctx), grid=(t // tm, df // tf),
        in_specs=[row(), pl.BlockSpec((d, tf), lambda i, f: (0, f)), pl.BlockSpec((d, tf), lambda i, f: (0, f)),
                  pl.BlockSpec((tf, d), lambda i, f: (f, 0)),
                  pl.BlockSpec((tm,) + y_routed.shape[1:], lambda i, f: (i, 0, 0)), row(),
                  pl.BlockSpec((2, d), lambda i, f: (0, 0))],
        out_specs=row(),
        out_shape=jax.ShapeDtypeStruct((t, d), F32),
        compiler_params=_cparams("parallel", "arbitrary"),
        name="shared_ffn_combine",
    )(hn, w_gate, w_up, w_down, y_routed, resid, gate2)


def _group_sort_plan(grp, tm):
    t = grp.shape[0]
    n_tiles = (t + N_GROUPS * (tm - 1)) // tm
    n_rows = n_tiles * tm
    onehot = (grp[:, None] == jnp.arange(N_GROUPS, dtype=jnp.int32)[None, :]).astype(jnp.int32)
    rank = jnp.sum((jnp.cumsum(onehot, axis=0) - onehot) * onehot, axis=1)
    counts = jnp.sum(onehot, axis=0)
    padded = (counts + tm - 1) // tm * tm
    ends = jnp.cumsum(padded)
    dest = (ends - padded)[grp] + rank
    src = (jnp.arange(n_rows, dtype=jnp.int32) % t).at[dest].set(jnp.arange(t, dtype=jnp.int32))
    row_valid = jnp.zeros((n_rows,), jnp.int32).at[dest].set(1)
    tile_start = jnp.arange(n_tiles, dtype=jnp.int32) * tm
    tile_group = jnp.minimum(jnp.sum((tile_start[:, None] >= ends[None, :]).astype(jnp.int32), axis=1),
                             N_GROUPS - 1)
    tile_valid = (tile_start < ends[-1]).astype(jnp.int32)
    return dest, src, row_valid, tile_group, tile_valid


def moe_layer(x, hn, hn_words, logits_t, router_b, p, gate2, n_ctx):
    t = hn.shape[0]
    n_e = logits_t.shape[0]
    epg = n_e // N_GROUPS
    tm = _pick(t, (640, 512, 256, 128))
    gates_t, grp = gating(logits_t, router_b)
    dest, src, row_valid, tile_group, tile_valid = _group_sort_plan(grp, tm)
    gate_cols = grp[:, None] * epg + jnp.arange(epg, dtype=jnp.int32)[None, :]
    gates_tok = jnp.take_along_axis(gates_t.T, gate_cols, axis=1)
    gates_sorted = jnp.where(row_valid[:, None] > 0, gates_tok[src], 0.0)
    xs = row_gather(hn_words, src, tm)
    ys = group_ffn(xs, p["w_gate"], p["w_up"], p["w_down"], gates_sorted, tile_group, tile_valid, tm)
    y_tok = row_gather(ys, dest, tm)
    return shared_ffn_combine(hn, p["ws_gate"], p["ws_up"], p["ws_down"], y_tok, x, gate2, n_ctx)


def kernel(x, c, ctx, c_ctx, ada_down, ada_up, ada_bias, norm1_g, norm2_g, final_norm_g, mla_w_in, mla_q_norm, mla_kv_norm, mla_w_uq, mla_w_ukv, mla_w_o, ssm_w_in, ssm_conv_w, ssm_conv_b, ssm_dt_bias, ssm_a_log, ssm_d, ssm_norm_g, ssm_w_out, router_w, router_b, moe_w_gate, moe_w_up, moe_w_down, shared_w_gate, shared_w_up, shared_w_down):
    assert x.shape[0] == 1 and ctx.shape[0] == 1
    depth = ada_down.shape[0]
    n_lat, d = x.shape[1], x.shape[2]
    n_ctx = ctx.shape[1]
    xs = jnp.concatenate([ctx[0], x[0]], axis=0).astype(F32)

    cvecs = jnp.zeros((8, d), F32).at[0].set(c_ctx).at[1].set(c[0])
    mods = ada_modulation(cvecs, ada_down, ada_up, ada_bias)[:, :2]
    mods = mods.reshape(depth, 2, N_MOD, d)
    cc, ss = _rope_tables(n_ctx, n_lat)
    router_wt = router_w.T

    for i in range(depth):
        sh1, sc1, g1, sh2, sc2, g2 = (mods[i, :, k] for k in range(N_MOD))
        j = i // 2
        hn = norm_modulate(xs, norm1_g[i], sh1, sc1, n_ctx)
        if i % 2 == 0:
            p = prep_mla_weights(mla_w_in[j], mla_q_norm[j], mla_kv_norm[j], mla_w_uq[j], mla_w_ukv[j], mla_w_o[j])
            xs = mla_layer(xs, hn, p, g1, n_ctx, cc, ss)
        else:
            p = prep_mamba_weights(ssm_w_in[j], ssm_conv_w[j], ssm_conv_b[j], ssm_dt_bias[j], ssm_a_log[j],
                                   ssm_d[j], ssm_norm_g[j], ssm_w_out[j])
            xs = mamba_layer(xs, hn, p, g1, n_ctx)
        hn2, hn2_words, logits_t = norm_modulate(xs, norm2_g[i], sh2, sc2, n_ctx, router_wt=router_wt)
        pm = dict(w_gate=moe_w_gate[i].astype(BF16), w_up=moe_w_up[i].astype(BF16),
                  w_down=moe_w_down[i].astype(BF16), ws_gate=shared_w_gate[i].astype(BF16),
                  ws_up=shared_w_up[i].astype(BF16), ws_down=shared_w_down[i].astype(BF16))
        xs = moe_layer(xs, hn2, hn2_words, logits_t, router_b, pm, g2, n_ctx)
    return final_norm(xs, final_norm_g, n_ctx)[None]
```

```python
import functools
import math

import jax
import jax.numpy as jnp
from jax import lax
from jax.experimental import pallas as pl
from jax.experimental.pallas import tpu as pltpu

F32 = jnp.float32
BF16 = jnp.bfloat16

GRID_W = 64
ROPE_THETA = 10000.0
NORM_EPS = 1e-6
N_MOD = 6
QK_NOPE = 128
QK_ROPE = 64
V_HEAD = 128
SSM_HEAD_DIM = 64
SSM_GROUPS = 8
D_STATE = 128
SSM_CHUNK = 128
N_GROUPS = 4
TOP_K = 2

LANE = 128
VMEM_LIMIT_BYTES = 56 * 1024 * 1024


def _cparams(*sem):
    return pltpu.CompilerParams(dimension_semantics=sem, vmem_limit_bytes=VMEM_LIMIT_BYTES)


def _pick(n, prefs):
    for p in prefs:
        if p <= n and n % p == 0:
            return p
    return n


def _row_select(row0, tm, n_ctx, ref):
    row = row0 + lax.broadcasted_iota(jnp.int32, (tm, 1), 0)
    return jnp.where(row < n_ctx, ref[0:1, :], ref[1:2, :])


def _silu(v):
    return v * (1.0 / (1.0 + jnp.exp(-v)))


def _ada_kernel(c_ref, wd_ref, wu_ref, b_ref, o_ref):
    cv = c_ref[...]
    h = jnp.dot(_silu(cv), wd_ref[...], preferred_element_type=F32, precision=lax.Precision.HIGHEST)
    o_ref[...] = jnp.dot(h, wu_ref[...], preferred_element_type=F32,
                         precision=lax.Precision.HIGHEST) + b_ref[...]


def ada_modulation(cvecs, w_down, w_up, bias):
    depth, d, r = w_down.shape
    n = w_up.shape[-1]
    tn = _pick(n, (2048, 1024, 512, 256, 128))
    return pl.pallas_call(
        _ada_kernel,
        grid=(depth, n // tn),
        in_specs=[pl.BlockSpec((8, d), lambda l, j: (0, 0)),
                  pl.BlockSpec((None, d, r), lambda l, j: (l, 0, 0)),
                  pl.BlockSpec((None, r, tn), lambda l, j: (l, 0, j)),
                  pl.BlockSpec((None, 1, tn), lambda l, j: (l, 0, j))],
        out_specs=pl.BlockSpec((None, 8, tn), lambda l, j: (l, 0, j)),
        out_shape=jax.ShapeDtypeStruct((depth, 8, n), F32),
        compiler_params=_cparams("parallel", "parallel"),
    )(cvecs, w_down, w_up, bias.reshape(depth, 1, n))


def _norm_mod_kernel(x_ref, g_ref, sh_ref, sc_ref, o_ref, *, tm, n_ctx):
    x = x_ref[...]
    y = x * lax.rsqrt(jnp.mean(x * x, axis=-1, keepdims=True) + NORM_EPS) * g_ref[...]
    row0 = pl.program_id(0) * tm
    sh = _row_select(row0, tm, n_ctx, sh_ref)
    sc = _row_select(row0, tm, n_ctx, sc_ref)
    o_ref[...] = (y * (1.0 + sc) + sh).astype(o_ref.dtype)


def _norm_mod_router_kernel(x_ref, g_ref, sh_ref, sc_ref, rw_ref, o_ref, ow_ref, lt_ref, *, tm, n_ctx):
    x = x_ref[...]
    y = x * lax.rsqrt(jnp.mean(x * x, axis=-1, keepdims=True) + NORM_EPS) * g_ref[...]
    row0 = pl.program_id(0) * tm
    sh = _row_select(row0, tm, n_ctx, sh_ref)
    sc = _row_select(row0, tm, n_ctx, sc_ref)
    t = y * (1.0 + sc) + sh
    o_ref[...] = t.astype(o_ref.dtype)
    _store_word_rows(ow_ref, _pack_words(t))
    lt_ref[...] = lax.dot_general(rw_ref[...], t, (((1,), (1,)), ((), ())),
                                  preferred_element_type=F32, precision=lax.Precision.HIGHEST)


def norm_modulate(x, g, shift2, scale2, n_ctx, router_wt=None):
    t, d = x.shape
    tm = _pick(t, (256, 128))
    vec = pl.BlockSpec((1, d), lambda i: (0, 0))
    two = pl.BlockSpec((2, d), lambda i: (0, 0))
    xs = pl.BlockSpec((tm, d), lambda i: (i, 0))
    if router_wt is None:
        return pl.pallas_call(
            functools.partial(_norm_mod_kernel, tm=tm, n_ctx=n_ctx),
            grid=(t // tm,), in_specs=[xs, vec, two, two], out_specs=xs,
            out_shape=jax.ShapeDtypeStruct((t, d), BF16),
            compiler_params=_cparams("parallel"),
        )(x, g.reshape(1, d), shift2, scale2)
    e = router_wt.shape[0]
    row_words = (d // 2 // LANE, LANE)
    return pl.pallas_call(
        functools.partial(_norm_mod_router_kernel, tm=tm, n_ctx=n_ctx),
        grid=(t // tm,),
        in_specs=[xs, vec, two, two, pl.BlockSpec((e, d), lambda i: (0, 0))],
        out_specs=[xs, pl.BlockSpec((tm,) + row_words, lambda i: (i, 0, 0)), pl.BlockSpec((e, tm), lambda i: (0, i))],
        out_shape=[jax.ShapeDtypeStruct((t, d), BF16), jax.ShapeDtypeStruct((t,) + row_words, jnp.uint32),
                   jax.ShapeDtypeStruct((e, t), F32)],
        compiler_params=_cparams("parallel"),
        name="norm_modulate_router",
    )(x, g.reshape(1, d), shift2, scale2, router_wt)


def _final_norm_kernel(x_ref, g_ref, o_ref):
    x = x_ref[...]
    o_ref[...] = x * lax.rsqrt(jnp.mean(x * x, axis=-1, keepdims=True) + NORM_EPS) * g_ref[...]


def final_norm(x, g, row_start):
    t, d = x.shape
    tm = _pick(t, (256, 128))
    assert row_start % tm == 0
    off = row_start // tm
    return pl.pallas_call(
        _final_norm_kernel, grid=((t - row_start) // tm,),
        in_specs=[pl.BlockSpec((tm, d), lambda i: (i + off, 0)), pl.BlockSpec((1, d), lambda i: (0, 0))],
        out_specs=pl.BlockSpec((tm, d), lambda i: (i, 0)),
        out_shape=jax.ShapeDtypeStruct((t - row_start, d), F32),
        compiler_params=_cparams("parallel"),
    )(x, g.reshape(1, d))


def _mm_kernel(a_ref, b_ref, o_ref):
    o_ref[...] = jnp.dot(a_ref[...], b_ref[...], preferred_element_type=F32).astype(o_ref.dtype)


def _mm_tiles(m, k, n):
    tm = _pick(m, (1280, 1024, 640, 512, 256, 128))
    if k > 4096:
        tm = _pick(m, (640, 512, 256, 128))
    tn = _pick(n, (512, 256, 128))
    return tm, tn


def matmul(a, b, out_dtype):
    m, k = a.shape
    n = b.shape[1]
    tm, tn = _mm_tiles(m, k, n)
    return pl.pallas_call(
        _mm_kernel, grid=(m // tm, n // tn),
        in_specs=[pl.BlockSpec((tm, k), lambda i, j: (i, 0)), pl.BlockSpec((k, tn), lambda i, j: (0, j))],
        out_specs=pl.BlockSpec((tm, tn), lambda i, j: (i, j)),
        out_shape=jax.ShapeDtypeStruct((m, n), out_dtype),
        compiler_params=_cparams("parallel", "parallel"),
    )(a, b)


def _mm_res_kernel(a_ref, b_ref, r_ref, g_ref, o_ref, *, tm, n_ctx):
    acc = jnp.dot(a_ref[...], b_ref[...], preferred_element_type=F32)
    gate = _row_select(pl.program_id(0) * tm, tm, n_ctx, g_ref)
    o_ref[...] = r_ref[...] + gate * acc


def matmul_residual(a, b, resid, gate2, n_ctx):
    m, k = a.shape
    n = b.shape[1]
    tm, tn = _mm_tiles(m, k, n)
    return pl.pallas_call(
        functools.partial(_mm_res_kernel, tm=tm, n_ctx=n_ctx), grid=(m // tm, n // tn),
        in_specs=[pl.BlockSpec((tm, k), lambda i, j: (i, 0)), pl.BlockSpec((k, tn), lambda i, j: (0, j)),
                  pl.BlockSpec((tm, tn), lambda i, j: (i, j)), pl.BlockSpec((2, tn), lambda i, j: (0, j))],
        out_specs=pl.BlockSpec((tm, tn), lambda i, j: (i, j)),
        out_shape=jax.ShapeDtypeStruct((m, n), F32),
        compiler_params=_cparams("parallel", "parallel"),
    )(a, b, resid, gate2)


def _rope_tables(n_ctx, n_lat):
    rows = n_lat // GRID_W
    row = jnp.broadcast_to(jnp.arange(rows, dtype=F32)[:, None], (rows, GRID_W)).reshape(-1)
    col = jnp.broadcast_to(jnp.arange(GRID_W, dtype=F32)[None, :], (rows, GRID_W)).reshape(-1)
    n_freq = QK_ROPE // 4
    inv = ROPE_THETA ** (-jnp.arange(n_freq, dtype=F32) / n_freq)
    ang = jnp.concatenate([row[:, None] * inv, col[:, None] * inv], axis=-1)
    ang = jnp.concatenate([jnp.zeros((n_ctx, QK_ROPE // 2), F32), ang], axis=0)
    return jnp.tile(jnp.cos(ang), (1, 4)), jnp.tile(jnp.sin(ang), (1, 4))


def _rot_cols(w):
    h = QK_ROPE // 2
    return jnp.concatenate([-w[..., h:], w[..., :h]], axis=-1)


def _mla_lat_post_kernel(lat_ref, qg_ref, kg_ref, cc_ref, ss_ref, cq_ref, ckv_ref, kr_ref, *, q_rank, kv_rank):
    cq = lat_ref[:, :q_rank]
    cq_ref[...] = (cq * lax.rsqrt(jnp.mean(cq * cq, axis=-1, keepdims=True) + NORM_EPS)
                   * qg_ref[...]).astype(cq_ref.dtype)
    ckv = lat_ref[:, q_rank:q_rank + kv_rank]
    ckv_ref[...] = (ckv * lax.rsqrt(jnp.mean(ckv * ckv, axis=-1, keepdims=True) + NORM_EPS)
                    * kg_ref[...]).astype(ckv_ref.dtype)
    base = q_rank + kv_rank
    cc = cc_ref[...]
    ss = ss_ref[...]
    lo = lat_ref[:, base:base + LANE] * cc + lat_ref[:, base + LANE:base + 2 * LANE] * ss
    hi = lat_ref[:, base + 2 * LANE:base + 3 * LANE] * cc + lat_ref[:, base + 3 * LANE:base + 4 * LANE] * ss
    kr_ref[:, :LANE] = lo.astype(kr_ref.dtype)
    kr_ref[:, LANE:] = hi.astype(kr_ref.dtype)


def mla_lat_post(lat, q_norm, kv_norm, cc, ss):
    t = lat.shape[0]
    q_rank, kv_rank = q_norm.shape[0], kv_norm.shape[0]
    tm = _pick(t, (256, 128))
    row = lambda w: pl.BlockSpec((tm, w), lambda i: (i, 0))
    return pl.pallas_call(
        functools.partial(_mla_lat_post_kernel, q_rank=q_rank, kv_rank=kv_rank), grid=(t // tm,),
        in_specs=[row(lat.shape[1]), pl.BlockSpec((1, q_rank), lambda i: (0, 0)),
                  pl.BlockSpec((1, kv_rank), lambda i: (0, 0)), row(LANE), row(LANE)],
        out_specs=[row(q_rank), row(kv_rank), row(2 * LANE)],
        out_shape=[jax.ShapeDtypeStruct((t, q_rank), BF16), jax.ShapeDtypeStruct((t, kv_rank), BF16),
                   jax.ShapeDtypeStruct((t, 2 * LANE), BF16)],
        compiler_params=_cparams("parallel"),
    )(lat, q_norm.reshape(1, -1), kv_norm.reshape(1, -1), cc, ss)


def _q_proj_kernel(a_ref, b_ref, cc_ref, ss_ref, qn_ref, qr_ref):
    acc = jnp.dot(a_ref[...], b_ref[...], preferred_element_type=F32)
    qn_ref[...] = acc[:, :2 * LANE].astype(qn_ref.dtype)
    qr_ref[...] = (acc[:, 2 * LANE:3 * LANE] * cc_ref[...]
                   + acc[:, 3 * LANE:4 * LANE] * ss_ref[...]).astype(qr_ref.dtype)


def q_proj(cqn, w_pairs, cc, ss):
    t, k = cqn.shape
    n_pairs = w_pairs.shape[1] // (4 * LANE)
    tm = _pick(t, (1280, 1024, 640, 512, 256, 128))
    return pl.pallas_call(
        _q_proj_kernel, grid=(t // tm, n_pairs),
        in_specs=[pl.BlockSpec((tm, k), lambda i, j: (i, 0)), pl.BlockSpec((k, 4 * LANE), lambda i, j: (0, j)),
                  pl.BlockSpec((tm, LANE), lambda i, j: (i, 0)), pl.BlockSpec((tm, LANE), lambda i, j: (i, 0))],
        out_specs=[pl.BlockSpec((tm, 2 * LANE), lambda i, j: (i, j)), pl.BlockSpec((tm, LANE), lambda i, j: (i, j))],
        out_shape=[jax.ShapeDtypeStruct((t, n_pairs * 2 * LANE), BF16),
                   jax.ShapeDtypeStruct((t, n_pairs * LANE), BF16)],
        compiler_params=_cparams("parallel", "parallel"),
    )(cqn, w_pairs, cc, ss)


def _attn_kernel(qn_ref, qr_ref, kn_ref, kr_ref, vt_ref, o_ref, s_scr, acc_scr, *, tq, tk, unroll, n_ctx, n_all):
    q = jnp.concatenate([qn_ref[...], qr_ref[...]], axis=1)
    qt = q.astype(F32).T.astype(BF16)

    def scores(slot, off, size):
        k = jnp.concatenate([kn_ref[pl.ds(off, size), :], kr_ref[pl.ds(off, size), :]], axis=1)
        s = jnp.dot(k, qt, preferred_element_type=F32)
        s_scr[slot, 0:size, :] = s
        return jnp.max(s, axis=0, keepdims=True)

    def update(slot, mx, off, size, carry):
        m, l = carry
        m_new = jnp.maximum(m, mx)
        alpha = jnp.exp2(m - m_new)
        p = jnp.exp2(s_scr[slot, 0:size, :] - m_new)
        l = alpha * l + jnp.sum(p, axis=0, keepdims=True)
        acc_scr[...] = alpha * acc_scr[...] + jnp.dot(vt_ref[:, pl.ds(off, size)], p.astype(BF16),
                                                      preferred_element_type=F32)
        return m_new, l

    def finish(carry):
        o_ref[...] = (acc_scr[...] / carry[1]).T.astype(o_ref.dtype)

    acc_scr[...] = jnp.zeros(acc_scr.shape, F32)
    init = (jnp.full((1, tq), -jnp.inf, F32), jnp.zeros((1, tq), F32))
    is_ctx = pl.program_id(1) * tq < n_ctx

    @pl.when(is_ctx)
    def _():
        finish(update(0, scores(0, 0, n_ctx), 0, n_ctx, init))

    @pl.when(jnp.logical_not(is_ctx))
    def _():
        n_chunks = n_all // tk

        def run(base, count, carry, lookahead):
            mx, st = carry[0], carry[1:]
            for u in range(count):
                off = base + u * tk
                more = u + 1 < count or lookahead
                mx_next = scores((u + 1) % 2, off + tk, tk) if more else None
                st = update(u % 2, mx, off, tk, st)
                mx = mx_next
            return (mx,) + st

        def body(i, carry):
            return run(pl.multiple_of(i * (unroll * tk), tk), unroll, carry, True)

        n_trips = (n_chunks - 1) // unroll
        carry = lax.fori_loop(0, n_trips, body, (scores(0, 0, tk),) + init)
        finish(run(n_trips * unroll * tk, n_chunks - n_trips * unroll, carry, False)[1:])


def attention(qn, qr, kn, kr2, vt, n_ctx):
    t = qn.shape[0]
    heads = qn.shape[1] // QK_NOPE
    tq = _pick(n_ctx, (256, 128))
    tk = _pick(t, (3328, 1280, 512, 256, 128))
    assert n_ctx % tq == 0 and t % tq == 0 and n_ctx <= tk
    return pl.pallas_call(
        functools.partial(_attn_kernel, tq=tq, tk=tk, unroll=12, n_ctx=n_ctx, n_all=t),
        grid=(heads, t // tq),
        in_specs=[pl.BlockSpec((tq, LANE), lambda h, i: (i, h)),
                  pl.BlockSpec((tq, LANE), lambda h, i: (i, h // 2)),
                  pl.BlockSpec((t, LANE), lambda h, i: (0, h)),
                  pl.BlockSpec((t, LANE), lambda h, i: (0, h % 2)),
                  pl.BlockSpec((V_HEAD, t), lambda h, i: (h, 0))],
        out_specs=pl.BlockSpec((tq, LANE), lambda h, i: (i, h)),
        out_shape=jax.ShapeDtypeStruct((t, heads * V_HEAD), BF16),
        scratch_shapes=[pltpu.VMEM((2, tk, tq), F32), pltpu.VMEM((V_HEAD, tq), F32)],
        compiler_params=_cparams("parallel", "arbitrary"),
        name="mla_attention",
    )(qn, qr, kn, kr2, vt)


def _mm_nt_kernel(a_ref, b_ref, o_ref):
    o_ref[...] = lax.dot_general(a_ref[...], b_ref[...], (((1,), (1,)), ((), ())),
                                 preferred_element_type=F32).astype(o_ref.dtype)


def matmul_nt(a, b, out_dtype):
    m, k = a.shape
    n = b.shape[0]
    tm = _pick(m, (1024, 512, 256, 128))
    tn = _pick(n, (1280, 1024, 640, 512, 256, 128))
    return pl.pallas_call(
        _mm_nt_kernel, grid=(m // tm, n // tn),
        in_specs=[pl.BlockSpec((tm, k), lambda i, j: (i, 0)), pl.BlockSpec((tn, k), lambda i, j: (j, 0))],
        out_specs=pl.BlockSpec((tm, tn), lambda i, j: (i, j)),
        out_shape=jax.ShapeDtypeStruct((m, n), out_dtype),
        compiler_params=_cparams("parallel", "parallel"),
        name="matmul_nt",
    )(a, b)


def mla_layer(x, hn, p, gate2, n_ctx, cc, ss):
    lat = matmul(hn, p["w_in"], F32)
    cqn, ckvn, kr2 = mla_lat_post(lat, p["q_norm"], p["kv_norm"], cc, ss)
    qn, qr = q_proj(cqn, p["w_uq"], cc, ss)
    kn = matmul(ckvn, p["w_uk"], BF16)
    vt = matmul_nt(p["w_uv_t"], ckvn, BF16)
    o = attention(qn, qr, kn, kr2, vt, n_ctx)
    return matmul_residual(o, p["w_o"], x, gate2, n_ctx)


def prep_mla_weights(w_in, q_norm, kv_norm, w_uq, w_ukv, w_o):
    d = w_in.shape[0]
    heads = d // 128
    q_rank, kv_rank = q_norm.shape[0], kv_norm.shape[0]
    z64 = jnp.zeros((d, QK_ROPE), F32)
    wkr = w_in[:, q_rank + kv_rank:]
    wkr_rot = _rot_cols(wkr)
    w_in_p = jnp.concatenate([w_in[:, :q_rank + kv_rank], wkr, z64, wkr_rot, z64, z64, wkr, z64, wkr_rot], axis=1)
    scale = (QK_NOPE + QK_ROPE) ** -0.5 * math.log2(math.e)
    wq = (w_uq * scale).reshape(q_rank, heads, QK_NOPE + QK_ROPE)
    nope = wq[..., :QK_NOPE].reshape(q_rank, heads // 2, 2 * QK_NOPE)
    rope = wq[..., QK_NOPE:]
    rope_a = rope.reshape(q_rank, heads // 2, 2 * QK_ROPE)
    rope_b = _rot_cols(rope).reshape(q_rank, heads // 2, 2 * QK_ROPE)
    w_uq_p = jnp.concatenate([nope, rope_a, rope_b], axis=-1).reshape(q_rank, heads // 2 * 4 * LANE)
    wkv = w_ukv.reshape(kv_rank, heads, QK_NOPE + V_HEAD)
    w_uk = wkv[..., :QK_NOPE].reshape(kv_rank, heads * QK_NOPE)
    w_uv_t = wkv[..., QK_NOPE:].reshape(kv_rank, heads * V_HEAD).T
    return dict(w_in=w_in_p.astype(BF16), q_norm=q_norm, kv_norm=kv_norm, w_uq=w_uq_p.astype(BF16),
                w_uk=w_uk.astype(BF16), w_uv_t=w_uv_t.astype(BF16), w_o=w_o.astype(BF16))


def _conv_silu_kernel(prev_ref, cur_ref, next_ref, w_ref, b_ref, o_ref, win_ref, *, tm, n_ctx, n_tiles):
    i = pl.program_id(0)
    row0 = i * tm
    has_prev = jnp.logical_and(i > 0, row0 != n_ctx)
    has_next = jnp.logical_and(i < n_tiles - 1, row0 + tm != n_ctx)
    h = CONV_HALO
    win_ref[0:h, :] = jnp.where(has_prev, prev_ref[...].astype(F32), 0.0)
    win_ref[h:h + tm, :] = cur_ref[...].astype(F32)
    win_ref[h + tm:2 * h + tm, :] = jnp.where(has_next, next_ref[...].astype(F32), 0.0)
    pad = w_ref.shape[0] // 2
    acc = jnp.zeros(o_ref.shape, F32) + b_ref[...]
    win = win_ref[...]
    n = tm + 2 * h
    for k in range(w_ref.shape[0]):
        rolled = win if k == pad else pltpu.roll(win, (pad - k) % n, 0)
        acc = acc + w_ref[k:k + 1, :] * rolled[h:h + tm, :]
    o_ref[...] = _silu(acc).astype(o_ref.dtype)


CONV_HALO = 16


def conv_silu(u, w, b, n_ctx):
    t, c = u.shape
    tm = _pick(n_ctx, (256, 128))
    tn = _pick(c, (2048, 1024, 512, 256, 128))
    n_tiles = t // tm
    rh = tm // CONV_HALO
    nbh = t // CONV_HALO
    assert w.shape[0] // 2 <= CONV_HALO
    return pl.pallas_call(
        functools.partial(_conv_silu_kernel, tm=tm, n_ctx=n_ctx, n_tiles=n_tiles),
        grid=(n_tiles, c // tn),
        in_specs=[pl.BlockSpec((CONV_HALO, tn), lambda i, j: (jnp.maximum(i * rh - 1, 0), j)),
                  pl.BlockSpec((tm, tn), lambda i, j: (i, j)),
                  pl.BlockSpec((CONV_HALO, tn), lambda i, j: (jnp.minimum((i + 1) * rh, nbh - 1), j)),
                  pl.BlockSpec((w.shape[0], tn), lambda i, j: (0, j)),
                  pl.BlockSpec((1, tn), lambda i, j: (0, j))],
        out_specs=pl.BlockSpec((tm, tn), lambda i, j: (i, j)),
        out_shape=jax.ShapeDtypeStruct((t, c), BF16),
        scratch_shapes=[pltpu.VMEM((tm + 2 * CONV_HALO, tn), F32)],
        compiler_params=_cparams("parallel", "parallel"),
        name="conv_silu",
    )(u, u, u, w, b.reshape(1, c))


def _dt_prep_kernel(raw_ref, bias_ref, a_ref, dtf_ref, cf_ref, dtb_ref, cb_ref, *, nh, q):
    raw = raw_ref[...]
    r = lax.broadcasted_iota(jnp.int32, (q, q), 0)
    c = lax.broadcasted_iota(jnp.int32, (q, q), 1)
    for d, (dt_ref, c_ref) in enumerate(((dtf_ref, cf_ref), (dtb_ref, cb_ref))):
        v = raw[:, d * nh:(d + 1) * nh] + bias_ref[d:d + 1, :]
        dt = jnp.maximum(v, 0.0) + jnp.log1p(jnp.exp(-jnp.abs(v)))
        da = dt * a_ref[d:d + 1, :]
        tri = jnp.where(r >= c, 1.0, 0.0) if d == 0 else jnp.where(r <= c, 1.0, 0.0)
        cum = jnp.dot(tri.astype(F32), da, preferred_element_type=F32, precision=lax.Precision.HIGHEST)
        dt_ref[...] = dt.T
        c_ref[...] = cum.T


def dt_prep(dt_raw, dt_bias, a_log):
    t = dt_raw.shape[0]
    nh = dt_raw.shape[1] // 2
    q = SSM_CHUNK
    a = -jnp.exp(a_log.astype(F32))
    out = jax.ShapeDtypeStruct((nh, t), F32)
    ob = pl.BlockSpec((nh, q), lambda i: (0, i))
    return pl.pallas_call(
        functools.partial(_dt_prep_kernel, nh=nh, q=q), grid=(t // q,),
        in_specs=[pl.BlockSpec((q, 2 * nh), lambda i: (i, 0)), pl.BlockSpec((2, nh), lambda i: (0, 0)),
                  pl.BlockSpec((2, nh), lambda i: (0, 0))],
        out_specs=[ob, ob, ob, ob], out_shape=[out, out, out, out],
        compiler_params=_cparams("parallel"),
    )(dt_raw, dt_bias.astype(F32), a)


def _ssd_kernel(x_ref, b_ref, c_ref, dt_ref, cum_ref, y_ref, s_ref, *, hg, reverse, n_sub):
    q = SSM_CHUNK
    n_pairs = hg // 2

    @pl.when(pl.program_id(1) == 0)
    def _():
        s_ref[...] = jnp.zeros(s_ref.shape, s_ref.dtype)

    ii = lax.broadcasted_iota(jnp.int32, (q, q), 0)
    jj = lax.broadcasted_iota(jnp.int32, (q, q), 1)
    mask = (jj >= ii) if reverse else (ii >= jj)
    lane_lo = lax.broadcasted_iota(jnp.int32, (q, 2 * SSM_HEAD_DIM), 1) < SSM_HEAD_DIM
    last = 0 if reverse else q - 1

    for sub in (range(n_sub - 1, -1, -1) if reverse else range(n_sub)):
        r0 = sub * q
        bsub = b_ref[r0:r0 + q, :]
        csub = c_ref[r0:r0 + q, :]
        cmat = csub.astype(F32)
        cb = lax.dot_general(csub, bsub, (((1,), (1,)), ((), ())), preferred_element_type=F32)
        bt = bsub.astype(F32).T
        for pr in range(n_pairs):
            xp = x_ref[r0:r0 + q, pr * LANE:(pr + 1) * LANE]
            sp = s_ref[:, pr * LANE:(pr + 1) * LANE]
            rhs = jnp.concatenate([xp, sp.astype(BF16)], axis=0)
            ys, ss, decs = [], [], []
            for hh in range(2):
                h = pr * 2 + hh
                cum_h = cum_ref[h:h + 1, r0:r0 + q]
                crow = jnp.broadcast_to(cum_h, (q, q))
                ccol = crow.T
                dtrow = dt_ref[h:h + 1, r0:r0 + q]
                seg = jnp.where(mask, ccol - crow, -jnp.inf)
                mm = cb * jnp.exp(seg) * dtrow
                coff = cmat * jnp.exp(ccol)
                lhs = jnp.concatenate([mm.astype(BF16), coff.astype(BF16)], axis=1)
                ys.append(jnp.dot(lhs, rhs, preferred_element_type=F32))
                tot = cum_h[:, last:last + 1]
                wrow = jnp.exp(tot - cum_h) * dtrow
                ss.append(jnp.dot((bt * wrow).astype(BF16), xp, preferred_element_type=F32))
                decs.append(jnp.exp(tot))
            y_ref[r0:r0 + q, pr * LANE:(pr + 1) * LANE] = jnp.where(lane_lo, ys[0], ys[1]).astype(y_ref.dtype)
            dec = jnp.where(lane_lo[0:1, :], decs[0], decs[1])
            s_ref[:, pr * LANE:(pr + 1) * LANE] = sp * dec + jnp.where(lane_lo, ss[0], ss[1])


def ssd_scan(xbc, dt_t, cum_t, d_inner, n_ctx, reverse):
    t = xbc.shape[0]
    q = SSM_CHUNK
    nh = d_inner // SSM_HEAD_DIM
    hg = nh // SSM_GROUPS
    gw = hg * SSM_HEAD_DIM
    assert gw % LANE == 0 and hg % 8 == 0
    if reverse:
        n_sub = 2 if (t // q) % 2 == 0 and (n_ctx // q) % 2 == 0 else 1
    else:
        n_sub = _pick(t // q, (5, 2, 1))
    rows = n_sub * q
    nc = t // rows
    ncc = n_ctx // rows
    xoff = d_inner // LANE
    if reverse:
        cidx = lambda s: jnp.where(s < ncc, ncc - 1 - s, nc - 1 + ncc - s)
    else:
        cidx = lambda s: s
    return pl.pallas_call(
        functools.partial(_ssd_kernel, hg=hg, reverse=reverse, n_sub=n_sub),
        grid=(SSM_GROUPS, nc),
        in_specs=[pl.BlockSpec((rows, gw), lambda g, s: (cidx(s), g)),
                  pl.BlockSpec((rows, D_STATE), lambda g, s: (cidx(s), xoff + g)),
                  pl.BlockSpec((rows, D_STATE), lambda g, s: (cidx(s), xoff + SSM_GROUPS + g)),
                  pl.BlockSpec((hg, rows), lambda g, s: (g, cidx(s))),
                  pl.BlockSpec((hg, rows), lambda g, s: (g, cidx(s)))],
        out_specs=pl.BlockSpec((rows, gw), lambda g, s: (cidx(s), g)),
        out_shape=jax.ShapeDtypeStruct((t, d_inner), BF16),
        scratch_shapes=[pltpu.VMEM((D_STATE, gw), F32)],
        compiler_params=_cparams("parallel", "arbitrary"),
        name="ssd_scan",
    )(xbc, xbc, xbc, dt_t, cum_t)


def _ssm_finish_kernel(yf_ref, yb_ref, x_ref, z_ref, d_ref, g_ref, o_ref, *, n_groups):
    z = z_ref[...].astype(F32)
    y = (yf_ref[...].astype(F32) + yb_ref[...].astype(F32) + d_ref[...] * x_ref[...].astype(F32)) * _silu(z)
    gw = y.shape[1] // n_groups
    for g in range(n_groups):
        yg = y[:, g * gw:(g + 1) * gw]
        yn = yg * lax.rsqrt(jnp.mean(yg * yg, axis=-1, keepdims=True) + NORM_EPS)
        o_ref[:, g * gw:(g + 1) * gw] = (yn * g_ref[:, g * gw:(g + 1) * gw]).astype(o_ref.dtype)


def ssm_finish(yf, yb, xbc, z, d_exp, norm_g):
    t, di = yf.shape
    tm = _pick(t, (256, 128))
    row = pl.BlockSpec((tm, di), lambda i: (i, 0))
    vec = pl.BlockSpec((1, di), lambda i: (0, 0))
    return pl.pallas_call(
        functools.partial(_ssm_finish_kernel, n_groups=SSM_GROUPS), grid=(t // tm,),
        in_specs=[row, row, row, row, vec, vec], out_specs=row,
        out_shape=jax.ShapeDtypeStruct((t, di), BF16),
        compiler_params=_cparams("parallel"),
    )(yf, yb, xbc, z, d_exp, norm_g.reshape(1, di))


def mamba_layer(x, hn, p, gate2, n_ctx):
    di = p["d_inner"]
    z = matmul(hn, p["w_z"], BF16)
    xbc_raw = matmul(hn, p["w_xbc"], BF16)
    dt_raw = matmul(hn, p["w_dt"], F32)
    xbc = conv_silu(xbc_raw, p["conv_w"], p["conv_b"], n_ctx)
    dtf, cf, dtb, cb = dt_prep(dt_raw, p["dt_bias"], p["a_log"])
    yf = ssd_scan(xbc, dtf, cf, di, n_ctx, reverse=False)
    yb = ssd_scan(xbc, dtb, cb, di, n_ctx, reverse=True)
    yn = ssm_finish(yf, yb, xbc, z, p["d_exp"], p["norm_g"])
    return matmul_residual(yn, p["w_out"], x, gate2, n_ctx)


def prep_mamba_weights(w_in, conv_w, conv_b, dt_bias, a_log, d_skip, norm_g, w_out):
    di = norm_g.shape[0]
    conv_dim = conv_w.shape[1]
    nh = d_skip.shape[0]
    w_dt = w_in[:, di + conv_dim:]
    pad = (-w_dt.shape[1]) % LANE
    if pad:
        w_dt = jnp.concatenate([w_dt, jnp.zeros((w_in.shape[0], pad), F32)], axis=1)
    return dict(d_inner=di, w_z=w_in[:, :di].astype(BF16), w_xbc=w_in[:, di:di + conv_dim].astype(BF16),
                w_dt=w_dt.astype(BF16), conv_w=conv_w, conv_b=conv_b, dt_bias=dt_bias, a_log=a_log,
                d_exp=jnp.repeat(d_skip.astype(F32), SSM_HEAD_DIM).reshape(1, di), norm_g=norm_g,
                w_out=w_out.astype(BF16), n_heads=nh)


def _gating_kernel(lt_ref, b_ref, g_ref, grp_ref, *, n_experts):
    epg = n_experts // N_GROUPS
    s = [1.0 / (1.0 + jnp.exp(-lt_ref[e])) for e in range(n_experts)]
    sel = [s[e] + b_ref[e] for e in range(n_experts)]
    best_val, best_idx = None, None
    for g in range(N_GROUPS):
        mem = sel[g * epg:(g + 1) * epg]
        gs = None
        for a in range(epg):
            for b in range(a + 1, epg):
                pair = mem[a] + mem[b]
                gs = pair if gs is None else jnp.maximum(gs, pair)
        if g == 0:
            best_val, best_idx = gs, jnp.zeros(gs.shape, jnp.int32)
        else:
            better = gs > best_val
            best_val = jnp.where(better, gs, best_val)
            best_idx = jnp.where(better, g, best_idx)
    chosen, denom = [], None
    for e in range(n_experts):
        g = e // epg
        rank = jnp.zeros(best_idx.shape, jnp.int32)
        for m in range(g * epg, (g + 1) * epg):
            if m == e:
                continue
            ahead = (sel[m] > sel[e]) if m > e else (sel[m] >= sel[e])
            rank = rank + ahead.astype(jnp.int32)
        pick = jnp.logical_and(best_idx == g, rank < TOP_K)
        w = jnp.where(pick, s[e], 0.0)
        chosen.append(w)
        denom = w if denom is None else denom + w
    inv = 1.0 / denom
    for e in range(n_experts):
        g_ref[e] = chosen[e] * inv
    grp_ref[...] = best_idx


def gating(logits_t, router_b):
    e, t = logits_t.shape
    r = t // LANE
    lt3 = logits_t.reshape(e, r, LANE)
    gates, grp = pl.pallas_call(
        functools.partial(_gating_kernel, n_experts=e), grid=(1,),
        in_specs=[pl.BlockSpec((e, r, LANE), lambda i: (0, 0, 0)),
                  pl.BlockSpec(memory_space=pltpu.SMEM)],
        out_specs=[pl.BlockSpec((e, r, LANE), lambda i: (0, 0, 0)), pl.BlockSpec((r, LANE), lambda i: (0, 0))],
        out_shape=[jax.ShapeDtypeStruct((e, r, LANE), F32), jax.ShapeDtypeStruct((r, LANE), jnp.int32)],
        compiler_params=_cparams("arbitrary"),
        name="moe_gating",
    )(lt3, router_b.astype(F32))
    return gates.reshape(e, t), grp.reshape(t)


HI_HALF = 0xFFFF0000
GATHER_UNROLL = 8


def _pack_words(v):
    w = v.shape[1] // 2
    lo = pltpu.bitcast(v[:, :w].astype(jnp.bfloat16).astype(F32), jnp.uint32) >> 16
    hi = pltpu.bitcast(v[:, w:].astype(jnp.bfloat16).astype(F32), jnp.uint32) & jnp.uint32(HI_HALF)
    return lo | hi


def _unpack_words(words):
    return (pltpu.bitcast(words << 16, F32), pltpu.bitcast(words & jnp.uint32(HI_HALF), F32))


def _store_word_rows(ref3, words):
    for k in range(ref3.shape[1]):
        ref3[:, k, :] = words[:, k * LANE:(k + 1) * LANE]


def _load_word_rows(ref3):
    return jnp.concatenate([ref3[:, k, :] for k in range(ref3.shape[1])], axis=1)


def _row_gather_kernel(idx_ref, src_hbm, o_ref, sem, *, blk):
    def row_copy(q, src_row):
        return pltpu.make_async_copy(src_hbm.at[src_row], o_ref.at[q], sem)

    def issue(q, carry):
        row_copy(q, idx_ref[0, 0, q]).start()
        return carry

    def drain(q, carry):
        row_copy(q, 0).wait()
        return carry

    lax.fori_loop(0, blk, issue, 0, unroll=GATHER_UNROLL)
    lax.fori_loop(0, blk, drain, 0, unroll=GATHER_UNROLL)


def row_gather(src3, idx, blk):
    m = idx.shape[0]
    row_words = src3.shape[1:]
    assert m % blk == 0
    return pl.pallas_call(
        functools.partial(_row_gather_kernel, blk=blk), grid=(m // blk,),
        in_specs=[pl.BlockSpec((1, 1, blk), lambda b: (b, 0, 0), memory_space=pltpu.SMEM),
                  pl.BlockSpec(memory_space=pl.ANY)],
        out_specs=pl.BlockSpec((blk,) + row_words, lambda b: (b, 0, 0)),
        out_shape=jax.ShapeDtypeStruct((m,) + row_words, jnp.uint32),
        scratch_shapes=[pltpu.SemaphoreType.DMA(())],
        compiler_params=_cparams("arbitrary"),
        name="row_gather",
    )(idx.reshape(m // blk, 1, blk), src3)


def _group_ffn_kernel(tg_ref, tv_ref, x_ref, wg_ref, wu_ref, wd_ref, gs_ref, o_ref, acc_ref, xb_ref):
    r = pl.program_id(0)
    j = pl.program_id(1)
    f = pl.program_id(2)
    first = jnp.logical_and(j == 0, f == 0)
    last = jnp.logical_and(j == pl.num_programs(1) - 1, f == pl.num_programs(2) - 1)

    @pl.when(first)
    def _():
        acc_ref[...] = jnp.zeros(acc_ref.shape, F32)
        lo, hi = _unpack_words(_load_word_rows(x_ref))
        w = lo.shape[1]
        xb_ref[:, :w] = lo.astype(BF16)
        xb_ref[:, w:] = hi.astype(BF16)

    @pl.when(tv_ref[r] > 0)
    def _():
        x = xb_ref[...]
        hgate = jnp.dot(x, wg_ref[...], preferred_element_type=F32)
        hup = jnp.dot(x, wu_ref[...], preferred_element_type=F32)
        gts = gs_ref[...]
        lane = lax.broadcasted_iota(jnp.int32, gts.shape, 1)
        gcol = jnp.sum(jnp.where(lane == j, gts, 0.0), axis=1, keepdims=True)
        act = (_silu(hgate) * hup * gcol).astype(BF16)
        d = acc_ref.shape[1]
        tc = min(d, 1024)
        for c0 in range(0, d, tc):
            acc_ref[:, c0:c0 + tc] += jnp.dot(act, wd_ref[:, c0:c0 + tc], preferred_element_type=F32)

    @pl.when(last)
    def _():
        _store_word_rows(o_ref, _pack_words(acc_ref[...]))


def group_ffn(xs3, w_gate, w_up, w_down, gates_sorted, tile_group, tile_valid, tm):
    n_rows = xs3.shape[0]
    row_words = xs3.shape[1:]
    n_e, d, df = w_gate.shape
    epg = n_e // N_GROUPS
    tf = _pick(df, (256, 128))
    words = lambda: pl.BlockSpec((tm,) + row_words, lambda r, j, f, tg, tv: (r, 0, 0))
    grid_spec = pltpu.PrefetchScalarGridSpec(
        num_scalar_prefetch=2, grid=(n_rows // tm, epg, df // tf),
        in_specs=[words(),
                  pl.BlockSpec((None, d, tf), lambda r, j, f, tg, tv: (tg[r] * epg + j, 0, f)),
                  pl.BlockSpec((None, d, tf), lambda r, j, f, tg, tv: (tg[r] * epg + j, 0, f)),
                  pl.BlockSpec((None, tf, d), lambda r, j, f, tg, tv: (tg[r] * epg + j, f, 0)),
                  pl.BlockSpec((tm, epg), lambda r, j, f, tg, tv: (r, 0))],
        out_specs=words(),
        scratch_shapes=[pltpu.VMEM((tm, d), F32), pltpu.VMEM((tm, d), BF16)])
    return pl.pallas_call(
        _group_ffn_kernel, grid_spec=grid_spec,
        out_shape=jax.ShapeDtypeStruct((n_rows,) + row_words, jnp.uint32),
        compiler_params=_cparams("parallel", "arbitrary", "arbitrary"),
        name="group_ffn",
    )(tile_group, tile_valid, xs3, w_gate, w_up, w_down, gates_sorted)


def _shared_ffn_kernel(x_ref, wg_ref, wu_ref, wd_ref, y_ref, res_ref, g2_ref, o_ref, *, tm, n_ctx):
    g2 = _row_select(pl.program_id(0) * tm, tm, n_ctx, g2_ref)

    @pl.when(pl.program_id(1) == 0)
    def _():
        y = jnp.concatenate(_unpack_words(_load_word_rows(y_ref)), axis=1)
        o_ref[...] = res_ref[...] + g2 * y

    x = x_ref[...]
    act = (_silu(jnp.dot(x, wg_ref[...], preferred_element_type=F32))
           * jnp.dot(x, wu_ref[...], preferred_element_type=F32)).astype(BF16)
    o_ref[...] += g2 * jnp.dot(act, wd_ref[...], preferred_element_type=F32)


def shared_ffn_combine(hn, w_gate, w_up, w_down, y_routed, resid, gate2, n_ctx):
    t, d = hn.shape
    df = w_gate.shape[1]
    tm = _pick(t, (256, 128))
    tf = df
    row = lambda: pl.BlockSpec((tm, d), lambda i, f: (i, 0))
    return pl.pallas_call(
        functools.partial(_shared_ffn_kernel, tm=tm, n_ctx=n_ctx), grid=(t // tm, df // tf),
        in_specs=[row(), pl.BlockSpec((d, tf), lambda i, f: (0, f)), pl.BlockSpec((d, tf), lambda i, f: (0, f)),
                  pl.BlockSpec((tf, d), lambda i, f: (f, 0)),
                  pl.BlockSpec((tm,) + y_routed.shape[1:], lambda i, f: (i, 0, 0)), row(),
                  pl.BlockSpec((2, d), lambda i, f: (0, 0))],
        out_specs=row(),
        out_shape=jax.ShapeDtypeStruct((t, d), F32),
        compiler_params=_cparams("parallel", "arbitrary"),
        name="shared_ffn_combine",
    )(hn, w_gate, w_up, w_down, y_routed, resid, gate2)


def _group_sort_plan(grp, tm):
    t = grp.shape[0]
    n_tiles = (t + N_GROUPS * (tm - 1)) // tm
    n_rows = n_tiles * tm
    onehot = (grp[:, None] == jnp.arange(N_GROUPS, dtype=jnp.int32)[None, :]).astype(jnp.int32)
    rank = jnp.sum((jnp.cumsum(onehot, axis=0) - onehot) * onehot, axis=1)
    counts = jnp.sum(onehot, axis=0)
    padded = (counts + tm - 1) // tm * tm
    ends = jnp.cumsum(padded)
    dest = (ends - padded)[grp] + rank
    src = (jnp.arange(n_rows, dtype=jnp.int32) % t).at[dest].set(jnp.arange(t, dtype=jnp.int32))
    row_valid = jnp.zeros((n_rows,), jnp.int32).at[dest].set(1)
    tile_start = jnp.arange(n_tiles, dtype=jnp.int32) * tm
    tile_group = jnp.minimum(jnp.sum((tile_start[:, None] >= ends[None, :]).astype(jnp.int32), axis=1),
                             N_GROUPS - 1)
    tile_valid = (tile_start < ends[-1]).astype(jnp.int32)
    return dest, src, row_valid, tile_group, tile_valid


def moe_layer(x, hn, hn_words, logits_t, router_b, p, gate2, n_ctx):
    t = hn.shape[0]
    n_e = logits_t.shape[0]
    epg = n_e // N_GROUPS
    tm = _pick(t, (640, 512, 256, 128))
    gates_t, grp = gating(logits_t, router_b)
    dest, src, row_valid, tile_group, tile_valid = _group_sort_plan(grp, tm)
    gate_cols = grp[:, None] * epg + jnp.arange(epg, dtype=jnp.int32)[None, :]
    gates_tok = jnp.take_along_axis(gates_t.T, gate_cols, axis=1)
    gates_sorted = jnp.where(row_valid[:, None] > 0, gates_tok[src], 0.0)
    xs = row_gather(hn_words, src, tm)
    ys = group_ffn(xs, p["w_gate"], p["w_up"], p["w_down"], gates_sorted, tile_group, tile_valid, tm)
    y_tok = row_gather(ys, dest, tm)
    return shared_ffn_combine(hn, p["ws_gate"], p["ws_up"], p["ws_down"], y_tok, x, gate2, n_ctx)


def kernel(x, c, ctx, c_ctx, ada_down, ada_up, ada_bias, norm1_g, norm2_g, final_norm_g, mla_w_in, mla_q_norm, mla_kv_norm, mla_w_uq, mla_w_ukv, mla_w_o, ssm_w_in, ssm_conv_w, ssm_conv_b, ssm_dt_bias, ssm_a_log, ssm_d, ssm_norm_g, ssm_w_out, router_w, router_b, moe_w_gate, moe_w_up, moe_w_down, shared_w_gate, shared_w_up, shared_w_down):
    assert x.shape[0] == 1 and ctx.shape[0] == 1
    depth = ada_down.shape[0]
    n_lat, d = x.shape[1], x.shape[2]
    n_ctx = ctx.shape[1]
    xs = jnp.concatenate([ctx[0], x[0]], axis=0).astype(F32)

    cvecs = jnp.zeros((8, d), F32).at[0].set(c_ctx).at[1].set(c[0])
    mods = ada_modulation(cvecs, ada_down, ada_up, ada_bias)[:, :2]
    mods = mods.reshape(depth, 2, N_MOD, d)
    cc, ss = _rope_tables(n_ctx, n_lat)
    router_wt = router_w.T

    for i in range(depth):
        sh1, sc1, g1, sh2, sc2, g2 = (mods[i, :, k] for k in range(N_MOD))
        j = i // 2
        hn = norm_modulate(xs, norm1_g[i], sh1, sc1, n_ctx)
        if i % 2 == 0:
            p = prep_mla_weights(mla_w_in[j], mla_q_norm[j], mla_kv_norm[j], mla_w_uq[j], mla_w_ukv[j], mla_w_o[j])
            xs = mla_layer(xs, hn, p, g1, n_ctx, cc, ss)
        else:
            p = prep_mamba_weights(ssm_w_in[j], ssm_conv_w[j], ssm_conv_b[j], ssm_dt_bias[j], ssm_a_log[j],
                                   ssm_d[j], ssm_norm_g[j], ssm_w_out[j])
            xs = mamba_layer(xs, hn, p, g1, n_ctx)
        hn2, hn2_words, logits_t = norm_modulate(xs, norm2_g[i], sh2, sc2, n_ctx, router_wt=router_wt)
        pm = dict(w_gate=moe_w_gate[i].astype(BF16), w_up=moe_w_up[i].astype(BF16),
                  w_down=moe_w_down[i].astype(BF16), ws_gate=shared_w_gate[i].astype(BF16),
                  ws_up=shared_w_up[i].astype(BF16), ws_down=shared_w_down[i].astype(BF16))
        xs = moe_layer(xs, hn2, hn2_words, logits_t, router_b, pm, g2, n_ctx)
    return final_norm(xs, final_norm_g, n_ctx)[None]
```

```python
import functools
import math

import jax
import jax.numpy as jnp
from jax import lax
from jax.experimental import pallas as pl
from jax.experimental.pallas import tpu as pltpu

F32 = jnp.float32
BF16 = jnp.bfloat16

GRID_W = 64
ROPE_THETA = 10000.0
NORM_EPS = 1e-6
N_MOD = 6
QK_NOPE = 128
QK_ROPE = 64
V_HEAD = 128
SSM_HEAD_DIM = 64
SSM_GROUPS = 8
D_STATE = 128
SSM_CHUNK = 128
N_GROUPS = 4
TOP_K = 2

LANE = 128
VMEM_LIMIT_BYTES = 56 * 1024 * 1024


def _cparams(*sem):
    return pltpu.CompilerParams(dimension_semantics=sem, vmem_limit_bytes=VMEM_LIMIT_BYTES)


def _pick(n, prefs):
    for p in prefs:
        if p <= n and n % p == 0:
            return p
    return n


def _row_select(row0, tm, n_ctx, ref):
    if n_ctx % tm == 0:
        return ref[pl.ds((row0 >= n_ctx).astype(jnp.int32), 1), :]
    row = row0 + lax.broadcasted_iota(jnp.int32, (tm, 1), 0)
    return jnp.where(row < n_ctx, ref[0:1, :], ref[1:2, :])


def _silu(v):
    return v * (1.0 / (1.0 + jnp.exp(-v)))


def _ada_kernel(c_ref, wd_ref, wu_ref, b_ref, o_ref):
    cv = c_ref[...]
    h = jnp.dot(_silu(cv), wd_ref[...], preferred_element_type=F32, precision=lax.Precision.HIGHEST)
    o_ref[...] = jnp.dot(h, wu_ref[...], preferred_element_type=F32,
                         precision=lax.Precision.HIGHEST) + b_ref[...]


def ada_modulation(cvecs, w_down, w_up, bias):
    depth, d, r = w_down.shape
    n = w_up.shape[-1]
    tn = _pick(n, (2048, 1024, 512, 256, 128))
    return pl.pallas_call(
        _ada_kernel,
        grid=(depth, n // tn),
        in_specs=[pl.BlockSpec((8, d), lambda l, j: (0, 0)),
                  pl.BlockSpec((None, d, r), lambda l, j: (l, 0, 0)),
                  pl.BlockSpec((None, r, tn), lambda l, j: (l, 0, j)),
                  pl.BlockSpec((None, 1, tn), lambda l, j: (l, 0, j))],
        out_specs=pl.BlockSpec((None, 8, tn), lambda l, j: (l, 0, j)),
        out_shape=jax.ShapeDtypeStruct((depth, 8, n), F32),
        compiler_params=_cparams("parallel", "parallel"),
    )(cvecs, w_down, w_up, bias.reshape(depth, 1, n))


def _norm_mod_kernel(x_ref, g_ref, sh_ref, sc_ref, o_ref, *, tm, n_ctx):
    x = x_ref[...]
    y = x * lax.rsqrt(jnp.mean(x * x, axis=-1, keepdims=True) + NORM_EPS) * g_ref[...]
    row0 = pl.program_id(0) * tm
    sh = _row_select(row0, tm, n_ctx, sh_ref)
    sc = _row_select(row0, tm, n_ctx, sc_ref)
    o_ref[...] = (y * (1.0 + sc) + sh).astype(o_ref.dtype)


def _norm_mod_router_kernel(x_ref, g_ref, sh_ref, sc_ref, rw_ref, o_ref, ow_ref, lt_ref, *, tm, n_ctx):
    x = x_ref[...]
    y = x * lax.rsqrt(jnp.mean(x * x, axis=-1, keepdims=True) + NORM_EPS) * g_ref[...]
    row0 = pl.program_id(0) * tm
    sh = _row_select(row0, tm, n_ctx, sh_ref)
    sc = _row_select(row0, tm, n_ctx, sc_ref)
    t = y * (1.0 + sc) + sh
    o_ref[...] = t.astype(o_ref.dtype)
    _store_word_rows(ow_ref, _pack_words(t))
    lt_ref[...] = lax.dot_general(rw_ref[...], t, (((1,), (1,)), ((), ())),
                                  preferred_element_type=F32, precision=lax.Precision.HIGHEST)


def norm_modulate(x, g, shift2, scale2, n_ctx, router_wt=None):
    t, d = x.shape
    tm = _pick(t, (256, 128))
    vec = pl.BlockSpec((1, d), lambda i: (0, 0))
    two = pl.BlockSpec((2, d), lambda i: (0, 0))
    xs = pl.BlockSpec((tm, d), lambda i: (i, 0))
    if router_wt is None:
        return pl.pallas_call(
            functools.partial(_norm_mod_kernel, tm=tm, n_ctx=n_ctx),
            grid=(t // tm,), in_specs=[xs, vec, two, two], out_specs=xs,
            out_shape=jax.ShapeDtypeStruct((t, d), BF16),
            compiler_params=_cparams("parallel"),
        )(x, g.reshape(1, d), shift2, scale2)
    e = router_wt.shape[0]
    row_words = (d // 2 // LANE, LANE)
    return pl.pallas_call(
        functools.partial(_norm_mod_router_kernel, tm=tm, n_ctx=n_ctx),
        grid=(t // tm,),
        in_specs=[xs, vec, two, two, pl.BlockSpec((e, d), lambda i: (0, 0))],
        out_specs=[xs, pl.BlockSpec((tm,) + row_words, lambda i: (i, 0, 0)), pl.BlockSpec((e, tm), lambda i: (0, i))],
        out_shape=[jax.ShapeDtypeStruct((t, d), BF16), jax.ShapeDtypeStruct((t,) + row_words, jnp.uint32),
                   jax.ShapeDtypeStruct((e, t), F32)],
        compiler_params=_cparams("parallel"),
        name="norm_modulate_router",
    )(x, g.reshape(1, d), shift2, scale2, router_wt)


def _final_norm_kernel(x_ref, g_ref, o_ref):
    x = x_ref[...]
    o_ref[...] = x * lax.rsqrt(jnp.mean(x * x, axis=-1, keepdims=True) + NORM_EPS) * g_ref[...]


def final_norm(x, g, row_start):
    t, d = x.shape
    tm = _pick(t, (256, 128))
    assert row_start % tm == 0
    off = row_start // tm
    return pl.pallas_call(
        _final_norm_kernel, grid=((t - row_start) // tm,),
        in_specs=[pl.BlockSpec((tm, d), lambda i: (i + off, 0)), pl.BlockSpec((1, d), lambda i: (0, 0))],
        out_specs=pl.BlockSpec((tm, d), lambda i: (i, 0)),
        out_shape=jax.ShapeDtypeStruct((t - row_start, d), F32),
        compiler_params=_cparams("parallel"),
    )(x, g.reshape(1, d))


def _mm_kernel(a_ref, b_ref, o_ref):
    o_ref[...] = jnp.dot(a_ref[...], b_ref[...], preferred_element_type=F32).astype(o_ref.dtype)


def _mm_tiles(m, k, n):
    tm = _pick(m, (1280, 1024, 640, 512, 256, 128))
    if k > 4096:
        tm = _pick(m, (640, 512, 256, 128))
    tn = _pick(n, (512, 256, 128))
    return tm, tn


def matmul(a, b, out_dtype):
    m, k = a.shape
    n = b.shape[1]
    tm, tn = _mm_tiles(m, k, n)
    return pl.pallas_call(
        _mm_kernel, grid=(m // tm, n // tn),
        in_specs=[pl.BlockSpec((tm, k), lambda i, j: (i, 0)), pl.BlockSpec((k, tn), lambda i, j: (0, j))],
        out_specs=pl.BlockSpec((tm, tn), lambda i, j: (i, j)),
        out_shape=jax.ShapeDtypeStruct((m, n), out_dtype),
        compiler_params=_cparams("parallel", "parallel"),
    )(a, b)


def _mm_res_kernel(a_ref, b_ref, r_ref, g_ref, o_ref, *, tm, n_ctx):
    acc = jnp.dot(a_ref[...], b_ref[...], preferred_element_type=F32)
    gate = _row_select(pl.program_id(0) * tm, tm, n_ctx, g_ref)
    o_ref[...] = r_ref[...] + gate * acc


def matmul_residual(a, b, resid, gate2, n_ctx):
    m, k = a.shape
    n = b.shape[1]
    tm, tn = _mm_tiles(m, k, n)
    return pl.pallas_call(
        functools.partial(_mm_res_kernel, tm=tm, n_ctx=n_ctx), grid=(m // tm, n // tn),
        in_specs=[pl.BlockSpec((tm, k), lambda i, j: (i, 0)), pl.BlockSpec((k, tn), lambda i, j: (0, j)),
                  pl.BlockSpec((tm, tn), lambda i, j: (i, j)), pl.BlockSpec((2, tn), lambda i, j: (0, j))],
        out_specs=pl.BlockSpec((tm, tn), lambda i, j: (i, j)),
        out_shape=jax.ShapeDtypeStruct((m, n), F32),
        compiler_params=_cparams("parallel", "parallel"),
    )(a, b, resid, gate2)


def _rope_tables(n_ctx, n_lat):
    rows = n_lat // GRID_W
    row = jnp.broadcast_to(jnp.arange(rows, dtype=F32)[:, None], (rows, GRID_W)).reshape(-1)
    col = jnp.broadcast_to(jnp.arange(GRID_W, dtype=F32)[None, :], (rows, GRID_W)).reshape(-1)
    n_freq = QK_ROPE // 4
    inv = ROPE_THETA ** (-jnp.arange(n_freq, dtype=F32) / n_freq)
    ang = jnp.concatenate([row[:, None] * inv, col[:, None] * inv], axis=-1)
    ang = jnp.concatenate([jnp.zeros((n_ctx, QK_ROPE // 2), F32), ang], axis=0)
    return jnp.tile(jnp.cos(ang), (1, 4)), jnp.tile(jnp.sin(ang), (1, 4))


def _rot_cols(w):
    h = QK_ROPE // 2
    return jnp.concatenate([-w[..., h:], w[..., :h]], axis=-1)


def _mla_lat_post_kernel(lat_ref, qg_ref, kg_ref, cc_ref, ss_ref, cq_ref, ckv_ref, kr_ref, *, q_rank, kv_rank):
    cq = lat_ref[:, :q_rank]
    cq_ref[...] = (cq * lax.rsqrt(jnp.mean(cq * cq, axis=-1, keepdims=True) + NORM_EPS)
                   * qg_ref[...]).astype(cq_ref.dtype)
    ckv = lat_ref[:, q_rank:q_rank + kv_rank]
    ckv_ref[...] = (ckv * lax.rsqrt(jnp.mean(ckv * ckv, axis=-1, keepdims=True) + NORM_EPS)
                    * kg_ref[...]).astype(ckv_ref.dtype)
    base = q_rank + kv_rank
    cc = cc_ref[...]
    ss = ss_ref[...]
    lo = lat_ref[:, base:base + LANE] * cc + lat_ref[:, base + LANE:base + 2 * LANE] * ss
    hi = lat_ref[:, base + 2 * LANE:base + 3 * LANE] * cc + lat_ref[:, base + 3 * LANE:base + 4 * LANE] * ss
    kr_ref[:, :LANE] = lo.astype(kr_ref.dtype)
    kr_ref[:, LANE:] = hi.astype(kr_ref.dtype)


def mla_lat_post(lat, q_norm, kv_norm, cc, ss):
    t = lat.shape[0]
    q_rank, kv_rank = q_norm.shape[0], kv_norm.shape[0]
    tm = _pick(t, (256, 128))
    row = lambda w: pl.BlockSpec((tm, w), lambda i: (i, 0))
    return pl.pallas_call(
        functools.partial(_mla_lat_post_kernel, q_rank=q_rank, kv_rank=kv_rank), grid=(t // tm,),
        in_specs=[row(lat.shape[1]), pl.BlockSpec((1, q_rank), lambda i: (0, 0)),
                  pl.BlockSpec((1, kv_rank), lambda i: (0, 0)), row(LANE), row(LANE)],
        out_specs=[row(q_rank), row(kv_rank), row(2 * LANE)],
        out_shape=[jax.ShapeDtypeStruct((t, q_rank), BF16), jax.ShapeDtypeStruct((t, kv_rank), BF16),
                   jax.ShapeDtypeStruct((t, 2 * LANE), BF16)],
        compiler_params=_cparams("parallel"),
    )(lat, q_norm.reshape(1, -1), kv_norm.reshape(1, -1), cc, ss)


def _q_proj_kernel(a_ref, b_ref, cc_ref, ss_ref, qn_ref, qr_ref):
    acc = jnp.dot(a_ref[...], b_ref[...], preferred_element_type=F32)
    qn_ref[...] = acc[:, :2 * LANE].astype(qn_ref.dtype)
    qr_ref[...] = (acc[:, 2 * LANE:3 * LANE] * cc_ref[...]
                   + acc[:, 3 * LANE:4 * LANE] * ss_ref[...]).astype(qr_ref.dtype)


def q_proj(cqn, w_pairs, cc, ss):
    t, k = cqn.shape
    n_pairs = w_pairs.shape[1] // (4 * LANE)
    tm = _pick(t, (1280, 1024, 640, 512, 256, 128))
    return pl.pallas_call(
        _q_proj_kernel, grid=(t // tm, n_pairs),
        in_specs=[pl.BlockSpec((tm, k), lambda i, j: (i, 0)), pl.BlockSpec((k, 4 * LANE), lambda i, j: (0, j)),
                  pl.BlockSpec((tm, LANE), lambda i, j: (i, 0)), pl.BlockSpec((tm, LANE), lambda i, j: (i, 0))],
        out_specs=[pl.BlockSpec((tm, 2 * LANE), lambda i, j: (i, j)), pl.BlockSpec((tm, LANE), lambda i, j: (i, j))],
        out_shape=[jax.ShapeDtypeStruct((t, n_pairs * 2 * LANE), BF16),
                   jax.ShapeDtypeStruct((t, n_pairs * LANE), BF16)],
        compiler_params=_cparams("parallel", "parallel"),
    )(cqn, w_pairs, cc, ss)


def _attn_kernel(qn_ref, qr_ref, kn_ref, kr_ref, vt_ref, o_ref, s_scr, acc_scr, *, tq, tk, unroll, n_ctx, n_all):
    q = jnp.concatenate([qn_ref[...], qr_ref[...]], axis=1)
    qt = q.astype(F32).T.astype(BF16)

    def scores(slot, off, size):
        k = jnp.concatenate([kn_ref[pl.ds(off, size), :], kr_ref[pl.ds(off, size), :]], axis=1)
        s = jnp.dot(k, qt, preferred_element_type=F32)
        s_scr[slot, 0:size, :] = s
        return jnp.max(s, axis=0, keepdims=True)

    def update(slot, mx, off, size, carry):
        m, l = carry
        m_new = jnp.maximum(m, mx)
        alpha = jnp.exp2(m - m_new)
        p = jnp.exp2(s_scr[slot, 0:size, :] - m_new)
        l = alpha * l + jnp.sum(p, axis=0, keepdims=True)
        acc_scr[...] = alpha * acc_scr[...] + jnp.dot(vt_ref[:, pl.ds(off, size)], p.astype(BF16),
                                                      preferred_element_type=F32)
        return m_new, l

    def finish(carry):
        o_ref[...] = (acc_scr[...] / carry[1]).T.astype(o_ref.dtype)

    acc_scr[...] = jnp.zeros(acc_scr.shape, F32)
    init = (jnp.full((1, tq), -jnp.inf, F32), jnp.zeros((1, tq), F32))
    is_ctx = pl.program_id(1) * tq < n_ctx

    @pl.when(is_ctx)
    def _():
        finish(update(0, scores(0, 0, n_ctx), 0, n_ctx, init))

    @pl.when(jnp.logical_not(is_ctx))
    def _():
        n_chunks = n_all // tk

        def run(base, count, carry, lookahead):
            mx, st = carry[0], carry[1:]
            for u in range(count):
                off = base + u * tk
                more = u + 1 < count or lookahead
                mx_next = scores((u + 1) % 2, off + tk, tk) if more else None
                st = update(u % 2, mx, off, tk, st)
                mx = mx_next
            return (mx,) + st

        def body(i, carry):
            return run(pl.multiple_of(i * (unroll * tk), tk), unroll, carry, True)

        n_trips = (n_chunks - 1) // unroll
        carry = lax.fori_loop(0, n_trips, body, (scores(0, 0, tk),) + init)
        finish(run(n_trips * unroll * tk, n_chunks - n_trips * unroll, carry, False)[1:])


def attention(qn, qr, kn, kr2, vt, n_ctx):
    t = qn.shape[0]
    heads = qn.shape[1] // QK_NOPE
    tq = _pick(n_ctx, (256, 128))
    tk = _pick(t, (3328, 1280, 512, 256, 128))
    assert n_ctx % tq == 0 and t % tq == 0 and n_ctx <= tk
    return pl.pallas_call(
        functools.partial(_attn_kernel, tq=tq, tk=tk, unroll=12, n_ctx=n_ctx, n_all=t),
        grid=(heads, t // tq),
        in_specs=[pl.BlockSpec((tq, LANE), lambda h, i: (i, h)),
                  pl.BlockSpec((tq, LANE), lambda h, i: (i, h // 2)),
                  pl.BlockSpec((t, LANE), lambda h, i: (0, h)),
                  pl.BlockSpec((t, LANE), lambda h, i: (0, h % 2)),
                  pl.BlockSpec((V_HEAD, t), lambda h, i: (h, 0))],
        out_specs=pl.BlockSpec((tq, LANE), lambda h, i: (i, h)),
        out_shape=jax.ShapeDtypeStruct((t, heads * V_HEAD), BF16),
        scratch_shapes=[pltpu.VMEM((2, tk, tq), F32), pltpu.VMEM((V_HEAD, tq), F32)],
        compiler_params=_cparams("parallel", "arbitrary"),
        name="mla_attention",
    )(qn, qr, kn, kr2, vt)


def _mm_nt_kernel(a_ref, b_ref, o_ref):
    o_ref[...] = lax.dot_general(a_ref[...], b_ref[...], (((1,), (1,)), ((), ())),
                                 preferred_element_type=F32).astype(o_ref.dtype)


def matmul_nt(a, b, out_dtype):
    m, k = a.shape
    n = b.shape[0]
    tm = _pick(m, (1024, 512, 256, 128))
    tn = _pick(n, (1280, 1024, 640, 512, 256, 128))
    return pl.pallas_call(
        _mm_nt_kernel, grid=(m // tm, n // tn),
        in_specs=[pl.BlockSpec((tm, k), lambda i, j: (i, 0)), pl.BlockSpec((tn, k), lambda i, j: (j, 0))],
        out_specs=pl.BlockSpec((tm, tn), lambda i, j: (i, j)),
        out_shape=jax.ShapeDtypeStruct((m, n), out_dtype),
        compiler_params=_cparams("parallel", "parallel"),
        name="matmul_nt",
    )(a, b)


def mla_layer(x, hn, p, gate2, n_ctx, cc, ss):
    lat = matmul(hn, p["w_in"], F32)
    cqn, ckvn, kr2 = mla_lat_post(lat, p["q_norm"], p["kv_norm"], cc, ss)
    qn, qr = q_proj(cqn, p["w_uq"], cc, ss)
    kn = matmul(ckvn, p["w_uk"], BF16)
    vt = matmul_nt(p["w_uv_t"], ckvn, BF16)
    o = attention(qn, qr, kn, kr2, vt, n_ctx)
    return matmul_residual(o, p["w_o"], x, gate2, n_ctx)


def prep_mla_weights(w_in, q_norm, kv_norm, w_uq, w_ukv, w_o):
    d = w_in.shape[0]
    heads = d // 128
    q_rank, kv_rank = q_norm.shape[0], kv_norm.shape[0]
    z64 = jnp.zeros((d, QK_ROPE), F32)
    wkr = w_in[:, q_rank + kv_rank:]
    wkr_rot = _rot_cols(wkr)
    w_in_p = jnp.concatenate([w_in[:, :q_rank + kv_rank], wkr, z64, wkr_rot, z64, z64, wkr, z64, wkr_rot], axis=1)
    scale = (QK_NOPE + QK_ROPE) ** -0.5 * math.log2(math.e)
    wq = (w_uq * scale).reshape(q_rank, heads, QK_NOPE + QK_ROPE)
    nope = wq[..., :QK_NOPE].reshape(q_rank, heads // 2, 2 * QK_NOPE)
    rope = wq[..., QK_NOPE:]
    rope_a = rope.reshape(q_rank, heads // 2, 2 * QK_ROPE)
    rope_b = _rot_cols(rope).reshape(q_rank, heads // 2, 2 * QK_ROPE)
    w_uq_p = jnp.concatenate([nope, rope_a, rope_b], axis=-1).reshape(q_rank, heads // 2 * 4 * LANE)
    wkv = w_ukv.reshape(kv_rank, heads, QK_NOPE + V_HEAD)
    w_uk = wkv[..., :QK_NOPE].reshape(kv_rank, heads * QK_NOPE)
    w_uv_t = wkv[..., QK_NOPE:].reshape(kv_rank, heads * V_HEAD).T
    return dict(w_in=w_in_p.astype(BF16), q_norm=q_norm, kv_norm=kv_norm, w_uq=w_uq_p.astype(BF16),
                w_uk=w_uk.astype(BF16), w_uv_t=w_uv_t.astype(BF16), w_o=w_o.astype(BF16))


def _conv_silu_kernel(prev_ref, cur_ref, next_ref, w_ref, b_ref, o_ref, win_ref, *, tm, n_ctx, n_tiles):
    i = pl.program_id(0)
    row0 = i * tm
    has_prev = jnp.logical_and(i > 0, row0 != n_ctx)
    has_next = jnp.logical_and(i < n_tiles - 1, row0 + tm != n_ctx)
    h = CONV_HALO
    win_ref[0:h, :] = jnp.where(has_prev, prev_ref[...].astype(F32), 0.0)
    win_ref[h:h + tm, :] = cur_ref[...].astype(F32)
    win_ref[h + tm:2 * h + tm, :] = jnp.where(has_next, next_ref[...].astype(F32), 0.0)
    pad = w_ref.shape[0] // 2
    acc = jnp.zeros(o_ref.shape, F32) + b_ref[...]
    win = win_ref[...]
    n = tm + 2 * h
    for k in range(w_ref.shape[0]):
        rolled = win if k == pad else pltpu.roll(win, (pad - k) % n, 0)
        acc = acc + w_ref[k:k + 1, :] * rolled[h:h + tm, :]
    o_ref[...] = _silu(acc).astype(o_ref.dtype)


CONV_HALO = 16


def conv_silu(u, w, b, n_ctx):
    t, c = u.shape
    tm = _pick(n_ctx, (256, 128))
    tn = _pick(c, (2048, 1024, 512, 256, 128))
    n_tiles = t // tm
    rh = tm // CONV_HALO
    nbh = t // CONV_HALO
    assert w.shape[0] // 2 <= CONV_HALO
    return pl.pallas_call(
        functools.partial(_conv_silu_kernel, tm=tm, n_ctx=n_ctx, n_tiles=n_tiles),
        grid=(n_tiles, c // tn),
        in_specs=[pl.BlockSpec((CONV_HALO, tn), lambda i, j: (jnp.maximum(i * rh - 1, 0), j)),
                  pl.BlockSpec((tm, tn), lambda i, j: (i, j)),
                  pl.BlockSpec((CONV_HALO, tn), lambda i, j: (jnp.minimum((i + 1) * rh, nbh - 1), j)),
                  pl.BlockSpec((w.shape[0], tn), lambda i, j: (0, j)),
                  pl.BlockSpec((1, tn), lambda i, j: (0, j))],
        out_specs=pl.BlockSpec((tm, tn), lambda i, j: (i, j)),
        out_shape=jax.ShapeDtypeStruct((t, c), BF16),
        scratch_shapes=[pltpu.VMEM((tm + 2 * CONV_HALO, tn), F32)],
        compiler_params=_cparams("parallel", "parallel"),
        name="conv_silu",
    )(u, u, u, w, b.reshape(1, c))


def _dt_prep_kernel(raw_ref, bias_ref, a_ref, dtf_ref, cf_ref, dtb_ref, cb_ref, *, nh, q):
    raw = raw_ref[...]
    r = lax.broadcasted_iota(jnp.int32, (q, q), 0)
    c = lax.broadcasted_iota(jnp.int32, (q, q), 1)
    for d, (dt_ref, c_ref) in enumerate(((dtf_ref, cf_ref), (dtb_ref, cb_ref))):
        v = raw[:, d * nh:(d + 1) * nh] + bias_ref[d:d + 1, :]
        dt = jnp.maximum(v, 0.0) + jnp.log1p(jnp.exp(-jnp.abs(v)))
        da = dt * a_ref[d:d + 1, :]
        tri = jnp.where(r >= c, 1.0, 0.0) if d == 0 else jnp.where(r <= c, 1.0, 0.0)
        cum = jnp.dot(tri.astype(F32), da, preferred_element_type=F32, precision=lax.Precision.HIGHEST)
        dt_ref[...] = dt.T
        c_ref[...] = cum.T


def dt_prep(dt_raw, dt_bias, a_log):
    t = dt_raw.shape[0]
    nh = dt_raw.shape[1] // 2
    q = SSM_CHUNK
    a = -jnp.exp(a_log.astype(F32))
    out = jax.ShapeDtypeStruct((nh, t), F32)
    ob = pl.BlockSpec((nh, q), lambda i: (0, i))
    return pl.pallas_call(
        functools.partial(_dt_prep_kernel, nh=nh, q=q), grid=(t // q,),
        in_specs=[pl.BlockSpec((q, 2 * nh), lambda i: (i, 0)), pl.BlockSpec((2, nh), lambda i: (0, 0)),
                  pl.BlockSpec((2, nh), lambda i: (0, 0))],
        out_specs=[ob, ob, ob, ob], out_shape=[out, out, out, out],
        compiler_params=_cparams("parallel"),
    )(dt_raw, dt_bias.astype(F32), a)


def _ssd_kernel(x_ref, b_ref, c_ref, dt_ref, cum_ref, y_ref, s_ref, *, hg, reverse, n_sub):
    q = SSM_CHUNK
    n_pairs = hg // 2

    @pl.when(pl.program_id(1) == 0)
    def _():
        s_ref[...] = jnp.zeros(s_ref.shape, s_ref.dtype)

    ii = lax.broadcasted_iota(jnp.int32, (q, q), 0)
    jj = lax.broadcasted_iota(jnp.int32, (q, q), 1)
    mask = (jj >= ii) if reverse else (ii >= jj)
    lane_lo = lax.broadcasted_iota(jnp.int32, (q, 2 * SSM_HEAD_DIM), 1) < SSM_HEAD_DIM
    last = 0 if reverse else q - 1

    for sub in (range(n_sub - 1, -1, -1) if reverse else range(n_sub)):
        r0 = sub * q
        bsub = b_ref[r0:r0 + q, :]
        csub = c_ref[r0:r0 + q, :]
        cmat = csub.astype(F32)
        cb = lax.dot_general(csub, bsub, (((1,), (1,)), ((), ())), preferred_element_type=F32)
        bt = bsub.astype(F32).T
        for pr in range(n_pairs):
            xp = x_ref[r0:r0 + q, pr * LANE:(pr + 1) * LANE]
            sp = s_ref[:, pr * LANE:(pr + 1) * LANE]
            rhs = jnp.concatenate([xp, sp.astype(BF16)], axis=0)
            ys, ss, decs = [], [], []
            for hh in range(2):
                h = pr * 2 + hh
                cum_h = cum_ref[h:h + 1, r0:r0 + q]
                crow = jnp.broadcast_to(cum_h, (q, q))
                ccol = crow.T
                dtrow = dt_ref[h:h + 1, r0:r0 + q]
                seg = jnp.where(mask, ccol - crow, -jnp.inf)
                mm = cb * jnp.exp(seg) * dtrow
                coff = cmat * jnp.exp(ccol)
                lhs = jnp.concatenate([mm.astype(BF16), coff.astype(BF16)], axis=1)
                ys.append(jnp.dot(lhs, rhs, preferred_element_type=F32))
                tot = cum_h[:, last:last + 1]
                wrow = jnp.exp(tot - cum_h) * dtrow
                ss.append(jnp.dot((bt * wrow).astype(BF16), xp, preferred_element_type=F32))
                decs.append(jnp.exp(tot))
            y_ref[r0:r0 + q, pr * LANE:(pr + 1) * LANE] = jnp.where(lane_lo, ys[0], ys[1]).astype(y_ref.dtype)
            dec = jnp.where(lane_lo[0:1, :], decs[0], decs[1])
            s_ref[:, pr * LANE:(pr + 1) * LANE] = sp * dec + jnp.where(lane_lo, ss[0], ss[1])


def ssd_scan(xbc, dt_t, cum_t, d_inner, n_ctx, reverse):
    t = xbc.shape[0]
    q = SSM_CHUNK
    nh = d_inner // SSM_HEAD_DIM
    hg = nh // SSM_GROUPS
    gw = hg * SSM_HEAD_DIM
    assert gw % LANE == 0 and hg % 8 == 0
    if reverse:
        n_sub = 2 if (t // q) % 2 == 0 and (n_ctx // q) % 2 == 0 else 1
    else:
        n_sub = _pick(t // q, (5, 2, 1))
    rows = n_sub * q
    nc = t // rows
    ncc = n_ctx // rows
    xoff = d_inner // LANE
    if reverse:
        cidx = lambda s: jnp.where(s < ncc, ncc - 1 - s, nc - 1 + ncc - s)
    else:
        cidx = lambda s: s
    return pl.pallas_call(
        functools.partial(_ssd_kernel, hg=hg, reverse=reverse, n_sub=n_sub),
        grid=(SSM_GROUPS, nc),
        in_specs=[pl.BlockSpec((rows, gw), lambda g, s: (cidx(s), g)),
                  pl.BlockSpec((rows, D_STATE), lambda g, s: (cidx(s), xoff + g)),
                  pl.BlockSpec((rows, D_STATE), lambda g, s: (cidx(s), xoff + SSM_GROUPS + g)),
                  pl.BlockSpec((hg, rows), lambda g, s: (g, cidx(s))),
                  pl.BlockSpec((hg, rows), lambda g, s: (g, cidx(s)))],
        out_specs=pl.BlockSpec((rows, gw), lambda g, s: (cidx(s), g)),
        out_shape=jax.ShapeDtypeStruct((t, d_inner), BF16),
        scratch_shapes=[pltpu.VMEM((D_STATE, gw), F32)],
        compiler_params=_cparams("parallel", "arbitrary"),
        name="ssd_scan",
    )(xbc, xbc, xbc, dt_t, cum_t)


def _ssm_finish_kernel(yf_ref, yb_ref, x_ref, z_ref, d_ref, g_ref, o_ref, *, n_groups):
    z = z_ref[...].astype(F32)
    y = (yf_ref[...].astype(F32) + yb_ref[...].astype(F32) + d_ref[...] * x_ref[...].astype(F32)) * _silu(z)
    gw = y.shape[1] // n_groups
    for g in range(n_groups):
        yg = y[:, g * gw:(g + 1) * gw]
        yn = yg * lax.rsqrt(jnp.mean(yg * yg, axis=-1, keepdims=True) + NORM_EPS)
        o_ref[:, g * gw:(g + 1) * gw] = (yn * g_ref[:, g * gw:(g + 1) * gw]).astype(o_ref.dtype)


def ssm_finish(yf, yb, xbc, z, d_exp, norm_g):
    t, di = yf.shape
    tm = _pick(t, (256, 128))
    row = pl.BlockSpec((tm, di), lambda i: (i, 0))
    vec = pl.BlockSpec((1, di), lambda i: (0, 0))
    return pl.pallas_call(
        functools.partial(_ssm_finish_kernel, n_groups=SSM_GROUPS), grid=(t // tm,),
        in_specs=[row, row, row, row, vec, vec], out_specs=row,
        out_shape=jax.ShapeDtypeStruct((t, di), BF16),
        compiler_params=_cparams("parallel"),
    )(yf, yb, xbc, z, d_exp, norm_g.reshape(1, di))


def mamba_layer(x, hn, p, gate2, n_ctx):
    di = p["d_inner"]
    z = matmul(hn, p["w_z"], BF16)
    xbc_raw = matmul(hn, p["w_xbc"], BF16)
    dt_raw = matmul(hn, p["w_dt"], F32)
    xbc = conv_silu(xbc_raw, p["conv_w"], p["conv_b"], n_ctx)
    dtf, cf, dtb, cb = dt_prep(dt_raw, p["dt_bias"], p["a_log"])
    yf = ssd_scan(xbc, dtf, cf, di, n_ctx, reverse=False)
    yb = ssd_scan(xbc, dtb, cb, di, n_ctx, reverse=True)
    yn = ssm_finish(yf, yb, xbc, z, p["d_exp"], p["norm_g"])
    return matmul_residual(yn, p["w_out"], x, gate2, n_ctx)


def prep_mamba_weights(w_in, conv_w, conv_b, dt_bias, a_log, d_skip, norm_g, w_out):
    di = norm_g.shape[0]
    conv_dim = conv_w.shape[1]
    nh = d_skip.shape[0]
    w_dt = w_in[:, di + conv_dim:]
    pad = (-w_dt.shape[1]) % LANE
    if pad:
        w_dt = jnp.concatenate([w_dt, jnp.zeros((w_in.shape[0], pad), F32)], axis=1)
    return dict(d_inner=di, w_z=w_in[:, :di].astype(BF16), w_xbc=w_in[:, di:di + conv_dim].astype(BF16),
                w_dt=w_dt.astype(BF16), conv_w=conv_w, conv_b=conv_b, dt_bias=dt_bias, a_log=a_log,
                d_exp=jnp.repeat(d_skip.astype(F32), SSM_HEAD_DIM).reshape(1, di), norm_g=norm_g,
                w_out=w_out.astype(BF16), n_heads=nh)


def _gating_kernel(lt_ref, b_ref, g_ref, grp_ref, *, n_experts):
    epg = n_experts // N_GROUPS
    s = [1.0 / (1.0 + jnp.exp(-lt_ref[e])) for e in range(n_experts)]
    sel = [s[e] + b_ref[e] for e in range(n_experts)]
    best_val, best_idx = None, None
    for g in range(N_GROUPS):
        mem = sel[g * epg:(g + 1) * epg]
        gs = None
        for a in range(epg):
            for b in range(a + 1, epg):
                pair = mem[a] + mem[b]
                gs = pair if gs is None else jnp.maximum(gs, pair)
        if g == 0:
            best_val, best_idx = gs, jnp.zeros(gs.shape, jnp.int32)
        else:
            better = gs > best_val
            best_val = jnp.where(better, gs, best_val)
            best_idx = jnp.where(better, g, best_idx)
    chosen, denom = [], None
    for e in range(n_experts):
        g = e // epg
        rank = jnp.zeros(best_idx.shape, jnp.int32)
        for m in range(g * epg, (g + 1) * epg):
            if m == e:
                continue
            ahead = (sel[m] > sel[e]) if m > e else (sel[m] >= sel[e])
            rank = rank + ahead.astype(jnp.int32)
        pick = jnp.logical_and(best_idx == g, rank < TOP_K)
        w = jnp.where(pick, s[e], 0.0)
        chosen.append(w)
        denom = w if denom is None else denom + w
    inv = 1.0 / denom
    for e in range(n_experts):
        g_ref[e] = chosen[e] * inv
    grp_ref[...] = best_idx


def gating(logits_t, router_b):
    e, t = logits_t.shape
    r = t // LANE
    lt3 = logits_t.reshape(e, r, LANE)
    gates, grp = pl.pallas_call(
        functools.partial(_gating_kernel, n_experts=e), grid=(1,),
        in_specs=[pl.BlockSpec((e, r, LANE), lambda i: (0, 0, 0)),
                  pl.BlockSpec(memory_space=pltpu.SMEM)],
        out_specs=[pl.BlockSpec((e, r, LANE), lambda i: (0, 0, 0)), pl.BlockSpec((r, LANE), lambda i: (0, 0))],
        out_shape=[jax.ShapeDtypeStruct((e, r, LANE), F32), jax.ShapeDtypeStruct((r, LANE), jnp.int32)],
        compiler_params=_cparams("arbitrary"),
        name="moe_gating",
    )(lt3, router_b.astype(F32))
    return gates.reshape(e, t), grp.reshape(t)


HI_HALF = 0xFFFF0000
GATHER_UNROLL = 8


def _pack_words(v):
    w = v.shape[1] // 2
    lo = pltpu.bitcast(v[:, :w].astype(jnp.bfloat16).astype(F32), jnp.uint32) >> 16
    hi = pltpu.bitcast(v[:, w:].astype(jnp.bfloat16).astype(F32), jnp.uint32) & jnp.uint32(HI_HALF)
    return lo | hi


def _unpack_words(words):
    return (pltpu.bitcast(words << 16, F32), pltpu.bitcast(words & jnp.uint32(HI_HALF), F32))


def _store_word_rows(ref3, words):
    for k in range(ref3.shape[1]):
        ref3[:, k, :] = words[:, k * LANE:(k + 1) * LANE]


def _load_word_rows(ref3):
    return jnp.concatenate([ref3[:, k, :] for k in range(ref3.shape[1])], axis=1)


def _row_gather_kernel(idx_ref, src_hbm, o_ref, sem, *, blk):
    def row_copy(q, src_row):
        return pltpu.make_async_copy(src_hbm.at[src_row], o_ref.at[q], sem)

    def issue(q, carry):
        row_copy(q, idx_ref[0, 0, q]).start()
        return carry

    def drain(q, carry):
        row_copy(q, 0).wait()
        return carry

    lax.fori_loop(0, blk, issue, 0, unroll=GATHER_UNROLL)
    lax.fori_loop(0, blk, drain, 0, unroll=GATHER_UNROLL)


def row_gather(src3, idx, blk):
    m = idx.shape[0]
    row_words = src3.shape[1:]
    assert m % blk == 0
    return pl.pallas_call(
        functools.partial(_row_gather_kernel, blk=blk), grid=(m // blk,),
        in_specs=[pl.BlockSpec((1, 1, blk), lambda b: (b, 0, 0), memory_space=pltpu.SMEM),
                  pl.BlockSpec(memory_space=pl.ANY)],
        out_specs=pl.BlockSpec((blk,) + row_words, lambda b: (b, 0, 0)),
        out_shape=jax.ShapeDtypeStruct((m,) + row_words, jnp.uint32),
        scratch_shapes=[pltpu.SemaphoreType.DMA(())],
        compiler_params=_cparams("arbitrary"),
        name="row_gather",
    )(idx.reshape(m // blk, 1, blk), src3)


def _group_ffn_kernel(tg_ref, tv_ref, x_ref, wg_ref, wu_ref, wd_ref, gs_ref, o_ref, acc_ref, xb_ref):
    r = pl.program_id(0)
    j = pl.program_id(1)
    f = pl.program_id(2)
    first = jnp.logical_and(j == 0, f == 0)
    last = jnp.logical_and(j == pl.num_programs(1) - 1, f == pl.num_programs(2) - 1)

    @pl.when(first)
    def _():
        acc_ref[...] = jnp.zeros(acc_ref.shape, F32)
        lo, hi = _unpack_words(_load_word_rows(x_ref))
        w = lo.shape[1]
        xb_ref[:, :w] = lo.astype(BF16)
        xb_ref[:, w:] = hi.astype(BF16)

    @pl.when(tv_ref[r] > 0)
    def _():
        x = xb_ref[...]
        hgate = jnp.dot(x, wg_ref[...], preferred_element_type=F32)
        hup = jnp.dot(x, wu_ref[...], preferred_element_type=F32)
        gts = gs_ref[...]
        lane = lax.broadcasted_iota(jnp.int32, gts.shape, 1)
        gcol = jnp.sum(jnp.where(lane == j, gts, 0.0), axis=1, keepdims=True)
        act = (_silu(hgate) * hup * gcol).astype(BF16)
        d = acc_ref.shape[1]
        tc = min(d, 1024)
        for c0 in range(0, d, tc):
            acc_ref[:, c0:c0 + tc] += jnp.dot(act, wd_ref[:, c0:c0 + tc], preferred_element_type=F32)

    @pl.when(last)
    def _():
        _store_word_rows(o_ref, _pack_words(acc_ref[...]))


def group_ffn(xs3, w_gate, w_up, w_down, gates_sorted, tile_group, tile_valid, tm):
    n_rows = xs3.shape[0]
    row_words = xs3.shape[1:]
    n_e, d, df = w_gate.shape
    epg = n_e // N_GROUPS
    tf = _pick(df, (256, 128))
    words = lambda: pl.BlockSpec((tm,) + row_words, lambda r, j, f, tg, tv: (r, 0, 0))
    grid_spec = pltpu.PrefetchScalarGridSpec(
        num_scalar_prefetch=2, grid=(n_rows // tm, epg, df // tf),
        in_specs=[words(),
                  pl.BlockSpec((None, d, tf), lambda r, j, f, tg, tv: (tg[r] * epg + j, 0, f)),
                  pl.BlockSpec((None, d, tf), lambda r, j, f, tg, tv: (tg[r] * epg + j, 0, f)),
                  pl.BlockSpec((None, tf, d), lambda r, j, f, tg, tv: (tg[r] * epg + j, f, 0)),
                  pl.BlockSpec((tm, epg), lambda r, j, f, tg, tv: (r, 0))],
        out_specs=words(),
        scratch_shapes=[pltpu.VMEM((tm, d), F32), pltpu.VMEM((tm, d), BF16)])
    return pl.pallas_call(
        _group_ffn_kernel, grid_spec=grid_spec,
        out_shape=jax.ShapeDtypeStruct((n_rows,) + row_words, jnp.uint32),
        compiler_params=_cparams("parallel", "arbitrary", "arbitrary"),
        name="group_ffn",
    )(tile_group, tile_valid, xs3, w_gate, w_up, w_down, gates_sorted)


def _shared_ffn_kernel(x_ref, wg_ref, wu_ref, wd_ref, y_ref, res_ref, g2_ref, o_ref, *, tm, n_ctx):
    g2 = _row_select(pl.program_id(0) * tm, tm, n_ctx, g2_ref)

    @pl.when(pl.program_id(1) == 0)
    def _():
        y = jnp.concatenate(_unpack_words(_load_word_rows(y_ref)), axis=1)
        o_ref[...] = res_ref[...] + g2 * y

    x = x_ref[...]
    act = (_silu(jnp.dot(x, wg_ref[...], preferred_element_type=F32))
           * jnp.dot(x, wu_ref[...], preferred_element_type=F32)).astype(BF16)
    o_ref[...] += g2 * jnp.dot(act, wd_ref[...], preferred_element_type=F32)


def shared_ffn_combine(hn, w_gate, w_up, w_down, y_routed, resid, gate2, n_ctx):
    t, d = hn.shape
    df = w_gate.shape[1]
    tm = _pick(t, (256, 128))
    tf = df
    row = lambda: pl.BlockSpec((tm, d), lambda i, f: (i, 0))
    return pl.pallas_call(
        functools.partial(_shared_ffn_kernel, tm=tm, n_ctx=n_ctx), grid=(t // tm, df // tf),
        in_specs=[row(), pl.BlockSpec((d, tf), lambda i, f: (0, f)), pl.BlockSpec((d, tf), lambda i, f: (0, f)),
                  pl.BlockSpec((tf, d), lambda i, f: (f, 0)),
                  pl.BlockSpec((tm,) + y_routed.shape[1:], lambda i, f: (i, 0, 0)), row(),
                  pl.BlockSpec((2, d), lambda i, f: (0, 0))],
        out_specs=row(),
        out_shape=jax.ShapeDtypeStruct((t, d), F32),
        compiler_params=_cparams("parallel", "arbitrary"),
        name="shared_ffn_combine",
    )(hn, w_gate, w_up, w_down, y_routed, resid, gate2)


def _group_sort_plan(grp, tm):
    t = grp.shape[0]
    n_tiles = (t + N_GROUPS * (tm - 1)) // tm
    n_rows = n_tiles * tm
    onehot = (grp[:, None] == jnp.arange(N_GROUPS, dtype=jnp.int32)[None, :]).astype(jnp.int32)
    rank = jnp.sum((jnp.cumsum(onehot, axis=0) - onehot) * onehot, axis=1)
    counts = jnp.sum(onehot, axis=0)
    padded = (counts + tm - 1) // tm * tm
    ends = jnp.cumsum(padded)
    dest = (ends - padded)[grp] + rank
    src = (jnp.arange(n_rows, dtype=jnp.int32) % t).at[dest].set(jnp.arange(t, dtype=jnp.int32))
    row_valid = jnp.zeros((n_rows,), jnp.int32).at[dest].set(1)
    tile_start = jnp.arange(n_tiles, dtype=jnp.int32) * tm
    tile_group = jnp.minimum(jnp.sum((tile_start[:, None] >= ends[None, :]).astype(jnp.int32), axis=1),
                             N_GROUPS - 1)
    tile_valid = (tile_start < ends[-1]).astype(jnp.int32)
    return dest, src, row_valid, tile_group, tile_valid


def moe_layer(x, hn, hn_words, logits_t, router_b, p, gate2, n_ctx):
    t = hn.shape[0]
    n_e = logits_t.shape[0]
    epg = n_e // N_GROUPS
    tm = _pick(t, (640, 512, 256, 128))
    gates_t, grp = gating(logits_t, router_b)
    dest, src, row_valid, tile_group, tile_valid = _group_sort_plan(grp, tm)
    gate_cols = grp[:, None] * epg + jnp.arange(epg, dtype=jnp.int32)[None, :]
    gates_tok = jnp.take_along_axis(gates_t.T, gate_cols, axis=1)
    gates_sorted = jnp.where(row_valid[:, None] > 0, gates_tok[src], 0.0)
    xs = row_gather(hn_words, src, tm)
    ys = group_ffn(xs, p["w_gate"], p["w_up"], p["w_down"], gates_sorted, tile_group, tile_valid, tm)
    y_tok = row_gather(ys, dest, tm)
    return shared_ffn_combine(hn, p["ws_gate"], p["ws_up"], p["ws_down"], y_tok, x, gate2, n_ctx)


def kernel(x, c, ctx, c_ctx, ada_down, ada_up, ada_bias, norm1_g, norm2_g, final_norm_g, mla_w_in, mla_q_norm, mla_kv_norm, mla_w_uq, mla_w_ukv, mla_w_o, ssm_w_in, ssm_conv_w, ssm_conv_b, ssm_dt_bias, ssm_a_log, ssm_d, ssm_norm_g, ssm_w_out, router_w, router_b, moe_w_gate, moe_w_up, moe_w_down, shared_w_gate, shared_w_up, shared_w_down):
    assert x.shape[0] == 1 and ctx.shape[0] == 1
    depth = ada_down.shape[0]
    n_lat, d = x.shape[1], x.shape[2]
    n_ctx = ctx.shape[1]
    xs = jnp.concatenate([ctx[0], x[0]], axis=0).astype(F32)

    cvecs = jnp.zeros((8, d), F32).at[0].set(c_ctx).at[1].set(c[0])
    mods = ada_modulation(cvecs, ada_down, ada_up, ada_bias)[:, :2]
    mods = mods.reshape(depth, 2, N_MOD, d)
    cc, ss = _rope_tables(n_ctx, n_lat)
    router_wt = router_w.T

    for i in range(depth):
        sh1, sc1, g1, sh2, sc2, g2 = (mods[i, :, k] for k in range(N_MOD))
        j = i // 2
        hn = norm_modulate(xs, norm1_g[i], sh1, sc1, n_ctx)
        if i % 2 == 0:
            p = prep_mla_weights(mla_w_in[j], mla_q_norm[j], mla_kv_norm[j], mla_w_uq[j], mla_w_ukv[j], mla_w_o[j])
            xs = mla_layer(xs, hn, p, g1, n_ctx, cc, ss)
        else:
            p = prep_mamba_weights(ssm_w_in[j], ssm_conv_w[j], ssm_conv_b[j], ssm_dt_bias[j], ssm_a_log[j],
                                   ssm_d[j], ssm_norm_g[j], ssm_w_out[j])
            xs = mamba_layer(xs, hn, p, g1, n_ctx)
        hn2, hn2_words, logits_t = norm_modulate(xs, norm2_g[i], sh2, sc2, n_ctx, router_wt=router_wt)
        pm = dict(w_gate=moe_w_gate[i].astype(BF16), w_up=moe_w_up[i].astype(BF16),
                  w_down=moe_w_down[i].astype(BF16), ws_gate=shared_w_gate[i].astype(BF16),
                  ws_up=shared_w_up[i].astype(BF16), ws_down=shared_w_down[i].astype(BF16))
        xs = moe_layer(xs, hn2, hn2_words, logits_t, router_b, pm, g2, n_ctx)
    return final_norm(xs, final_norm_g, n_ctx)[None]
```
